```python
import math
import jax, jax.numpy as jnp
from jax import lax
import numpy as np

D_MODEL = 2048
BATCH = 4
SEQ = 2048
DEPTH = 1

GRID_W = 64
D_SSM = 1024
SSM_GROUP = 16
N_SSM_GROUPS = D_SSM // SSM_GROUP
SSM_STATE = 64
DT_MIN = 1e-3
DT_MAX = 1e-1
N_HEADS = 16
HEAD_DIM = 64
D_ATTN = N_HEADS * HEAD_DIM
WIN_H_MAX = 8
WIN_W = 16
D_MIX = D_SSM + D_ATTN
D_IN_PROJ = D_SSM + 3 * D_ATTN
N_EXPERTS = 16
CAPACITY_FACTOR = 2
D_FF_EXPERT = 2816
EPS = 1e-6

kernel_name = "hybrid_s5_natten_ecmoe_encoder"


def rms_norm(x, g):
    xf = x.astype(jnp.float32)
    y = xf * lax.rsqrt(jnp.mean(xf * xf, axis=-1, keepdims=True) + EPS)
    return (y * g.astype(jnp.float32)).astype(x.dtype)


def modulate(h, shift, scale):
    return h * (1.0 + scale[:, None, :]) + shift[:, None, :]


def _complex_linear_scan(a_re, a_im, b_re, b_im):
    def combine(e1, e2):
        a1r, a1i, b1r, b1i = e1
        a2r, a2i, b2r, b2i = e2
        ar = a1r * a2r - a1i * a2i
        ai = a1r * a2i + a1i * a2r
        br = a2r * b1r - a2i * b1i + b2r
        bi = a2r * b1i + a2i * b1r + b2i
        return (ar, ai, br, bi)
    _, _, xr, xi = lax.associative_scan(combine, (a_re, a_im, b_re, b_im), axis=1)
    return xr, xi


def s5_mixer(u, a_re, a_im, log_dt, b_re, b_im, c_re, c_im, d_skip, w_glu, b_glu):
    bsz, L, _ = u.shape
    ug = u.reshape(bsz, L, N_SSM_GROUPS, SSM_GROUP).astype(jnp.float32)
    y = (d_skip * u).astype(jnp.float32).reshape(bsz, L, N_SSM_GROUPS, SSM_GROUP)
    for direction in range(2):
        dt = jnp.exp(log_dt[direction].astype(jnp.float32))[:, None]
        lr = a_re[direction].astype(jnp.float32)
        li = a_im[direction].astype(jnp.float32)
        mag = jnp.exp(lr * dt)
        lbr = mag * jnp.cos(li * dt)
        lbi = mag * jnp.sin(li * dt)
        den = lr * lr + li * li
        nr = lbr - 1.0
        fr = (nr * lr + lbi * li) / den
        fi = (lbi * lr - nr * li) / den
        br = b_re[direction].astype(jnp.float32)
        bi = b_im[direction].astype(jnp.float32)
        bbr = fr[..., None] * br - fi[..., None] * bi
        bbi = fr[..., None] * bi + fi[..., None] * br
        bu_re = jnp.einsum('blgh,gph->blgp', ug, bbr)
        bu_im = jnp.einsum('blgh,gph->blgp', ug, bbi)
        if direction == 1:
            bu_re = jnp.flip(bu_re, axis=1)
            bu_im = jnp.flip(bu_im, axis=1)
        ar = jnp.broadcast_to(lbr[None, None], (1, L, N_SSM_GROUPS, SSM_STATE))
        ai = jnp.broadcast_to(lbi[None, None], (1, L, N_SSM_GROUPS, SSM_STATE))
        xr, xi = _complex_linear_scan(ar, ai, bu_re, bu_im)
        if direction == 1:
            xr = jnp.flip(xr, axis=1)
            xi = jnp.flip(xi, axis=1)
        y = y + jnp.einsum('blgp,ghp->blgh', xr, c_re[direction].astype(jnp.float32)) \
              - jnp.einsum('blgp,ghp->blgh', xi, c_im[direction].astype(jnp.float32))
    y = y.reshape(bsz, L, D_SSM).astype(u.dtype)
    z = jax.nn.gelu(y)
    return z * jax.nn.sigmoid(z @ w_glu + b_glu)


def neighbourhood_attention(q, k, v, rpb):
    bsz, L, H, Dh = q.shape
    rows = L // GRID_W
    kh = min(WIN_H_MAX, rows)
    r = jnp.arange(rows)
    row_start = jnp.clip(r - kh // 2, 0, rows - kh)
    row_idx = row_start[:, None] + jnp.arange(kh)[None, :]
    w = jnp.arange(GRID_W)
    col_start = jnp.clip(w - WIN_W // 2, 0, GRID_W - WIN_W)
    col_mask = (w[None, :] >= col_start[:, None]) & (w[None, :] < col_start[:, None] + WIN_W)

    qg = q.reshape(bsz, rows, GRID_W, H, Dh)
    kg = k.reshape(bsz, rows, GRID_W, H, Dh)[:, row_idx]
    vg = v.reshape(bsz, rows, GRID_W, H, Dh)[:, row_idx]
    s = jnp.einsum('brqhd,brjkhd->bhrqjk', qg, kg).astype(jnp.float32) * (Dh ** -0.5)
    dr = row_idx - r[:, None] + (WIN_H_MAX - 1)
    dc = jnp.clip(w[None, :] - w[:, None] + (WIN_W - 1), 0, 2 * WIN_W - 2)
    bias = rpb[:, dr[:, None, :, None], dc[None, :, None, :]].astype(jnp.float32)
    s = s + bias[None]
    s = jnp.where(col_mask[None, None, None, :, None, :], s, jnp.float32(-1e30))
    shp = s.shape
    p = jax.nn.softmax(s.reshape(shp[:-2] + (kh * GRID_W,)), axis=-1).reshape(shp)
    out = jnp.einsum('bhrqjk,brjkhd->brqhd', p.astype(v.dtype), vg)
    return out.reshape(bsz, L, H * Dh)


def expert_choice_ffn(h, w_router, w_gate, w_up, w_down):
    bsz, L, _ = h.shape
    cap = CAPACITY_FACTOR * L // N_EXPERTS
    logits = jnp.einsum('bld,de->ble', h, w_router).astype(jnp.float32)
    aff = jax.nn.softmax(logits, axis=-1)
    top_aff, top_idx = lax.top_k(jnp.swapaxes(aff, 1, 2), cap)
    b_idx = jnp.arange(bsz)[:, None, None]
    xe = h[b_idx, top_idx]
    hid = jax.nn.silu(jnp.einsum('becd,edf->becf', xe, w_gate)) * jnp.einsum('becd,edf->becf', xe, w_up)
    ye = jnp.einsum('becf,efd->becd', hid, w_down) * top_aff[..., None].astype(h.dtype)
    return jnp.zeros_like(h).at[b_idx, top_idx].add(ye)


def setup_inputs(seed: int = 0) -> dict:
    key = jax.random.key(seed)
    ks = jax.random.split(key, 32)
    f32 = jnp.float32

    def nrm(k, shape, s):
        return jax.random.normal(k, shape, f32) * s

    G, P, H = N_SSM_GROUPS, SSM_STATE, SSM_GROUP
    n = jnp.arange(P, dtype=f32)
    return {
        "x": nrm(ks[0], (BATCH, SEQ, D_MODEL), 1.0),
        "c": nrm(ks[1], (BATCH, D_MODEL), 1.0),
        "w_ada": nrm(ks[2], (DEPTH, D_MODEL, 6 * D_MODEL), 0.5 * D_MODEL ** -0.5),
        "b_ada": nrm(ks[3], (DEPTH, 6 * D_MODEL), 0.02),
        "g_mix": 1.0 + nrm(ks[4], (DEPTH, D_MODEL), 0.02),
        "w_in": nrm(ks[5], (DEPTH, D_MODEL, D_IN_PROJ), D_MODEL ** -0.5),
        "ssm_a_re": -0.5 + nrm(ks[6], (DEPTH, 2, G, P), 0.02),
        "ssm_a_im": math.pi * n + nrm(ks[7], (DEPTH, 2, G, P), 0.02),
        "ssm_log_dt": jax.random.uniform(ks[8], (DEPTH, 2, G), f32, math.log(DT_MIN), math.log(DT_MAX)),
        "ssm_b_re": nrm(ks[9], (DEPTH, 2, G, P, H), (2 * H) ** -0.5),
        "ssm_b_im": nrm(ks[10], (DEPTH, 2, G, P, H), (2 * H) ** -0.5),
        "ssm_c_re": nrm(ks[11], (DEPTH, 2, G, H, P), (2 * P) ** -0.5),
        "ssm_c_im": nrm(ks[12], (DEPTH, 2, G, H, P), (2 * P) ** -0.5),
        "ssm_d": nrm(ks[13], (DEPTH, D_SSM), 1.0),
        "w_glu": nrm(ks[14], (DEPTH, D_SSM, D_SSM), D_SSM ** -0.5),
        "b_glu": nrm(ks[15], (DEPTH, D_SSM), 0.02),
        "rpb": nrm(ks[16], (DEPTH, N_HEADS, 2 * WIN_H_MAX - 1, 2 * WIN_W - 1), 0.1),
        "g_ssm_out": 1.0 + nrm(ks[17], (DEPTH, D_SSM), 0.02),
        "g_attn_out": 1.0 + nrm(ks[18], (DEPTH, D_ATTN), 0.02),
        "w_out": nrm(ks[19], (DEPTH, D_MIX, D_MODEL), D_MIX ** -0.5),
        "g_ffn": 1.0 + nrm(ks[20], (DEPTH, D_MODEL), 0.02),
        "w_router": nrm(ks[21], (DEPTH, D_MODEL, N_EXPERTS), D_MODEL ** -0.5),
        "w_gate": nrm(ks[22], (DEPTH, N_EXPERTS, D_MODEL, D_FF_EXPERT), D_MODEL ** -0.5),
        "w_up": nrm(ks[23], (DEPTH, N_EXPERTS, D_MODEL, D_FF_EXPERT), D_MODEL ** -0.5),
        "w_down": nrm(ks[24], (DEPTH, N_EXPERTS, D_FF_EXPERT, D_MODEL), D_FF_EXPERT ** -0.5),
        "g_final": 1.0 + nrm(ks[25], (D_MODEL,), 0.02),
    }


def reference(x, c, w_ada, b_ada, g_mix, w_in, ssm_a_re, ssm_a_im, ssm_log_dt,
              ssm_b_re, ssm_b_im, ssm_c_re, ssm_c_im, ssm_d, w_glu, b_glu, rpb,
              g_ssm_out, g_attn_out, w_out, g_ffn, w_router, w_gate, w_up, w_down,
              g_final):
    bsz, L, _ = x.shape
    c_act = jax.nn.silu(c)
    for layer in range(DEPTH):
        mod = c_act @ w_ada[layer] + b_ada[layer]
        sh1, sc1, gt1, sh2, sc2, gt2 = jnp.split(mod, 6, axis=-1)

        h = modulate(rms_norm(x, g_mix[layer]), sh1, sc1)
        proj = h @ w_in[layer]
        u, q, k, v = jnp.split(proj, [D_SSM, D_SSM + D_ATTN, D_SSM + 2 * D_ATTN], axis=-1)
        y_ssm = s5_mixer(u, ssm_a_re[layer], ssm_a_im[layer], ssm_log_dt[layer],
                         ssm_b_re[layer], ssm_b_im[layer], ssm_c_re[layer], ssm_c_im[layer],
                         ssm_d[layer], w_glu[layer], b_glu[layer])
        y_attn = neighbourhood_attention(q.reshape(bsz, L, N_HEADS, HEAD_DIM),
                                         k.reshape(bsz, L, N_HEADS, HEAD_DIM),
                                         v.reshape(bsz, L, N_HEADS, HEAD_DIM), rpb[layer])
        mixed = jnp.concatenate([rms_norm(y_ssm, g_ssm_out[layer]),
                                 rms_norm(y_attn, g_attn_out[layer])], axis=-1)
        x = x + gt1[:, None, :] * (mixed @ w_out[layer])

        h = modulate(rms_norm(x, g_ffn[layer]), sh2, sc2)
        x = x + gt2[:, None, :] * expert_choice_ffn(h, w_router[layer], w_gate[layer],
                                                     w_up[layer], w_down[layer])
    return rms_norm(x, g_final)
```

```python
import functools

import jax
import jax.numpy as jnp
from jax import lax
from jax.experimental import pallas as pl
from jax.experimental.pallas import tpu as pltpu

F32 = jnp.float32
BF16 = jnp.bfloat16
I32 = jnp.int32

EPS = 1e-6
GRID_W = 64
SSM_GROUP = 16
SSM_STATE = 64
N_HEADS = 16
HEAD_DIM = 64
WIN_H_MAX = 8
WIN_W = 16
N_EXPERTS = 16
CAPACITY_FACTOR = 2
SSM_T = 16
NEG = -1e30

VMEM_LIMIT = 56 * 1024 * 1024


def _params(sem, vmem=VMEM_LIMIT):
    return pltpu.CompilerParams(dimension_semantics=sem, vmem_limit_bytes=vmem)


def _sigmoid(x):
    return 1.0 / (1.0 + jnp.exp(-x))


def _rms(x, g):
    return x * lax.rsqrt(jnp.mean(x * x, axis=-1, keepdims=True) + EPS) * g


def _ada_kernel(c_ref, w_ref, b_ref, o_ref):
    c = c_ref[...]
    ca = (c * _sigmoid(c)).astype(BF16)
    o_ref[...] = jnp.dot(ca, w_ref[...].astype(BF16), preferred_element_type=F32) + b_ref[...]


def _ada(c8, w, b):
    k, n = w.shape
    tn = 1024
    return pl.pallas_call(
        _ada_kernel,
        grid=(n // tn,),
        in_specs=[pl.BlockSpec((8, k), lambda j: (0, 0)),
                  pl.BlockSpec((k, tn), lambda j: (0, j)),
                  pl.BlockSpec((1, tn), lambda j: (0, j))],
        out_specs=pl.BlockSpec((8, tn), lambda j: (0, j)),
        out_shape=jax.ShapeDtypeStruct((8, n), F32),
        compiler_params=_params(("arbitrary",)),
        name="ada",
    )(c8, w, b)


def _cast_kernel(w_ref, o_ref):
    o_ref[...] = w_ref[...].astype(BF16)


def _cast_bf16(w):
    k, n = w.shape
    tk = 256
    return pl.pallas_call(
        _cast_kernel,
        grid=(k // tk,),
        in_specs=[pl.BlockSpec((tk, n), lambda i: (i, 0))],
        out_specs=pl.BlockSpec((tk, n), lambda i: (i, 0)),
        out_shape=jax.ShapeDtypeStruct((k, n), BF16),
        compiler_params=_params(("arbitrary",)),
        name="cast",
    )(w)


def _inproj_kernel(x_ref, sh_ref, sc_ref, g_ref, w_ref, u_ref, qkv_ref, *, d_ssm):
    h = (_rms(x_ref[0], g_ref[...]) * (1.0 + sc_ref[0]) + sh_ref[0]).astype(BF16)
    n_total = w_ref.shape[1]
    for n in range(n_total // d_ssm):
        r = jnp.dot(h, w_ref[:, n * d_ssm:(n + 1) * d_ssm], preferred_element_type=F32)
        if n == 0:
            u_ref[0] = r
        else:
            qkv_ref[0, :, (n - 1) * d_ssm:n * d_ssm] = r.astype(BF16)


def _inproj(x, sh, sc, g, w_bf, d_ssm):
    bsz, L, D = x.shape
    n = w_bf.shape[1]
    tm = 256
    return pl.pallas_call(
        functools.partial(_inproj_kernel, d_ssm=d_ssm),
        grid=(bsz, L // tm),
        in_specs=[pl.BlockSpec((1, tm, D), lambda b, i: (b, i, 0)),
                  pl.BlockSpec((1, 1, D), lambda b, i: (b, 0, 0)),
                  pl.BlockSpec((1, 1, D), lambda b, i: (b, 0, 0)),
                  pl.BlockSpec((1, D), lambda b, i: (0, 0)),
                  pl.BlockSpec((D, n), lambda b, i: (0, 0))],
        out_specs=[pl.BlockSpec((1, tm, d_ssm), lambda b, i: (b, i, 0)),
                   pl.BlockSpec((1, tm, n - d_ssm), lambda b, i: (b, i, 0))],
        out_shape=[jax.ShapeDtypeStruct((bsz, L, d_ssm), F32),
                   jax.ShapeDtypeStruct((bsz, L, n - d_ssm), BF16)],
        compiler_params=_params(("arbitrary", "arbitrary")),
        name="inproj",
    )(x, sh, sc, g, w_bf)


def _ssm_operators(a_re, a_im, log_dt, b_re, b_im, c_re, c_im):
    T = SSM_T
    hi = lax.Precision.HIGHEST
    G, P = a_re.shape[1], a_re.shape[2]
    H = b_re.shape[3]
    dt = jnp.exp(log_dt)[..., None]
    mag = jnp.exp(a_re * dt)
    lbr = mag * jnp.cos(a_im * dt)
    lbi = mag * jnp.sin(a_im * dt)
    den = a_re * a_re + a_im * a_im
    nr = lbr - 1.0
    fr = (nr * a_re + lbi * a_im) / den
    fi = (lbi * a_re - nr * a_im) / den
    bbr = fr[..., None] * b_re - fi[..., None] * b_im
    bbi = fr[..., None] * b_im + fi[..., None] * b_re
    prs, pis = [jnp.ones_like(lbr)], [jnp.zeros_like(lbi)]
    for _ in range(T):
        pr, pi = prs[-1], pis[-1]
        prs.append(pr * lbr - pi * lbi)
        pis.append(pr * lbi + pi * lbr)
    pr = jnp.stack(prs)
    pi = jnp.stack(pis)
    cpr = c_re[None] * pr[:, :, :, None, :] - c_im[None] * pi[:, :, :, None, :]
    cpi = c_re[None] * pi[:, :, :, None, :] + c_im[None] * pr[:, :, :, None, :]
    kern = (jnp.einsum('kdghp,dgpj->kdghj', cpr[:T], bbr, precision=hi)
            - jnp.einsum('kdghp,dgpj->kdghj', cpi[:T], bbi, precision=hi))
    s_idx = jnp.arange(T)[:, None]
    t_idx = jnp.arange(T)[None, :]
    lag_f = jnp.clip(t_idx - s_idx, 0, T - 1)
    lag_b = jnp.clip(s_idx - t_idx, 0, T - 1)
    kf = jnp.where((t_idx >= s_idx)[:, :, None, None, None], kern[:, 0][lag_f], 0.0)
    kb = jnp.where((s_idx >= t_idx)[:, :, None, None, None], kern[:, 1][lag_b], 0.0)
    m = jnp.transpose(kf + kb, (2, 0, 4, 1, 3)).reshape(G, T * H, T * H)
    pf_r, pf_i = pr[:T, 0][::-1], pi[:T, 0][::-1]
    pb_r, pb_i = pr[:T, 1], pi[:T, 1]
    wf_r = pf_r[..., None] * bbr[0][None] - pf_i[..., None] * bbi[0][None]
    wf_i = pf_r[..., None] * bbi[0][None] + pf_i[..., None] * bbr[0][None]
    wb_r = pb_r[..., None] * bbr[1][None] - pb_i[..., None] * bbi[1][None]
    wb_i = pb_r[..., None] * bbi[1][None] + pb_i[..., None] * bbr[1][None]
    w4 = jnp.stack([wf_r, wf_i, wb_r, wb_i])
    w4 = jnp.transpose(w4, (2, 1, 4, 0, 3)).reshape(G // 2, 2, T * H, 4, P)
    eye2 = jnp.eye(2, dtype=F32)
    wpair = jnp.einsum('agrqp,gk->agrqkp', w4, eye2).reshape(G // 2, 2 * T * H, 8 * P)
    vf_r = cpr[1:, 0]
    vf_i = -cpi[1:, 0]
    vb_r = cpr[1:, 1][::-1]
    vb_i = -cpi[1:, 1][::-1]
    v4 = jnp.stack([vf_r, vf_i, vb_r, vb_i])
    v4 = jnp.transpose(v4, (2, 0, 4, 1, 3)).reshape(G // 2, 2, 4, P, T * H)
    vpair = jnp.einsum('agqpc,gk->aqgpkc', v4, eye2).reshape(G // 2, 8 * P, 2 * T * H)
    a4 = jnp.stack([pr[T, 0], pi[T, 0], pr[T, 1], pi[T, 1]])
    a_chunk = jnp.transpose(a4.reshape(4, G // 2, 2, P), (1, 0, 2, 3)).reshape(1, G // 2 * 8 * P)
    return m.astype(BF16), wpair.astype(BF16), vpair.astype(BF16), a_chunk


def _ssm_a_kernel(u_ref, w_ref, s_ref):
    u2 = jnp.concatenate([u_ref[0], u_ref[1]], axis=-1)
    s_ref[...] = jnp.dot(u2, w_ref[0], preferred_element_type=F32)


def _ssm_a(ug, wpair):
    G, R, K = ug.shape
    n = wpair.shape[2]
    return pl.pallas_call(
        _ssm_a_kernel,
        grid=(G // 2,),
        in_specs=[pl.BlockSpec((2, R, K), lambda j: (j, 0, 0)),
                  pl.BlockSpec((1, 2 * K, n), lambda j: (j, 0, 0))],
        out_specs=pl.BlockSpec((R, n), lambda j: (0, j)),
        out_shape=jax.ShapeDtypeStruct((R, G // 2 * n), F32),
        compiler_params=_params(("arbitrary",)),
        name="ssm_a",
    )(ug, wpair)


def _ssm_scan_kernel(s_ref, a_ref, x_ref, *, bsz, n_pairs, lanes):
    rows = s_ref.shape[0]
    n_it = rows // (2 * bsz)
    for j in range(n_pairs):
        base = j * 4 * lanes
        cols = [pl.ds(base + q * lanes, lanes) for q in range(4)]
        ar_f, ai_f, ar_b, ai_b = [a_ref[:, c] for c in cols]

        def fwd(i, carry):
            xr, xi = carry
            r0 = pl.multiple_of(i * 2 * bsz, 2 * bsz)
            sr = s_ref[pl.ds(r0, 2 * bsz), cols[0]]
            si = s_ref[pl.ds(r0, 2 * bsz), cols[1]]
            xr1 = ar_f * xr - ai_f * xi + sr[:bsz]
            xi1 = ar_f * xi + ai_f * xr + si[:bsz]
            x_ref[pl.ds(r0, 2 * bsz), cols[0]] = jnp.concatenate([xr, xr1], axis=0)
            x_ref[pl.ds(r0, 2 * bsz), cols[1]] = jnp.concatenate([xi, xi1], axis=0)
            xr2 = ar_f * xr1 - ai_f * xi1 + sr[bsz:]
            xi2 = ar_f * xi1 + ai_f * xr1 + si[bsz:]
            return xr2, xi2

        def bwd(i, carry):
            xr, xi = carry
            r0 = pl.multiple_of((n_it - 1 - i) * 2 * bsz, 2 * bsz)
            sr = s_ref[pl.ds(r0, 2 * bsz), cols[2]]
            si = s_ref[pl.ds(r0, 2 * bsz), cols[3]]
            xr1 = ar_b * xr - ai_b * xi + sr[bsz:]
            xi1 = ar_b * xi + ai_b * xr + si[bsz:]
            x_ref[pl.ds(r0, 2 * bsz), cols[2]] = jnp.concatenate([xr1, xr], axis=0)
            x_ref[pl.ds(r0, 2 * bsz), cols[3]] = jnp.concatenate([xi1, xi], axis=0)
            xr2 = ar_b * xr1 - ai_b * xi1 + sr[:bsz]
            xi2 = ar_b * xi1 + ai_b * xr1 + si[:bsz]
            return xr2, xi2

        z = jnp.zeros((bsz, lanes), F32)
        lax.fori_loop(0, n_it, fwd, (z, z))
        lax.fori_loop(0, n_it, bwd, (z, z))


def _ssm_scan(s, a_chunk, bsz):
    R, N = s.shape
    pairs_per_step = 4
    lanes = 128
    tn = pairs_per_step * 4 * lanes
    return pl.pallas_call(
        functools.partial(_ssm_scan_kernel, bsz=bsz, n_pairs=pairs_per_step, lanes=lanes),
        grid=(N // tn,),
        in_specs=[pl.BlockSpec((R, tn), lambda j: (0, j)),
                  pl.BlockSpec((1, tn), lambda j: (0, j))],
        out_specs=pl.BlockSpec((R, tn), lambda j: (0, j)),
        out_shape=jax.ShapeDtypeStruct((R, N), F32),
        compiler_params=_params(("arbitrary",)),
        name="ssm_scan",
    )(s, a_chunk)


def _ssm_b_kernel(u_ref, m_ref, x_ref, v_ref, y_ref):
    k = u_ref.shape[2]
    inter = jnp.dot(x_ref[...].astype(BF16), v_ref[0], preferred_element_type=F32)
    for gl in range(2):
        intra = jnp.dot(u_ref[gl], m_ref[gl], preferred_element_type=F32)
        y_ref[gl] = intra + inter[:, gl * k:(gl + 1) * k]


def _ssm_b(ug, m, xin, vpair):
    G, R, K = ug.shape
    n = vpair.shape[1]
    return pl.pallas_call(
        _ssm_b_kernel,
        grid=(G // 2,),
        in_specs=[pl.BlockSpec((2, R, K), lambda j: (j, 0, 0)),
                  pl.BlockSpec((2, K, K), lambda j: (j, 0, 0)),
                  pl.BlockSpec((R, n), lambda j: (0, j)),
                  pl.BlockSpec((1, n, 2 * K), lambda j: (j, 0, 0))],
        out_specs=pl.BlockSpec((2, R, K), lambda j: (j, 0, 0)),
        out_shape=jax.ShapeDtypeStruct((G, R, K), F32),
        compiler_params=_params(("arbitrary",)),
        name="ssm_b",
    )(ug, m, xin, vpair)


def _ssm_core(u, ops):
    m, wpair, vpair, a_chunk = ops
    bsz, L, d_ssm = u.shape
    T, H = SSM_T, SSM_GROUP
    G, C = d_ssm // H, L // T
    ug = jnp.transpose(u.reshape(bsz, C, T, G, H), (3, 1, 0, 2, 4)).reshape(G, C * bsz, T * H).astype(BF16)
    s = _ssm_a(ug, wpair)
    xin = _ssm_scan(s, a_chunk, bsz)
    yg = _ssm_b(ug, m, xin, vpair)
    return jnp.transpose(yg.reshape(G, C, bsz, T, H), (2, 1, 3, 0, 4)).reshape(bsz, L, d_ssm)


def _attn_bias(rpb, rows):
    kh = min(WIN_H_MAX, rows)
    w = jnp.arange(GRID_W)
    col_start = jnp.clip(w - WIN_W // 2, 0, GRID_W - WIN_W)
    col_mask = (w[None, :] >= col_start[:, None]) & (w[None, :] < col_start[:, None] + WIN_W)
    dc = jnp.clip(w[None, :] - w[:, None] + (WIN_W - 1), 0, 2 * WIN_W - 2)
    pat = jnp.arange(kh)
    dr = jnp.arange(kh)[None, :] - pat[:, None] + (WIN_H_MAX - 1)
    bias = rpb[:, dr[:, None, :, None], dc[None, :, None, :]]
    bias = jnp.where(col_mask[None, None, :, None, :], bias, NEG)
    return bias.reshape(rpb.shape[0], kh, GRID_W, kh * GRID_W).astype(F32)


def _attn_kernel(q_ref, k_ref, v_ref, b_ref, o_ref, *, rows, kh):
    r = pl.program_id(1)
    row_start = jnp.clip(r - kh // 2, 0, rows - kh)
    start = pl.multiple_of(row_start * GRID_W, GRID_W)
    nk = kh * GRID_W
    for h in range(N_HEADS):
        sl = slice(h * HEAD_DIM, (h + 1) * HEAD_DIM)
        qh = q_ref[0, :, sl]
        kh_ = k_ref[0, pl.ds(start, nk), sl]
        vh = v_ref[0, pl.ds(start, nk), sl]
        s = lax.dot_general(qh, kh_, (((1,), (1,)), ((), ())), preferred_element_type=F32)
        s = s * (HEAD_DIM ** -0.5) + b_ref[h, 0]
        p = jnp.exp(s - jnp.max(s, axis=-1, keepdims=True))
        denom = jnp.sum(p, axis=-1, keepdims=True)
        o = jnp.dot(p.astype(BF16), vh, preferred_element_type=F32)
        o_ref[0, :, sl] = o / denom


def _attn(qkv, bias):
    bsz, L, n3 = qkv.shape
    d = n3 // 3
    rows = L // GRID_W
    kh = min(WIN_H_MAX, rows)

    def pat(r):
        return r - jnp.clip(r - kh // 2, 0, rows - kh)

    return pl.pallas_call(
        functools.partial(_attn_kernel, rows=rows, kh=kh),
        grid=(bsz, rows),
        in_specs=[pl.BlockSpec((1, GRID_W, d), lambda b, r: (b, r, 0)),
                  pl.BlockSpec((1, L, d), lambda b, r: (b, 0, 1)),
                  pl.BlockSpec((1, L, d), lambda b, r: (b, 0, 2)),
                  pl.BlockSpec((N_HEADS, 1, GRID_W, kh * GRID_W), lambda b, r: (0, pat(r), 0, 0))],
        out_specs=pl.BlockSpec((1, GRID_W, d), lambda b, r: (b, r, 0)),
        out_shape=jax.ShapeDtypeStruct((bsz, L, d), F32),
        compiler_params=_params(("arbitrary", "arbitrary")),
        name="attn",
    )(qkv, qkv, qkv, bias)


def _split_dot(a, w_hi, w_lo):
    a_hi = a.astype(BF16)
    a_lo = (a - a_hi.astype(F32)).astype(BF16)
    return (jnp.dot(a_hi, w_hi, preferred_element_type=F32)
            + jnp.dot(a_lo, w_hi, preferred_element_type=F32)
            + jnp.dot(a_hi, w_lo, preferred_element_type=F32))


def _mix_kernel(ys_ref, u_ref, d_ref, wglu_ref, bglu_ref, gs_ref, ya_ref, ga_ref, wout_ref,
                x_ref, gt_ref, gf_ref, sh_ref, sc_ref, wr_ref,
                x1_ref, h2_ref, lg_ref):
    d_ssm = ys_ref.shape[2]
    y = d_ref[...] * u_ref[0] + ys_ref[0]
    z = jax.nn.gelu(y)
    gate = _sigmoid(jnp.dot(z.astype(BF16), wglu_ref[...], preferred_element_type=F32) + bglu_ref[...])
    a = _rms(z * gate, gs_ref[...]).astype(BF16)
    t = _rms(ya_ref[0], ga_ref[...]).astype(BF16)
    mixed = (jnp.dot(a, wout_ref[:d_ssm, :], preferred_element_type=F32)
             + jnp.dot(t, wout_ref[d_ssm:, :], preferred_element_type=F32))
    x1 = x_ref[0] + gt_ref[0] * mixed
    x1_ref[0] = x1
    h2 = _rms(x1, gf_ref[...]) * (1.0 + sc_ref[0]) + sh_ref[0]
    h2_ref[0] = h2.astype(BF16)
    wr = wr_ref[...]
    wr_hi = wr.astype(BF16)
    wr_lo = (wr - wr_hi.astype(F32)).astype(BF16)
    lg_ref[0] = _split_dot(h2, wr_hi, wr_lo)


def _mix(ys, u, d_skip, wglu_bf, b_glu, g_ssm, ya, g_attn, wout_bf, x, gt1, g_ffn, sh2, sc2, w_router):
    bsz, L, D = x.shape
    d_ssm = ys.shape[2]
    E = w_router.shape[1]
    tm = 256
    tile = lambda n: pl.BlockSpec((1, tm, n), lambda b, i: (b, i, 0))
    per_b = lambda n: pl.BlockSpec((1, 1, n), lambda b, i: (b, 0, 0))
    full = lambda r, n: pl.BlockSpec((r, n), lambda b, i: (0, 0))
    return pl.pallas_call(
        _mix_kernel,
        grid=(bsz, L // tm),
        in_specs=[tile(d_ssm), tile(d_ssm), full(1, d_ssm), full(d_ssm, d_ssm), full(1, d_ssm), full(1, d_ssm),
                  tile(d_ssm), full(1, d_ssm), full(D, D),
                  tile(D), per_b(D), full(1, D), per_b(D), per_b(D), full(D, E)],
        out_specs=[tile(D), tile(D), tile(E)],
        out_shape=[jax.ShapeDtypeStruct((bsz, L, D), F32),
                   jax.ShapeDtypeStruct((bsz, L, D), BF16),
                   jax.ShapeDtypeStruct((bsz, L, E), F32)],
        compiler_params=_params(("arbitrary", "arbitrary")),
        name="mix",
    )(ys, u, d_skip, wglu_bf, b_glu, g_ssm, ya, g_attn, wout_bf, x, gt1, g_ffn, sh2, sc2, w_router)


def _topk_kernel(lg_ref, aff_ref, pos_ref, *, cap):
    lg = lg_ref[...]
    bsz, E, L = lg.shape
    e = jnp.exp(lg - jnp.max(lg, axis=1, keepdims=True))
    aff = e / jnp.sum(e, axis=1, keepdims=True)
    aff_ref[...] = aff
    aff2 = aff.reshape(bsz * E, L)
    bits = pltpu.bitcast(aff2, I32)

    def count(mask):
        return jnp.sum(jnp.where(mask, 1.0, 0.0), axis=-1, keepdims=True)

    thr = jnp.zeros((bsz * E, 1), I32)
    for bit in range(30, -1, -1):
        cand = thr | (1 << bit)
        thr = jnp.where(count(bits >= cand) >= cap, cand, thr)
    gt = bits > thr
    eq = bits == thr
    need = cap - count(gt)
    blk = 128
    tri = jnp.where(lax.broadcasted_iota(I32, (blk, blk), 0) < lax.broadcasted_iota(I32, (blk, blk), 1),
                    1.0, 0.0).astype(BF16)

    def prefix_count(mask):
        ones = jnp.where(mask, 1.0, 0.0)
        run = jnp.zeros((bsz * E, 1), F32)
        outs = []
        for j in range(L // blk):
            piece = ones[:, j * blk:(j + 1) * blk]
            outs.append(jnp.dot(piece.astype(BF16), tri, preferred_element_type=F32) + run)
            run = run + jnp.sum(piece, axis=-1, keepdims=True)
        return jnp.concatenate(outs, axis=-1)

    sel = gt | (eq & (prefix_count(eq) < need))
    pos = prefix_count(sel)
    pos_ref[...] = jnp.where(sel, pos.astype(I32), -1).reshape(bsz, E, L)


def _topk(lg_t, cap):
    bsz, E, L = lg_t.shape
    spec = pl.BlockSpec((bsz, E, L), lambda i: (0, 0, 0))
    return pl.pallas_call(
        functools.partial(_topk_kernel, cap=cap),
        grid=(1,),
        in_specs=[spec],
        out_specs=[spec, spec],
        out_shape=[jax.ShapeDtypeStruct((bsz, E, L), F32), jax.ShapeDtypeStruct((bsz, E, L), I32)],
        compiler_params=_params(("arbitrary",)),
        name="topk",
    )(lg_t)


def _gather_kernel(pos_ref, aff_ref, h_ref, xe_ref, as_ref, *, cap):
    L = h_ref.shape[1]
    slot = lax.broadcasted_iota(I32, (cap, L), 0)
    hit = pos_ref[0, 0] == slot
    xe_ref[0] = jnp.dot(jnp.where(hit, 1.0, 0.0).astype(BF16), h_ref[0],
                        preferred_element_type=F32).astype(BF16)
    as_ref[0] = jnp.sum(jnp.where(hit, aff_ref[0, 0], 0.0), axis=-1, keepdims=True)


def _gather(pos, aff, h2, cap):
    bsz, E, L = pos.shape
    D = h2.shape[2]
    pos4 = pos.reshape(bsz, E, 1, L)
    aff4 = aff.reshape(bsz, E, 1, L)
    return pl.pallas_call(
        functools.partial(_gather_kernel, cap=cap),
        grid=(bsz, E),
        in_specs=[pl.BlockSpec((1, 1, 1, L), lambda b, e: (b, e, 0, 0)),
                  pl.BlockSpec((1, 1, 1, L), lambda b, e: (b, e, 0, 0)),
                  pl.BlockSpec((1, L, D), lambda b, e: (b, 0, 0))],
        out_specs=[pl.BlockSpec((1, cap, D), lambda b, e: (e, b, 0)),
                   pl.BlockSpec((1, cap, 1), lambda b, e: (e, b, 0))],
        out_shape=[jax.ShapeDtypeStruct((E, bsz * cap, D), BF16),
                   jax.ShapeDtypeStruct((E, bsz * cap, 1), F32)],
        compiler_params=_params(("arbitrary", "arbitrary")),
        name="gather",
    )(pos4, aff4, h2)


def _ffn_kernel(xe_ref, wg_ref, wu_ref, wd_ref, as_ref, ye_ref, acc_ref):
    f = pl.program_id(1)

    @pl.when(f == 0)
    def _():
        acc_ref[...] = jnp.zeros_like(acc_ref)

    x = xe_ref[0]
    g = jnp.dot(x, wg_ref[0].astype(BF16), preferred_element_type=F32)
    up = jnp.dot(x, wu_ref[0].astype(BF16), preferred_element_type=F32)
    hid = (g * _sigmoid(g) * up).astype(BF16)
    acc_ref[...] += jnp.dot(hid, wd_ref[0].astype(BF16), preferred_element_type=F32)

    @pl.when(f == pl.num_programs(1) - 1)
    def _():
        ye_ref[0] = (acc_ref[...] * as_ref[0]).astype(BF16)


def _ffn(xe, w_gate, w_up, w_down, aff_slot):
    E, R, D = xe.shape
    F = w_gate.shape[2]
    tf = 256
    return pl.pallas_call(
        _ffn_kernel,
        grid=(E, F // tf),
        in_specs=[pl.BlockSpec((1, R, D), lambda e, f: (e, 0, 0)),
                  pl.BlockSpec((1, D, tf), lambda e, f: (e, 0, f)),
                  pl.BlockSpec((1, D, tf), lambda e, f: (e, 0, f)),
                  pl.BlockSpec((1, tf, D), lambda e, f: (e, f, 0)),
                  pl.BlockSpec((1, R, 1), lambda e, f: (e, 0, 0))],
        out_specs=pl.BlockSpec((1, R, D), lambda e, f: (e, 0, 0)),
        out_shape=jax.ShapeDtypeStruct((E, R, D), BF16),
        scratch_shapes=[pltpu.VMEM((R, D), F32)],
        compiler_params=_params(("arbitrary", "arbitrary")),
        name="ffn",
    )(xe, w_gate, w_up, w_down, aff_slot)


def _combine_kernel(pos_ref, ye_ref, x1_ref, gt_ref, g_ref, o_ref, acc_ref, *, cap, final_norm):
    e = pl.program_id(2)

    @pl.when(e == 0)
    def _():
        acc_ref[...] = jnp.zeros_like(acc_ref)

    pos_t = pos_ref[0]
    lane = lax.broadcasted_iota(I32, pos_t.shape, 1)
    pcol = jnp.sum(jnp.where(lane == e, pos_t, 0).astype(F32), axis=-1, keepdims=True).astype(I32)
    slot = lax.broadcasted_iota(I32, (pos_t.shape[0], cap), 1)
    onehot = jnp.where(pcol == slot, 1.0, 0.0).astype(BF16)
    acc_ref[...] += jnp.dot(onehot, ye_ref[0], preferred_element_type=F32)

    @pl.when(e == pl.num_programs(2) - 1)
    def _():
        o = x1_ref[0] + gt_ref[0] * acc_ref[...]
        o_ref[0] = _rms(o, g_ref[...]) if final_norm else o


def _combine(pos_t, ye, x1, gt2, g_final, cap, final_norm):
    bsz, L, E = pos_t.shape
    D = x1.shape[2]
    tl = 1024
    return pl.pallas_call(
        functools.partial(_combine_kernel, cap=cap, final_norm=final_norm),
        grid=(bsz, L // tl, E),
        in_specs=[pl.BlockSpec((1, tl, E), lambda b, i, e: (b, i, 0)),
                  pl.BlockSpec((1, cap, D), lambda b, i, e: (e, b, 0)),
                  pl.BlockSpec((1, tl, D), lambda b, i, e: (b, i, 0)),
                  pl.BlockSpec((1, 1, D), lambda b, i, e: (b, 0, 0)),
                  pl.BlockSpec((1, D), lambda b, i, e: (0, 0))],
        out_specs=pl.BlockSpec((1, tl, D), lambda b, i, e: (b, i, 0)),
        out_shape=jax.ShapeDtypeStruct((bsz, L, D), F32),
        scratch_shapes=[pltpu.VMEM((tl, D), F32)],
        compiler_params=_params(("arbitrary", "arbitrary", "arbitrary")),
        name="combine",
    )(pos_t, ye, x1, gt2, g_final)


def kernel(x, c, w_ada, b_ada, g_mix, w_in, ssm_a_re, ssm_a_im, ssm_log_dt, ssm_b_re, ssm_b_im,
           ssm_c_re, ssm_c_im, ssm_d, w_glu, b_glu, rpb, g_ssm_out, g_attn_out, w_out, g_ffn,
           w_router, w_gate, w_up, w_down, g_final):
    bsz, L, D = x.shape
    depth = w_ada.shape[0]
    d_ssm = ssm_d.shape[1]
    E = w_router.shape[2]
    cap = CAPACITY_FACTOR * L // E
    rows = L // GRID_W
    c8 = jnp.zeros((8, D), F32).at[:bsz].set(c)
    row = lambda v: v.reshape(1, -1)
    for layer in range(depth):
        mod = _ada(c8, w_ada[layer], row(b_ada[layer]))[:bsz]
        sh1, sc1, gt1, sh2, sc2, gt2 = [m.reshape(bsz, 1, D) for m in jnp.split(mod, 6, axis=-1)]

        u, qkv = _inproj(x, sh1, sc1, row(g_mix[layer]), _cast_bf16(w_in[layer]), d_ssm)
        ops = _ssm_operators(ssm_a_re[layer], ssm_a_im[layer], ssm_log_dt[layer], ssm_b_re[layer],
                             ssm_b_im[layer], ssm_c_re[layer], ssm_c_im[layer])
        y_ssm = _ssm_core(u, ops)
        y_attn = _attn(qkv, _attn_bias(rpb[layer], rows))
        x1, h2, logits = _mix(y_ssm, u, row(ssm_d[layer]), _cast_bf16(w_glu[layer]), row(b_glu[layer]),
                              row(g_ssm_out[layer]), y_attn, row(g_attn_out[layer]), _cast_bf16(w_out[layer]),
                              x, gt1, row(g_ffn[layer]), sh2, sc2, w_router[layer])

        aff, pos = _topk(jnp.swapaxes(logits, 1, 2), cap)
        xe, aff_slot = _gather(pos, aff, h2, cap)
        ye = _ffn(xe, w_gate[layer], w_up[layer], w_down[layer], aff_slot)
        x = _combine(jnp.swapaxes(pos, 1, 2), ye, x1, gt2, row(g_final), cap, layer == depth - 1)
    return x
```

```python
import functools

import jax
import jax.numpy as jnp
from jax import lax
from jax.experimental import pallas as pl
from jax.experimental.pallas import tpu as pltpu

F32 = jnp.float32
BF16 = jnp.bfloat16
I32 = jnp.int32

EPS = 1e-6
GRID_W = 64
SSM_GROUP = 16
SSM_STATE = 64
N_HEADS = 16
HEAD_DIM = 64
WIN_H_MAX = 8
WIN_W = 16
N_EXPERTS = 16
CAPACITY_FACTOR = 2
SSM_T = 16
LANES = 128
NEG = -1e30

ATT_ROWS = 4
ATT_WIN = 12

VMEM_LIMIT = 56 * 1024 * 1024


def _params(sem, vmem=VMEM_LIMIT):
    return pltpu.CompilerParams(dimension_semantics=sem, vmem_limit_bytes=vmem)


def _sigmoid(x):
    return 1.0 / (1.0 + jnp.exp(-x))


def _rms(x, g):
    return x * lax.rsqrt(jnp.mean(x * x, axis=-1, keepdims=True) + EPS) * g


def _ada_kernel(c_ref, w_ref, b_ref, o_ref):
    c = c_ref[...]
    ca = (c * _sigmoid(c)).astype(BF16)
    o_ref[...] = jnp.dot(ca, w_ref[...].astype(BF16), preferred_element_type=F32) + b_ref[...]


def _ada(c8, w, b):
    k, n = w.shape
    tn = 1024
    return pl.pallas_call(
        _ada_kernel,
        grid=(n // tn,),
        in_specs=[pl.BlockSpec((8, k), lambda j: (0, 0)),
                  pl.BlockSpec((k, tn), lambda j: (0, j)),
                  pl.BlockSpec((1, tn), lambda j: (0, j))],
        out_specs=pl.BlockSpec((8, tn), lambda j: (0, j)),
        out_shape=jax.ShapeDtypeStruct((8, n), F32),
        compiler_params=_params(("arbitrary",)),
        name="ada",
    )(c8, w, b)


def _cast_kernel(w_ref, o_ref):
    o_ref[...] = w_ref[...].astype(BF16)


def _cast_bf16(w):
    k, n = w.shape
    tk = 256
    return pl.pallas_call(
        _cast_kernel,
        grid=(k // tk,),
        in_specs=[pl.BlockSpec((tk, n), lambda i: (i, 0))],
        out_specs=pl.BlockSpec((tk, n), lambda i: (i, 0)),
        out_shape=jax.ShapeDtypeStruct((k, n), BF16),
        compiler_params=_params(("arbitrary",)),
        name="cast",
    )(w)


def _inproj_kernel(x_ref, sh_ref, sc_ref, g_ref, w_ref, u_ref, qkv_ref, *, d_ssm):
    h = (_rms(x_ref[0], g_ref[...]) * (1.0 + sc_ref[0]) + sh_ref[0]).astype(BF16)
    n_total = w_ref.shape[1]
    for n in range(n_total // d_ssm):
        r = jnp.dot(h, w_ref[:, n * d_ssm:(n + 1) * d_ssm], preferred_element_type=F32)
        if n == 0:
            for cc in range(u_ref.shape[0]):
                u_ref[cc, 0] = r[cc * SSM_T:(cc + 1) * SSM_T]
        else:
            qkv_ref[0, :, (n - 1) * d_ssm:n * d_ssm] = r.astype(BF16)


def _inproj(x, sh, sc, g, w_bf, d_ssm):
    bsz, L, D = x.shape
    n = w_bf.shape[1]
    tm = 256
    T = SSM_T
    return pl.pallas_call(
        functools.partial(_inproj_kernel, d_ssm=d_ssm),
        grid=(bsz, L // tm),
        in_specs=[pl.BlockSpec((1, tm, D), lambda b, i: (b, i, 0)),
                  pl.BlockSpec((1, 1, D), lambda b, i: (b, 0, 0)),
                  pl.BlockSpec((1, 1, D), lambda b, i: (b, 0, 0)),
                  pl.BlockSpec((1, D), lambda b, i: (0, 0)),
                  pl.BlockSpec((D, n), lambda b, i: (0, 0))],
        out_specs=[pl.BlockSpec((tm // T, 1, T, d_ssm), lambda b, i: (i, b, 0, 0)),
                   pl.BlockSpec((1, tm, n - d_ssm), lambda b, i: (b, i, 0))],
        out_shape=[jax.ShapeDtypeStruct((L // T, bsz, T, d_ssm), F32),
                   jax.ShapeDtypeStruct((bsz, L, n - d_ssm), BF16)],
        compiler_params=_params(("arbitrary", "arbitrary")),
        name="inproj",
    )(x, sh, sc, g, w_bf)


def _ssm_operators(a_re, a_im, log_dt, b_re, b_im, c_re, c_im):
    T = SSM_T
    hi = lax.Precision.HIGHEST
    G, P = a_re.shape[1], a_re.shape[2]
    H = b_re.shape[3]
    dt = jnp.exp(log_dt)[..., None]
    mag = jnp.exp(a_re * dt)
    lbr = mag * jnp.cos(a_im * dt)
    lbi = mag * jnp.sin(a_im * dt)
    den = a_re * a_re + a_im * a_im
    nr = lbr - 1.0
    fr = (nr * a_re + lbi * a_im) / den
    fi = (lbi * a_re - nr * a_im) / den
    bbr = fr[..., None] * b_re - fi[..., None] * b_im
    bbi = fr[..., None] * b_im + fi[..., None] * b_re
    prs, pis = [jnp.ones_like(lbr)], [jnp.zeros_like(lbi)]
    for _ in range(T):
        pr, pi = prs[-1], pis[-1]
        prs.append(pr * lbr - pi * lbi)
        pis.append(pr * lbi + pi * lbr)
    pr = jnp.stack(prs)
    pi = jnp.stack(pis)
    cpr = c_re[None] * pr[:, :, :, None, :] - c_im[None] * pi[:, :, :, None, :]
    cpi = c_re[None] * pi[:, :, :, None, :] + c_im[None] * pr[:, :, :, None, :]
    kern = (jnp.einsum('kdghp,dgpj->kdghj', cpr[:T], bbr, precision=hi)
            - jnp.einsum('kdghp,dgpj->kdghj', cpi[:T], bbi, precision=hi))
    s_idx = jnp.arange(T)[:, None, None]
    t_idx = jnp.arange(T)[None, :, None]
    k_idx = jnp.arange(T)[None, None, :]
    sel_f = (t_idx - s_idx == k_idx).astype(F32)
    sel_b = (s_idx - t_idx == k_idx).astype(F32)
    m = (jnp.einsum('stk,kghj->gsjth', sel_f, kern[:, 0], precision=hi)
         + jnp.einsum('stk,kghj->gsjth', sel_b, kern[:, 1], precision=hi)).reshape(G, T * H, T * H)
    pf_r, pf_i = pr[:T, 0][::-1], pi[:T, 0][::-1]
    pb_r, pb_i = pr[:T, 1], pi[:T, 1]
    wf_r = pf_r[..., None] * bbr[0][None] - pf_i[..., None] * bbi[0][None]
    wf_i = pf_r[..., None] * bbi[0][None] + pf_i[..., None] * bbr[0][None]
    wb_r = pb_r[..., None] * bbr[1][None] - pb_i[..., None] * bbi[1][None]
    wb_i = pb_r[..., None] * bbi[1][None] + pb_i[..., None] * bbr[1][None]
    w4 = jnp.stack([wf_r, wf_i, wb_r, wb_i])
    w4 = jnp.transpose(w4, (2, 1, 4, 0, 3)).reshape(G // 2, 2, T * H, 4, P)
    eye2 = jnp.eye(2, dtype=F32)
    wpair = jnp.einsum('agrqp,gk->agrqkp', w4, eye2).reshape(G // 2, 2 * T * H, 8 * P)
    vf_r = cpr[1:, 0]
    vf_i = -cpi[1:, 0]
    vb_r = cpr[1:, 1][::-1]
    vb_i = -cpi[1:, 1][::-1]
    v4 = jnp.stack([vf_r, vf_i, vb_r, vb_i])
    v4 = jnp.transpose(v4, (2, 0, 4, 1, 3)).reshape(G // 2, 2, 4, P, T * H)
    vpair = jnp.einsum('agqpc,gk->aqgpkc', v4, eye2).reshape(G // 2, 8 * P, 2 * T * H)
    a4 = jnp.stack([pr[T, 0], pi[T, 0], pr[T, 1], pi[T, 1]])
    a_chunk = jnp.transpose(a4.reshape(4, G // 2, 2, P), (1, 0, 2, 3)).reshape(1, G // 2 * 8 * P)
    return m.astype(BF16), wpair.astype(BF16), vpair.astype(BF16), a_chunk


def _slab_permutation():
    idx = jnp.arange(8 * LANES)
    t, g, h = idx // LANES, (idx % LANES) // SSM_GROUP, idx % SSM_GROUP
    dst = g * LANES + t * SSM_GROUP + h
    return (dst[:, None] == jnp.arange(8 * LANES)[None, :]).astype(BF16)


def _chunk_scan(s_ref, a_ref, *, bsz, n_pairs):
    n_it = s_ref.shape[0] // (2 * bsz)
    for j in range(n_pairs):
        cols = [pl.ds((4 * j + q) * LANES, LANES) for q in range(4)]
        ar_f, ai_f, ar_b, ai_b = [a_ref[:, c] for c in cols]

        def fwd(i, carry):
            xr, xi = carry
            r0 = pl.multiple_of(i * 2 * bsz, 2 * bsz)
            sr = s_ref[pl.ds(r0, 2 * bsz), cols[0]]
            si = s_ref[pl.ds(r0, 2 * bsz), cols[1]]
            xr1 = ar_f * xr - ai_f * xi + sr[:bsz]
            xi1 = ar_f * xi + ai_f * xr + si[:bsz]
            s_ref[pl.ds(r0, 2 * bsz), cols[0]] = jnp.concatenate([xr, xr1], axis=0)
            s_ref[pl.ds(r0, 2 * bsz), cols[1]] = jnp.concatenate([xi, xi1], axis=0)
            xr2 = ar_f * xr1 - ai_f * xi1 + sr[bsz:]
            xi2 = ar_f * xi1 + ai_f * xr1 + si[bsz:]
            return xr2, xi2

        def bwd(i, carry):
            xr, xi = carry
            r0 = pl.multiple_of((n_it - 1 - i) * 2 * bsz, 2 * bsz)
            sr = s_ref[pl.ds(r0, 2 * bsz), cols[2]]
            si = s_ref[pl.ds(r0, 2 * bsz), cols[3]]
            xr1 = ar_b * xr - ai_b * xi + sr[bsz:]
            xi1 = ar_b * xi + ai_b * xr + si[bsz:]
            s_ref[pl.ds(r0, 2 * bsz), cols[2]] = jnp.concatenate([xr1, xr], axis=0)
            s_ref[pl.ds(r0, 2 * bsz), cols[3]] = jnp.concatenate([xi1, xi], axis=0)
            xr2 = ar_b * xr1 - ai_b * xi1 + sr[:bsz]
            xi2 = ar_b * xi1 + ai_b * xr1 + si[:bsz]
            return xr2, xi2

        z = jnp.zeros((bsz, LANES), F32)
        lax.fori_loop(0, n_it, fwd, (z, z))
        lax.fori_loop(0, n_it, bwd, (z, z))


def _ssm_kernel(u_ref, d_ref, perm_ref, iperm_ref, w_ref, m_ref, v_ref, a_ref, y_ref,
                ug_ref, s_ref, yg_ref, *, bsz):
    T = SSM_T
    R = u_ref.shape[0] // T
    n_groups = LANES // SSM_GROUP
    kw = T * SSM_GROUP
    for tq in range(T // 8):
        cat = jnp.concatenate([u_ref[pl.ds(tq * 8 + t, R, stride=T), :].astype(BF16) for t in range(8)], axis=-1)
        grouped = jnp.dot(cat, perm_ref[...], preferred_element_type=F32).astype(BF16)
        for g in range(n_groups):
            ug_ref[g, :, tq * LANES:(tq + 1) * LANES] = grouped[:, g * LANES:(g + 1) * LANES]
    for pr in range(n_groups // 2):
        u2 = jnp.concatenate([ug_ref[2 * pr], ug_ref[2 * pr + 1]], axis=-1)
        s_ref[:, pr * 4 * LANES:(pr + 1) * 4 * LANES] = jnp.dot(u2, w_ref[pr], preferred_element_type=F32)
    _chunk_scan(s_ref, a_ref, bsz=bsz, n_pairs=n_groups // 2)
    for pr in range(n_groups // 2):
        inter = jnp.dot(s_ref[:, pr * 4 * LANES:(pr + 1) * 4 * LANES].astype(BF16), v_ref[pr],
                        preferred_element_type=F32)
        for gl in range(2):
            g = 2 * pr + gl
            yg_ref[g] = (jnp.dot(ug_ref[g], m_ref[g], preferred_element_type=F32)
                         + inter[:, gl * kw:(gl + 1) * kw])
    for tq in range(T // 8):
        ycat = jnp.concatenate([yg_ref[g, :, tq * LANES:(tq + 1) * LANES] for g in range(n_groups)], axis=-1)
        y_hi = ycat.astype(BF16)
        y_lo = (ycat - y_hi.astype(F32)).astype(BF16)
        back = (jnp.dot(y_hi, iperm_ref[...], preferred_element_type=F32)
                + jnp.dot(y_lo, iperm_ref[...], preferred_element_type=F32))
        for t in range(8):
            rows = pl.ds(tq * 8 + t, R, stride=T)
            y_ref[rows, :] = back[:, t * LANES:(t + 1) * LANES] + d_ref[...] * u_ref[rows, :]


def _ssm(u_tok, d_skip, ops, bsz):
    m, wpair, vpair, a_chunk = ops
    N, d_ssm = u_tok.shape
    T = SSM_T
    R = N // T
    n_groups = LANES // SSM_GROUP
    perm = _slab_permutation()
    const = lambda shape: pl.BlockSpec(shape, lambda j: (0,) * len(shape))
    return pl.pallas_call(
        functools.partial(_ssm_kernel, bsz=bsz),
        grid=(d_ssm // LANES,),
        in_specs=[pl.BlockSpec((N, LANES), lambda j: (0, j)),
                  pl.BlockSpec((1, LANES), lambda j: (0, j)),
                  const(perm.shape), const(perm.shape),
                  pl.BlockSpec((n_groups // 2,) + wpair.shape[1:], lambda j: (j, 0, 0)),
                  pl.BlockSpec((n_groups,) + m.shape[1:], lambda j: (j, 0, 0)),
                  pl.BlockSpec((n_groups // 2,) + vpair.shape[1:], lambda j: (j, 0, 0)),
                  pl.BlockSpec((1, n_groups // 2 * 4 * LANES), lambda j: (0, j))],
        out_specs=pl.BlockSpec((N, LANES), lambda j: (0, j)),
        out_shape=jax.ShapeDtypeStruct((N, d_ssm), F32),
        scratch_shapes=[pltpu.VMEM((n_groups, R, T * SSM_GROUP), BF16),
                        pltpu.VMEM((R, n_groups // 2 * 4 * LANES), F32),
                        pltpu.VMEM((n_groups, R, T * SSM_GROUP), F32)],
        compiler_params=_params(("arbitrary",)),
        name="ssm",
    )(u_tok, d_skip, perm, perm.T, wpair, m, vpair, a_chunk)


def _attn_tables(rpb, rows):
    kh = min(WIN_H_MAX, rows)
    H = rpb.shape[0]
    w = jnp.arange(GRID_W)
    col_start = jnp.clip(w - WIN_W // 2, 0, GRID_W - WIN_W)
    col_mask = (w[None, :] >= col_start[:, None]) & (w[None, :] < col_start[:, None] + WIN_W)
    dc = jnp.clip(w[None, :] - w[:, None] + (WIN_W - 1), 0, 2 * WIN_W - 2)
    col_sel = (dc[:, :, None] == jnp.arange(2 * WIN_W - 1)[None, None, :]).astype(F32)
    tile = jnp.einsum('hdc,qkc->hdqk', rpb, col_sel, precision=lax.Precision.HIGHEST)
    tile = jnp.where(col_mask[None, None], tile, NEG)
    lo_pad, hi_pad = ATT_ROWS, ATT_WIN - kh + 1
    tile = jnp.pad(tile, ((0, 0), (lo_pad, hi_pad), (0, 0), (0, 0)))
    n_dr = tile.shape[1] - 1
    tiles = jnp.concatenate([tile[:, :-1], tile[:, 1:]], axis=-1).reshape(H // 2, 2, n_dr, GRID_W, 2 * GRID_W)
    i = jnp.arange(ATT_ROWS)
    last_union = rows - ATT_WIN
    last_rel = (rows - kh) - last_union
    rel = jnp.stack([jnp.zeros_like(i), i, jnp.full_like(i, last_rel)])
    j = jnp.arange(ATT_WIN)
    valid = (j[None, None, :] >= rel[:, :, None]) & (j[None, None, :] < rel[:, :, None] + kh)
    masks = jnp.where(valid, 0.0, NEG).astype(F32)
    masks = jnp.repeat(masks.reshape(3 * ATT_ROWS * ATT_WIN // 2, 2), GRID_W, axis=-1)
    return tiles, masks.reshape(-1, 1, 2 * GRID_W)


def _attn_kernel(q_ref, k_ref, v_ref, t_ref, m_ref, o_ref, *, rows, kh):
    n_blocks = rows // ATT_ROWS
    nq = ATT_ROWS * GRID_W
    nk = ATT_WIN * GRID_W
    n_pairs = ATT_WIN // 2
    lane = lax.broadcasted_iota(I32, (1, 2 * HEAD_DIM), 1)
    head_lanes = [lane < HEAD_DIM, lane >= HEAD_DIM]

    def block(rb, carry):
        r0 = rb * ATT_ROWS
        union = jnp.clip(r0 - kh // 2, 0, rows - ATT_WIN)
        pattern = jnp.where(rb == 0, 0, jnp.where(rb == n_blocks - 1, 2, 1))
        dr0 = union - r0 + (WIN_H_MAX - 1) + ATT_ROWS
        q2 = q_ref[0, pl.ds(pl.multiple_of(r0 * GRID_W, nq), nq), :]
        k2 = k_ref[0, pl.ds(pl.multiple_of(union * GRID_W, GRID_W), nk), :]
        v2 = v_ref[0, pl.ds(pl.multiple_of(union * GRID_W, GRID_W), nk), :]
        out = jnp.zeros((nq, 2 * HEAD_DIM), F32)
        for hl in range(2):
            qm = jnp.where(head_lanes[hl], q2, 0.0) * (HEAD_DIM ** -0.5)
            s = lax.dot_general(qm.astype(BF16), k2, (((1,), (1,)), ((), ())), preferred_element_type=F32)
            bias = jnp.concatenate(
                [jnp.concatenate(
                    [t_ref[0, hl, dr0 + 2 * jp - i] + m_ref[(pattern * ATT_ROWS + i) * n_pairs + jp]
                     for jp in range(n_pairs)], axis=-1)
                 for i in range(ATT_ROWS)], axis=0)
            s = s + bias
            p = jnp.exp(s - jnp.max(s, axis=-1, keepdims=True))
            denom = jnp.sum(p, axis=-1, keepdims=True)
            vm = jnp.where(head_lanes[hl], v2, 0.0).astype(BF16)
            out = out + jnp.dot(p.astype(BF16), vm, preferred_element_type=F32) / denom
        o_ref[0, pl.ds(pl.multiple_of(r0 * GRID_W, nq), nq), :] = out
        return carry

    lax.fori_loop(0, n_blocks, block, 0)


def _attn(qkv, tiles, masks):
    bsz, L, n3 = qkv.shape
    d = n3 // 3
    rows = L // GRID_W
    kh = min(WIN_H_MAX, rows)
    n_hp = d // (2 * HEAD_DIM)
    blk = pl.BlockSpec((1, L, 2 * HEAD_DIM), lambda b, hp: (b, 0, hp))
    return pl.pallas_call(
        functools.partial(_attn_kernel, rows=rows, kh=kh),
        grid=(bsz, n_hp),
        in_specs=[blk,
                  pl.BlockSpec((1, L, 2 * HEAD_DIM), lambda b, hp: (b, 0, n_hp + hp)),
                  pl.BlockSpec((1, L, 2 * HEAD_DIM), lambda b, hp: (b, 0, 2 * n_hp + hp)),
                  pl.BlockSpec((1,) + tiles.shape[1:], lambda b, hp: (hp, 0, 0, 0, 0)),
                  pl.BlockSpec(masks.shape, lambda b, hp: (0, 0, 0))],
        out_specs=blk,
        out_shape=jax.ShapeDtypeStruct((bsz, L, d), F32),
        compiler_params=_params(("arbitrary", "arbitrary")),
        name="attn",
    )(qkv, qkv, qkv, tiles, masks)


def _split_dot(a, w_hi, w_lo):
    a_hi = a.astype(BF16)
    a_lo = (a - a_hi.astype(F32)).astype(BF16)
    return (jnp.dot(a_hi, w_hi, preferred_element_type=F32)
            + jnp.dot(a_lo, w_hi, preferred_element_type=F32)
            + jnp.dot(a_hi, w_lo, preferred_element_type=F32))


def _mix_kernel(ys_ref, wglu_ref, bglu_ref, gs_ref, ya_ref, ga_ref, wout_ref,
                x_ref, gt_ref, gf_ref, sh_ref, sc_ref, wr_ref,
                x1_ref, h2_ref, lg_ref):
    d_ssm = ys_ref.shape[3]
    y = ys_ref[:, 0].reshape(-1, d_ssm)
    z = jax.nn.gelu(y)
    gate = _sigmoid(jnp.dot(z.astype(BF16), wglu_ref[...], preferred_element_type=F32) + bglu_ref[...])
    a = _rms(z * gate, gs_ref[...]).astype(BF16)
    t = _rms(ya_ref[0], ga_ref[...]).astype(BF16)
    mixed = (jnp.dot(a, wout_ref[:d_ssm, :], preferred_element_type=F32)
             + jnp.dot(t, wout_ref[d_ssm:, :], preferred_element_type=F32))
    x1 = x_ref[0] + gt_ref[0] * mixed
    x1_ref[0] = x1
    h2 = _rms(x1, gf_ref[...]) * (1.0 + sc_ref[0]) + sh_ref[0]
    h2_ref[0] = h2.astype(BF16)
    wr = wr_ref[...]
    wr_hi = wr.astype(BF16)
    wr_lo = (wr - wr_hi.astype(F32)).astype(BF16)
    lg_ref[0] = _split_dot(h2, wr_hi, wr_lo)


def _mix(ys, wglu_bf, b_glu, g_ssm, ya, g_attn, wout_bf, x, gt1, g_ffn, sh2, sc2, w_router):
    bsz, L, D = x.shape
    d_ssm = ys.shape[3]
    E = w_router.shape[1]
    tm = 256
    T = SSM_T
    tile = lambda n: pl.BlockSpec((1, tm, n), lambda b, i: (b, i, 0))
    per_b = lambda n: pl.BlockSpec((1, 1, n), lambda b, i: (b, 0, 0))
    full = lambda r, n: pl.BlockSpec((r, n), lambda b, i: (0, 0))
    return pl.pallas_call(
        _mix_kernel,
        grid=(bsz, L // tm),
        in_specs=[pl.BlockSpec((tm // T, 1, T, d_ssm), lambda b, i: (i, b, 0, 0)),
                  full(d_ssm, d_ssm), full(1, d_ssm), full(1, d_ssm),
                  tile(d_ssm), full(1, d_ssm), full(D, D),
                  tile(D), per_b(D), full(1, D), per_b(D), per_b(D), full(D, E)],
        out_specs=[tile(D), tile(D), tile(E)],
        out_shape=[jax.ShapeDtypeStruct((bsz, L, D), F32),
                   jax.ShapeDtypeStruct((bsz, L, D), BF16),
                   jax.ShapeDtypeStruct((bsz, L, E), F32)],
        compiler_params=_params(("arbitrary", "arbitrary")),
        name="mix",
    )(ys, wglu_bf, b_glu, g_ssm, ya, g_attn, wout_bf, x, gt1, g_ffn, sh2, sc2, w_router)


TOPK_BISECTIONS = 160


def _topk_kernel(lg_ref, aff_ref, pos_ref, *, cap):
    lg = lg_ref[...]
    bsz, E, L = lg.shape
    e = jnp.exp(lg - jnp.max(lg, axis=1, keepdims=True))
    aff3 = e / jnp.sum(e, axis=1, keepdims=True)
    aff_ref[...] = aff3
    aff = aff3.reshape(bsz * E, L)

    def count(mask):
        return jnp.sum(jnp.where(mask, 1.0, 0.0), axis=-1, keepdims=True)

    def halve(_, carry):
        lo, hi = carry
        mid = lo + (hi - lo) * 0.5
        ok = count(aff >= mid) >= cap
        return jnp.where(ok, mid, lo), jnp.where(ok, hi, mid)

    lo0 = jnp.zeros((bsz * E, 1), F32)
    thr, _ = lax.fori_loop(0, TOPK_BISECTIONS, halve, (lo0, lo0 + 2.0))
    gt = aff > thr
    eq = aff == thr
    need = cap - count(gt)
    blk = LANES
    tri = jnp.where(lax.broadcasted_iota(I32, (blk, blk), 0) < lax.broadcasted_iota(I32, (blk, blk), 1),
                    1.0, 0.0).astype(BF16)

    def prefix_count(mask):
        ones = jnp.where(mask, 1.0, 0.0)
        run = jnp.zeros((bsz * E, 1), F32)
        outs = []
        for j in range(L // blk):
            piece = ones[:, j * blk:(j + 1) * blk]
            outs.append(jnp.dot(piece.astype(BF16), tri, preferred_element_type=F32) + run)
            run = run + jnp.sum(piece, axis=-1, keepdims=True)
        return jnp.concatenate(outs, axis=-1)

    sel = gt | (eq & (prefix_count(eq) < need))
    pos = prefix_count(sel)
    pos_ref[...] = jnp.where(sel, pos.astype(I32), -1).reshape(bsz, E, L)


def _topk(lg_t, cap):
    bsz, E, L = lg_t.shape
    spec = pl.BlockSpec((bsz, E, L), lambda i: (0, 0, 0))
    return pl.pallas_call(
        functools.partial(_topk_kernel, cap=cap),
        grid=(1,),
        in_specs=[spec],
        out_specs=[spec, spec],
        out_shape=[jax.ShapeDtypeStruct((bsz, E, L), F32), jax.ShapeDtypeStruct((bsz, E, L), I32)],
        compiler_params=_params(("arbitrary",)),
        name="topk",
    )(lg_t)


def _gather_kernel(pos_ref, aff_ref, h_ref, xe_ref, as_ref, *, cap):
    L = h_ref.shape[1]
    slot = lax.broadcasted_iota(I32, (cap, L), 0)
    hit = pos_ref[0, 0] == slot
    xe_ref[0] = jnp.dot(jnp.where(hit, 1.0, 0.0).astype(BF16), h_ref[0],
                        preferred_element_type=F32).astype(BF16)
    as_ref[0] = jnp.sum(jnp.where(hit, aff_ref[0, 0], 0.0), axis=-1, keepdims=True)


def _gather(pos, aff, h2, cap):
    bsz, E, L = pos.shape
    D = h2.shape[2]
    pos4 = pos.reshape(bsz, E, 1, L)
    aff4 = aff.reshape(bsz, E, 1, L)
    return pl.pallas_call(
        functools.partial(_gather_kernel, cap=cap),
        grid=(bsz, E),
        in_specs=[pl.BlockSpec((1, 1, 1, L), lambda b, e: (b, e, 0, 0)),
                  pl.BlockSpec((1, 1, 1, L), lambda b, e: (b, e, 0, 0)),
                  pl.BlockSpec((1, L, D), lambda b, e: (b, 0, 0))],
        out_specs=[pl.BlockSpec((1, cap, D), lambda b, e: (e, b, 0)),
                   pl.BlockSpec((1, cap, 1), lambda b, e: (e, b, 0))],
        out_shape=[jax.ShapeDtypeStruct((E, bsz * cap, D), BF16),
                   jax.ShapeDtypeStruct((E, bsz * cap, 1), F32)],
        compiler_params=_params(("arbitrary", "arbitrary")),
        name="gather",
    )(pos4, aff4, h2)


def _ffn_kernel(xe_ref, wg_ref, wu_ref, wd_ref, as_ref, ye_ref, acc_ref):
    f = pl.program_id(1)

    @pl.when(f == 0)
    def _():
        acc_ref[...] = jnp.zeros_like(acc_ref)

    x = xe_ref[0]
    g = jnp.dot(x, wg_ref[0].astype(BF16), preferred_element_type=F32)
    up = jnp.dot(x, wu_ref[0].astype(BF16), preferred_element_type=F32)
    hid = (g * _sigmoid(g) * up).astype(BF16)
    acc_ref[...] += jnp.dot(hid, wd_ref[0].astype(BF16), preferred_element_type=F32)

    @pl.when(f == pl.num_programs(1) - 1)
    def _():
        ye_ref[0] = (acc_ref[...] * as_ref[0]).astype(BF16)


def _ffn(xe, w_gate, w_up, w_down, aff_slot):
    E, R, D = xe.shape
    F = w_gate.shape[2]
    tf = 256
    return pl.pallas_call(
        _ffn_kernel,
        grid=(E, F // tf),
        in_specs=[pl.BlockSpec((1, R, D), lambda e, f: (e, 0, 0)),
                  pl.BlockSpec((1, D, tf), lambda e, f: (e, 0, f)),
                  pl.BlockSpec((1, D, tf), lambda e, f: (e, 0, f)),
                  pl.BlockSpec((1, tf, D), lambda e, f: (e, f, 0)),
                  pl.BlockSpec((1, R, 1), lambda e, f: (e, 0, 0))],
        out_specs=pl.BlockSpec((1, R, D), lambda e, f: (e, 0, 0)),
        out_shape=jax.ShapeDtypeStruct((E, R, D), BF16),
        scratch_shapes=[pltpu.VMEM((R, D), F32)],
        compiler_params=_params(("arbitrary", "arbitrary")),
        name="ffn",
    )(xe, w_gate, w_up, w_down, aff_slot)


def _combine_kernel(pos_ref, ye_ref, x1_ref, gt_ref, g_ref, o_ref, acc_ref, *, cap, final_norm):
    e = pl.program_id(2)

    @pl.when(e == 0)
    def _():
        acc_ref[...] = jnp.zeros_like(acc_ref)

    pos_t = pos_ref[0]
    lane = lax.broadcasted_iota(I32, pos_t.shape, 1)
    pcol = jnp.sum(jnp.where(lane == e, pos_t, 0).astype(F32), axis=-1, keepdims=True).astype(I32)
    slot = lax.broadcasted_iota(I32, (pos_t.shape[0], cap), 1)
    onehot = jnp.where(pcol == slot, 1.0, 0.0).astype(BF16)
    acc_ref[...] += jnp.dot(onehot, ye_ref[0], preferred_element_type=F32)

    @pl.when(e == pl.num_programs(2) - 1)
    def _():
        o = x1_ref[0] + gt_ref[0] * acc_ref[...]
        o_ref[0] = _rms(o, g_ref[...]) if final_norm else o


def _combine(pos_t, ye, x1, gt2, g_final, cap, final_norm):
    bsz, L, E = pos_t.shape
    D = x1.shape[2]
    tl = 1024
    return pl.pallas_call(
        functools.partial(_combine_kernel, cap=cap, final_norm=final_norm),
        grid=(bsz, L // tl, E),
        in_specs=[pl.BlockSpec((1, tl, E), lambda b, i, e: (b, i, 0)),
                  pl.BlockSpec((1, cap, D), lambda b, i, e: (e, b, 0)),
                  pl.BlockSpec((1, tl, D), lambda b, i, e: (b, i, 0)),
                  pl.BlockSpec((1, 1, D), lambda b, i, e: (b, 0, 0)),
                  pl.BlockSpec((1, D), lambda b, i, e: (0, 0))],
        out_specs=pl.BlockSpec((1, tl, D), lambda b, i, e: (b, i, 0)),
        out_shape=jax.ShapeDtypeStruct((bsz, L, D), F32),
        scratch_shapes=[pltpu.VMEM((tl, D), F32)],
        compiler_params=_params(("arbitrary", "arbitrary", "arbitrary")),
        name="combine",
    )(pos_t, ye, x1, gt2, g_final)


def kernel(x, c, w_ada, b_ada, g_mix, w_in, ssm_a_re, ssm_a_im, ssm_log_dt, ssm_b_re, ssm_b_im,
           ssm_c_re, ssm_c_im, ssm_d, w_glu, b_glu, rpb, g_ssm_out, g_attn_out, w_out, g_ffn,
           w_router, w_gate, w_up, w_down, g_final):
    bsz, L, D = x.shape
    depth = w_ada.shape[0]
    d_ssm = ssm_d.shape[1]
    E = w_router.shape[2]
    cap = CAPACITY_FACTOR * L // E
    rows = L // GRID_W
    c8 = jnp.zeros((8, D), F32).at[:bsz].set(c)
    row = lambda v: v.reshape(1, -1)
    for layer in range(depth):
        mod = _ada(c8, w_ada[layer], row(b_ada[layer]))[:bsz]
        sh1, sc1, gt1, sh2, sc2, gt2 = [m.reshape(bsz, 1, D) for m in jnp.split(mod, 6, axis=-1)]

        u, qkv = _inproj(x, sh1, sc1, row(g_mix[layer]), _cast_bf16(w_in[layer]), d_ssm)
        ops = _ssm_operators(ssm_a_re[layer], ssm_a_im[layer], ssm_log_dt[layer], ssm_b_re[layer],
                             ssm_b_im[layer], ssm_c_re[layer], ssm_c_im[layer])
        y_ssm = _ssm(u.reshape(-1, d_ssm), row(ssm_d[layer]), ops, bsz).reshape(u.shape)
        y_attn = _attn(qkv, *_attn_tables(rpb[layer], rows))
        x1, h2, logits = _mix(y_ssm, _cast_bf16(w_glu[layer]), row(b_glu[layer]),
                              row(g_ssm_out[layer]), y_attn, row(g_attn_out[layer]), _cast_bf16(w_out[layer]),
                              x, gt1, row(g_ffn[layer]), sh2, sc2, w_router[layer])

        aff, pos = _topk(jnp.swapaxes(logits, 1, 2), cap)
        xe, aff_slot = _gather(pos, aff, h2, cap)
        ye = _ffn(xe, w_gate[layer], w_up[layer], w_down[layer], aff_slot)
        x = _combine(jnp.swapaxes(pos, 1, 2), ye, x1, gt2, row(g_final), cap, layer == depth - 1)
    return x
```

```python
import functools

import jax
import jax.numpy as jnp
from jax import lax
from jax.experimental import pallas as pl
from jax.experimental.pallas import tpu as pltpu

F32 = jnp.float32
BF16 = jnp.bfloat16
I32 = jnp.int32

EPS = 1e-6
GRID_W = 64
SSM_GROUP = 16
SSM_STATE = 64
N_HEADS = 16
HEAD_DIM = 64
WIN_H_MAX = 8
WIN_W = 16
N_EXPERTS = 16
CAPACITY_FACTOR = 2
SSM_T = 16
LANES = 128
NEG = -1e30

ATT_ROWS = 4
ATT_WIN = 12

VMEM_LIMIT = 56 * 1024 * 1024


def _params(sem, vmem=VMEM_LIMIT):
    return pltpu.CompilerParams(dimension_semantics=sem, vmem_limit_bytes=vmem)


def _sigmoid(x):
    return 1.0 / (1.0 + jnp.exp(-x))


def _rms(x, g):
    return x * lax.rsqrt(jnp.mean(x * x, axis=-1, keepdims=True) + EPS) * g


def _ada_kernel(c_ref, w_ref, b_ref, o_ref):
    c = c_ref[...]
    ca = (c * _sigmoid(c)).astype(BF16)
    o_ref[...] = jnp.dot(ca, w_ref[...].astype(BF16), preferred_element_type=F32) + b_ref[...]


def _ada(c8, w, b):
    k, n = w.shape
    tn = 1024
    return pl.pallas_call(
        _ada_kernel,
        grid=(n // tn,),
        in_specs=[pl.BlockSpec((8, k), lambda j: (0, 0)),
                  pl.BlockSpec((k, tn), lambda j: (0, j)),
                  pl.BlockSpec((1, tn), lambda j: (0, j))],
        out_specs=pl.BlockSpec((8, tn), lambda j: (0, j)),
        out_shape=jax.ShapeDtypeStruct((8, n), F32),
        compiler_params=_params(("arbitrary",)),
        name="ada",
    )(c8, w, b)


def _cast_kernel(w_ref, o_ref):
    o_ref[...] = w_ref[...].astype(BF16)


def _cast_bf16(w):
    k, n = w.shape
    tk = 256
    return pl.pallas_call(
        _cast_kernel,
        grid=(k // tk,),
        in_specs=[pl.BlockSpec((tk, n), lambda i: (i, 0))],
        out_specs=pl.BlockSpec((tk, n), lambda i: (i, 0)),
        out_shape=jax.ShapeDtypeStruct((k, n), BF16),
        compiler_params=_params(("arbitrary",)),
        name="cast",
    )(w)


def _inproj_kernel(x_ref, sh_ref, sc_ref, g_ref, w_ref, u_ref, qkv_ref, *, d_ssm):
    h = (_rms(x_ref[0], g_ref[...]) * (1.0 + sc_ref[0]) + sh_ref[0]).astype(BF16)
    n_total = w_ref.shape[1]
    for n in range(n_total // d_ssm):
        r = jnp.dot(h, w_ref[:, n * d_ssm:(n + 1) * d_ssm], preferred_element_type=F32)
        if n == 0:
            for cc in range(u_ref.shape[0]):
                u_ref[cc, 0] = r[cc * SSM_T:(cc + 1) * SSM_T]
        else:
            qkv_ref[0, :, (n - 1) * d_ssm:n * d_ssm] = r.astype(BF16)


def _inproj(x, sh, sc, g, w_bf, d_ssm):
    bsz, L, D = x.shape
    n = w_bf.shape[1]
    tm = 256
    T = SSM_T
    return pl.pallas_call(
        functools.partial(_inproj_kernel, d_ssm=d_ssm),
        grid=(bsz, L // tm),
        in_specs=[pl.BlockSpec((1, tm, D), lambda b, i: (b, i, 0)),
                  pl.BlockSpec((1, 1, D), lambda b, i: (b, 0, 0)),
                  pl.BlockSpec((1, 1, D), lambda b, i: (b, 0, 0)),
                  pl.BlockSpec((1, D), lambda b, i: (0, 0)),
                  pl.BlockSpec((D, n), lambda b, i: (0, 0))],
        out_specs=[pl.BlockSpec((tm // T, 1, T, d_ssm), lambda b, i: (i, b, 0, 0)),
                   pl.BlockSpec((1, tm, n - d_ssm), lambda b, i: (b, i, 0))],
        out_shape=[jax.ShapeDtypeStruct((L // T, bsz, T, d_ssm), F32),
                   jax.ShapeDtypeStruct((bsz, L, n - d_ssm), BF16)],
        compiler_params=_params(("arbitrary", "arbitrary")),
        name="inproj",
    )(x, sh, sc, g, w_bf)


def _ssm_operators(a_re, a_im, log_dt, b_re, b_im, c_re, c_im):
    T = SSM_T
    G, P = a_re.shape[1], a_re.shape[2]
    H = b_re.shape[3]
    dt = jnp.exp(log_dt)[..., None]
    mag = jnp.exp(a_re * dt)
    lbr = mag * jnp.cos(a_im * dt)
    lbi = mag * jnp.sin(a_im * dt)
    den = a_re * a_re + a_im * a_im
    nr = lbr - 1.0
    fr = ((nr * a_re + lbi * a_im) / den)[:, :, None, :]
    fi = ((lbi * a_re - nr * a_im) / den)[:, :, None, :]
    b_re_t, b_im_t = jnp.swapaxes(b_re, 2, 3), jnp.swapaxes(b_im, 2, 3)
    bbr = fr * b_re_t - fi * b_im_t
    bbi = fr * b_im_t + fi * b_re_t
    prs, pis = [jnp.ones_like(lbr)], [jnp.zeros_like(lbi)]
    for _ in range(T):
        pr, pi = prs[-1], pis[-1]
        prs.append(pr * lbr - pi * lbi)
        pis.append(pr * lbi + pi * lbr)
    pr = jnp.stack(prs, axis=2)
    pi = jnp.stack(pis, axis=2)
    cat = lambda xs: jnp.concatenate(xs, axis=-1)
    zero = jnp.zeros((G, H, P), F32)
    even = (jnp.arange(G) % 2 == 0)[:, None, None]

    def slots(v4):
        ev = cat([v4[0], zero, v4[1], zero, v4[2], zero, v4[3], zero])
        od = cat([zero, v4[0], zero, v4[1], zero, v4[2], zero, v4[3]])
        return jnp.where(even, ev, od)

    mats = jnp.stack([slots([bbr[0], bbi[0], bbr[1], bbi[1]]), slots([-bbi[0], bbr[0], -bbi[1], bbr[1]]),
                      slots([c_re[0], -c_im[0], c_re[1], -c_im[1]]), slots([-c_im[0], -c_re[0], -c_im[1], -c_re[1]])],
                     axis=1)
    rep4 = lambda x: cat([x, x, x, x])
    rows = jnp.stack([cat([rep4(pr[0, :, :T][:, ::-1]), rep4(pr[1, :, :T])]),
                      cat([rep4(pi[0, :, :T][:, ::-1]), rep4(pi[1, :, :T])]),
                      cat([rep4(pr[0, :, 1:]), rep4(pr[1, :, 1:][:, ::-1])]),
                      cat([rep4(pi[0, :, 1:]), rep4(pi[1, :, 1:][:, ::-1])])], axis=1)
    kmats = jnp.stack([cat([bbr[0], -bbi[0], bbr[1], -bbi[1]]),
                       cat([c_re[0], c_re[0], c_re[1], c_re[1]]),
                       cat([-c_im[0], c_im[0], -c_im[1], c_im[1]])], axis=1)
    krows = jnp.stack([cat([pr[0], pi[0], pr[1], pi[1]]), cat([pi[0], pr[0], pi[1], pr[1]])], axis=1)
    m, wpair, vtpair = _ssm_prep(mats, rows, kmats, krows)
    a4 = jnp.stack([pr[0, :, T], pi[0, :, T], pr[1, :, T], pi[1, :, T]])
    a_chunk = jnp.transpose(a4.reshape(4, G // 2, 2, P), (1, 0, 2, 3)).reshape(1, G // 2 * 8 * P)
    return m, wpair, vtpair, a_chunk


def _nt_dot(a, b):
    return lax.dot_general(a, b, (((1,), (1,)), ((), ())), preferred_element_type=F32)


def _ssm_prep_kernel(mats_ref, rows_ref, kmats_ref, krows_ref, m_ref, w_ref, vt_ref):
    T = SSM_T
    H = mats_ref.shape[2]
    kw = T * H
    half = kmats_ref.shape[3] // 2
    lane = lax.broadcasted_iota(I32, (1, 2 * half), 1)

    def group(g, carry):
        pair, row0 = g // 2, (g % 2) * kw
        aw, bw, av, bv = [mats_ref[g, i] for i in range(4)]
        for s in range(T):
            r = pl.ds(pl.multiple_of(row0 + s * H, H), H)
            w_ref[pair, r, :] = (aw * rows_ref[g, 0, pl.ds(s, 1), :] + bw * rows_ref[g, 1, pl.ds(s, 1), :]).astype(BF16)
            vt_ref[pair, r, :] = (av * rows_ref[g, 2, pl.ds(s, 1), :] + bv * rows_ref[g, 3, pl.ds(s, 1), :]).astype(BF16)
        lhs, a4, b4 = [kmats_ref[g, i] for i in range(3)]
        blocks = []
        for i in range(2 * T):
            k = abs(i - (T - 1))
            if i == 2 * T - 1:
                blocks.append(jnp.zeros_like(a4))
                continue
            cp = a4 * krows_ref[g, 0, pl.ds(k, 1), :] + b4 * krows_ref[g, 1, pl.ds(k, 1), :]
            if i < T - 1:
                cp = jnp.where(lane >= half, cp, 0.0)
            elif i > T - 1:
                cp = jnp.where(lane < half, cp, 0.0)
            blocks.append(cp)
        rhs = jnp.concatenate(blocks, axis=0)
        lhs_hi = lhs.astype(BF16)
        lhs_lo = (lhs - lhs_hi.astype(F32)).astype(BF16)
        rhs_hi = rhs.astype(BF16)
        rhs_lo = (rhs - rhs_hi.astype(F32)).astype(BF16)
        z = _nt_dot(lhs_hi, rhs_hi) + _nt_dot(lhs_lo, rhs_hi) + _nt_dot(lhs_hi, rhs_lo)
        for s in range(T):
            off = (T - 1 - s) * H
            m_ref[g, s * H:(s + 1) * H, :] = z[:, off:off + kw].astype(BF16)
        return carry

    lax.fori_loop(0, mats_ref.shape[0], group, 0)


def _ssm_prep(mats, rows, kmats, krows):
    G, _, H, n8 = mats.shape
    T = SSM_T
    gs = LANES // SSM_GROUP
    lead = lambda a: pl.BlockSpec((gs,) + a.shape[1:], lambda j: (j,) + (0,) * (a.ndim - 1))
    return pl.pallas_call(
        _ssm_prep_kernel,
        grid=(G // gs,),
        in_specs=[lead(mats), lead(rows), lead(kmats), lead(krows)],
        out_specs=[pl.BlockSpec((gs, T * H, T * H), lambda j: (j, 0, 0)),
                   pl.BlockSpec((gs // 2, 2 * T * H, n8), lambda j: (j, 0, 0)),
                   pl.BlockSpec((gs // 2, 2 * T * H, n8), lambda j: (j, 0, 0))],
        out_shape=[jax.ShapeDtypeStruct((G, T * H, T * H), BF16),
                   jax.ShapeDtypeStruct((G // 2, 2 * T * H, n8), BF16),
                   jax.ShapeDtypeStruct((G // 2, 2 * T * H, n8), BF16)],
        compiler_params=_params(("arbitrary",)),
        name="ssm_prep",
    )(mats, rows, kmats, krows)


def _slab_permutation():
    idx = jnp.arange(8 * LANES)
    t, g, h = idx // LANES, (idx % LANES) // SSM_GROUP, idx % SSM_GROUP
    dst = g * LANES + t * SSM_GROUP + h
    return (dst[:, None] == jnp.arange(8 * LANES)[None, :]).astype(BF16)


def _chunk_scan(s_ref, a_ref, *, bsz, n_pairs):
    n_it = s_ref.shape[0] // (2 * bsz)
    for j in range(n_pairs):
        cols = [pl.ds((4 * j + q) * LANES, LANES) for q in range(4)]
        ar_f, ai_f, ar_b, ai_b = [a_ref[:, c] for c in cols]

        def fwd(i, carry):
            xr, xi = carry
            r0 = pl.multiple_of(i * 2 * bsz, 2 * bsz)
            sr = s_ref[pl.ds(r0, 2 * bsz), cols[0]]
            si = s_ref[pl.ds(r0, 2 * bsz), cols[1]]
            xr1 = ar_f * xr - ai_f * xi + sr[:bsz]
            xi1 = ar_f * xi + ai_f * xr + si[:bsz]
            s_ref[pl.ds(r0, 2 * bsz), cols[0]] = jnp.concatenate([xr, xr1], axis=0)
            s_ref[pl.ds(r0, 2 * bsz), cols[1]] = jnp.concatenate([xi, xi1], axis=0)
            xr2 = ar_f * xr1 - ai_f * xi1 + sr[bsz:]
            xi2 = ar_f * xi1 + ai_f * xr1 + si[bsz:]
            return xr2, xi2

        def bwd(i, carry):
            xr, xi = carry
            r0 = pl.multiple_of((n_it - 1 - i) * 2 * bsz, 2 * bsz)
            sr = s_ref[pl.ds(r0, 2 * bsz), cols[2]]
            si = s_ref[pl.ds(r0, 2 * bsz), cols[3]]
            xr1 = ar_b * xr - ai_b * xi + sr[bsz:]
            xi1 = ar_b * xi + ai_b * xr + si[bsz:]
            s_ref[pl.ds(r0, 2 * bsz), cols[2]] = jnp.concatenate([xr1, xr], axis=0)
            s_ref[pl.ds(r0, 2 * bsz), cols[3]] = jnp.concatenate([xi1, xi], axis=0)
            xr2 = ar_b * xr1 - ai_b * xi1 + sr[:bsz]
            xi2 = ar_b * xi1 + ai_b * xr1 + si[:bsz]
            return xr2, xi2

        z = jnp.zeros((bsz, LANES), F32)
        lax.fori_loop(0, n_it, fwd, (z, z))
        lax.fori_loop(0, n_it, bwd, (z, z))


def _ssm_kernel(u_ref, d_ref, perm_ref, iperm_ref, w_ref, m_ref, v_ref, a_ref, y_ref,
                ug_ref, s_ref, yg_ref, *, bsz):
    T = SSM_T
    R = u_ref.shape[0] // T
    n_groups = LANES // SSM_GROUP
    kw = T * SSM_GROUP
    for tq in range(T // 8):
        cat = jnp.concatenate([u_ref[pl.ds(tq * 8 + t, R, stride=T), :].astype(BF16) for t in range(8)], axis=-1)
        grouped = jnp.dot(cat, perm_ref[...], preferred_element_type=F32).astype(BF16)
        for g in range(n_groups):
            ug_ref[g, :, tq * LANES:(tq + 1) * LANES] = grouped[:, g * LANES:(g + 1) * LANES]
    for pr in range(n_groups // 2):
        u2 = jnp.concatenate([ug_ref[2 * pr], ug_ref[2 * pr + 1]], axis=-1)
        s_ref[:, pr * 4 * LANES:(pr + 1) * 4 * LANES] = jnp.dot(u2, w_ref[pr], preferred_element_type=F32)
    _chunk_scan(s_ref, a_ref, bsz=bsz, n_pairs=n_groups // 2)
    for pr in range(n_groups // 2):
        inter = _nt_dot(s_ref[:, pr * 4 * LANES:(pr + 1) * 4 * LANES].astype(BF16), v_ref[pr])
        for gl in range(2):
            g = 2 * pr + gl
            yg_ref[g] = (jnp.dot(ug_ref[g], m_ref[g], preferred_element_type=F32)
                         + inter[:, gl * kw:(gl + 1) * kw])
    for tq in range(T // 8):
        ycat = jnp.concatenate([yg_ref[g, :, tq * LANES:(tq + 1) * LANES] for g in range(n_groups)], axis=-1)
        y_hi = ycat.astype(BF16)
        y_lo = (ycat - y_hi.astype(F32)).astype(BF16)
        back = (jnp.dot(y_hi, iperm_ref[...], preferred_element_type=F32)
                + jnp.dot(y_lo, iperm_ref[...], preferred_element_type=F32))
        for t in range(8):
            rows = pl.ds(tq * 8 + t, R, stride=T)
            y_ref[rows, :] = back[:, t * LANES:(t + 1) * LANES] + d_ref[...] * u_ref[rows, :]


def _ssm(u_tok, d_skip, ops, bsz):
    m, wpair, vpair, a_chunk = ops
    N, d_ssm = u_tok.shape
    T = SSM_T
    R = N // T
    n_groups = LANES // SSM_GROUP
    perm = _slab_permutation()
    const = lambda shape: pl.BlockSpec(shape, lambda j: (0,) * len(shape))
    return pl.pallas_call(
        functools.partial(_ssm_kernel, bsz=bsz),
        grid=(d_ssm // LANES,),
        in_specs=[pl.BlockSpec((N, LANES), lambda j: (0, j)),
                  pl.BlockSpec((1, LANES), lambda j: (0, j)),
                  const(perm.shape), const(perm.shape),
                  pl.BlockSpec((n_groups // 2,) + wpair.shape[1:], lambda j: (j, 0, 0)),
                  pl.BlockSpec((n_groups,) + m.shape[1:], lambda j: (j, 0, 0)),
                  pl.BlockSpec((n_groups // 2,) + vpair.shape[1:], lambda j: (j, 0, 0)),
                  pl.BlockSpec((1, n_groups // 2 * 4 * LANES), lambda j: (0, j))],
        out_specs=pl.BlockSpec((N, LANES), lambda j: (0, j)),
        out_shape=jax.ShapeDtypeStruct((N, d_ssm), F32),
        scratch_shapes=[pltpu.VMEM((n_groups, R, T * SSM_GROUP), BF16),
                        pltpu.VMEM((R, n_groups // 2 * 4 * LANES), F32),
                        pltpu.VMEM((n_groups, R, T * SSM_GROUP), F32)],
        compiler_params=_params(("arbitrary",)),
        name="ssm",
    )(u_tok, d_skip, perm, perm.T, wpair, m, vpair, a_chunk)


def _attn_tables(rpb, rows):
    kh = min(WIN_H_MAX, rows)
    H = rpb.shape[0]
    w = jnp.arange(GRID_W)
    col_start = jnp.clip(w - WIN_W // 2, 0, GRID_W - WIN_W)
    col_mask = (w[None, :] >= col_start[:, None]) & (w[None, :] < col_start[:, None] + WIN_W)
    dc = jnp.clip(w[None, :] - w[:, None] + (WIN_W - 1), 0, 2 * WIN_W - 2)
    col_sel = (dc[:, :, None] == jnp.arange(2 * WIN_W - 1)[None, None, :]).astype(F32)
    tile = jnp.einsum('hdc,qkc->hdqk', rpb, col_sel, precision=lax.Precision.HIGHEST)
    tile = jnp.where(col_mask[None, None], tile, NEG)
    lo_pad, hi_pad = ATT_ROWS, ATT_WIN - kh + 1
    tile = jnp.pad(tile, ((0, 0), (lo_pad, hi_pad), (0, 0), (0, 0)))
    n_dr = tile.shape[1] - 1
    tiles = jnp.concatenate([tile[:, :-1], tile[:, 1:]], axis=-1).reshape(H // 2, 2, n_dr, GRID_W, 2 * GRID_W)
    i = jnp.arange(ATT_ROWS)
    last_union = rows - ATT_WIN
    last_rel = (rows - kh) - last_union
    rel = jnp.stack([jnp.zeros_like(i), i, jnp.full_like(i, last_rel)])
    j = jnp.arange(ATT_WIN)
    valid = (j[None, None, :] >= rel[:, :, None]) & (j[None, None, :] < rel[:, :, None] + kh)
    masks = jnp.where(valid, 0.0, NEG).astype(F32)
    masks = jnp.repeat(masks.reshape(3 * ATT_ROWS * ATT_WIN // 2, 2), GRID_W, axis=-1)
    return tiles, masks.reshape(-1, 1, 2 * GRID_W)


def _attn_kernel(q_ref, k_ref, v_ref, t_ref, m_ref, o_ref, *, rows, kh):
    n_blocks = rows // ATT_ROWS
    nq = ATT_ROWS * GRID_W
    nk = ATT_WIN * GRID_W
    n_pairs = ATT_WIN // 2
    lane = lax.broadcasted_iota(I32, (1, 2 * HEAD_DIM), 1)
    head_lanes = [lane < HEAD_DIM, lane >= HEAD_DIM]

    def block(rb, carry):
        r0 = rb * ATT_ROWS
        union = jnp.clip(r0 - kh // 2, 0, rows - ATT_WIN)
        pattern = jnp.where(rb == 0, 0, jnp.where(rb == n_blocks - 1, 2, 1))
        dr0 = union - r0 + (WIN_H_MAX - 1) + ATT_ROWS
        q2 = q_ref[0, pl.ds(pl.multiple_of(r0 * GRID_W, nq), nq), :]
        k2 = k_ref[0, pl.ds(pl.multiple_of(union * GRID_W, GRID_W), nk), :]
        v2 = v_ref[0, pl.ds(pl.multiple_of(union * GRID_W, GRID_W), nk), :]
        out = jnp.zeros((nq, 2 * HEAD_DIM), F32)
        for hl in range(2):
            qm = jnp.where(head_lanes[hl], q2, 0.0) * (HEAD_DIM ** -0.5)
            s = lax.dot_general(qm.astype(BF16), k2, (((1,), (1,)), ((), ())), preferred_element_type=F32)
            bias = jnp.concatenate(
                [jnp.concatenate(
                    [t_ref[0, hl, dr0 + 2 * jp - i] + m_ref[(pattern * ATT_ROWS + i) * n_pairs + jp]
                     for jp in range(n_pairs)], axis=-1)
                 for i in range(ATT_ROWS)], axis=0)
            s = s + bias
            p = jnp.exp(s - jnp.max(s, axis=-1, keepdims=True))
            denom = jnp.sum(p, axis=-1, keepdims=True)
            vm = jnp.where(head_lanes[hl], v2, 0.0).astype(BF16)
            out = out + jnp.dot(p.astype(BF16), vm, preferred_element_type=F32) / denom
        o_ref[0, pl.ds(pl.multiple_of(r0 * GRID_W, nq), nq), :] = out
        return carry

    lax.fori_loop(0, n_blocks, block, 0, unroll=2)


def _attn(qkv, tiles, masks):
    bsz, L, n3 = qkv.shape
    d = n3 // 3
    rows = L // GRID_W
    kh = min(WIN_H_MAX, rows)
    assert rows % ATT_ROWS == 0 and rows // ATT_ROWS >= 3 and ATT_ROWS == kh // 2
    assert ATT_WIN % 2 == 0 and kh + ATT_ROWS - 1 <= ATT_WIN <= rows
    n_hp = d // (2 * HEAD_DIM)
    blk = pl.BlockSpec((1, L, 2 * HEAD_DIM), lambda b, hp: (b, 0, hp))
    return pl.pallas_call(
        functools.partial(_attn_kernel, rows=rows, kh=kh),
        grid=(bsz, n_hp),
        in_specs=[blk,
                  pl.BlockSpec((1, L, 2 * HEAD_DIM), lambda b, hp: (b, 0, n_hp + hp)),
                  pl.BlockSpec((1, L, 2 * HEAD_DIM), lambda b, hp: (b, 0, 2 * n_hp + hp)),
                  pl.BlockSpec((1,) + tiles.shape[1:], lambda b, hp: (hp, 0, 0, 0, 0)),
                  pl.BlockSpec(masks.shape, lambda b, hp: (0, 0, 0))],
        out_specs=blk,
        out_shape=jax.ShapeDtypeStruct((bsz, L, d), F32),
        compiler_params=_params(("arbitrary", "arbitrary")),
        name="attn",
    )(qkv, qkv, qkv, tiles, masks)


def _split_dot(a, w_hi, w_lo):
    a_hi = a.astype(BF16)
    a_lo = (a - a_hi.astype(F32)).astype(BF16)
    return (jnp.dot(a_hi, w_hi, preferred_element_type=F32)
            + jnp.dot(a_lo, w_hi, preferred_element_type=F32)
            + jnp.dot(a_hi, w_lo, preferred_element_type=F32))


def _mix_kernel(ys_ref, wglu_ref, bglu_ref, gs_ref, ya_ref, ga_ref, wout_ref,
                x_ref, gt_ref, gf_ref, sh_ref, sc_ref, wr_ref,
                x1_ref, h2_ref, lg_ref):
    d_ssm = ys_ref.shape[3]
    y = ys_ref[:, 0].reshape(-1, d_ssm)
    z = jax.nn.gelu(y)
    gate = _sigmoid(jnp.dot(z.astype(BF16), wglu_ref[...], preferred_element_type=F32) + bglu_ref[...])
    a = _rms(z * gate, gs_ref[...]).astype(BF16)
    t = _rms(ya_ref[0], ga_ref[...]).astype(BF16)
    mixed = (jnp.dot(a, wout_ref[:d_ssm, :], preferred_element_type=F32)
             + jnp.dot(t, wout_ref[d_ssm:, :], preferred_element_type=F32))
    x1 = x_ref[0] + gt_ref[0] * mixed
    x1_ref[0] = x1
    h2 = _rms(x1, gf_ref[...]) * (1.0 + sc_ref[0]) + sh_ref[0]
    h2_ref[0] = h2.astype(BF16)
    wr = wr_ref[...]
    wr_hi = wr.astype(BF16)
    wr_lo = (wr - wr_hi.astype(F32)).astype(BF16)
    lg_ref[0] = _split_dot(h2, wr_hi, wr_lo)


def _mix(ys, wglu_bf, b_glu, g_ssm, ya, g_attn, wout_bf, x, gt1, g_ffn, sh2, sc2, w_router):
    bsz, L, D = x.shape
    d_ssm = ys.shape[3]
    E = w_router.shape[1]
    tm = 256
    T = SSM_T
    tile = lambda n: pl.BlockSpec((1, tm, n), lambda b, i: (b, i, 0))
    per_b = lambda n: pl.BlockSpec((1, 1, n), lambda b, i: (b, 0, 0))
    full = lambda r, n: pl.BlockSpec((r, n), lambda b, i: (0, 0))
    return pl.pallas_call(
        _mix_kernel,
        grid=(bsz, L // tm),
        in_specs=[pl.BlockSpec((tm // T, 1, T, d_ssm), lambda b, i: (i, b, 0, 0)),
                  full(d_ssm, d_ssm), full(1, d_ssm), full(1, d_ssm),
                  tile(d_ssm), full(1, d_ssm), full(D, D),
                  tile(D), per_b(D), full(1, D), per_b(D), per_b(D), full(D, E)],
        out_specs=[tile(D), tile(D), tile(E)],
        out_shape=[jax.ShapeDtypeStruct((bsz, L, D), F32),
                   jax.ShapeDtypeStruct((bsz, L, D), BF16),
                   jax.ShapeDtypeStruct((bsz, L, E), F32)],
        compiler_params=_params(("arbitrary", "arbitrary")),
        name="mix",
    )(ys, wglu_bf, b_glu, g_ssm, ya, g_attn, wout_bf, x, gt1, g_ffn, sh2, sc2, w_router)


TOPK_BISECTIONS = 160


def _topk_kernel(lg_ref, aff_ref, pos_ref, *, cap):
    lg = lg_ref[...]
    bsz, E, L = lg.shape
    e = jnp.exp(lg - jnp.max(lg, axis=1, keepdims=True))
    aff3 = e / jnp.sum(e, axis=1, keepdims=True)
    aff_ref[...] = aff3
    aff = aff3.reshape(bsz * E, L)

    def count(mask):
        return jnp.sum(jnp.where(mask, 1.0, 0.0), axis=-1, keepdims=True)

    def halve(_, carry):
        lo, hi = carry
        mid = lo + (hi - lo) * 0.5
        ok = count(aff >= mid) >= cap
        return jnp.where(ok, mid, lo), jnp.where(ok, hi, mid)

    lo0 = jnp.zeros((bsz * E, 1), F32)
    thr, _ = lax.fori_loop(0, TOPK_BISECTIONS, halve, (lo0, lo0 + 2.0))
    gt = aff > thr
    eq = aff == thr
    need = cap - count(gt)
    blk = LANES
    tri = jnp.where(lax.broadcasted_iota(I32, (blk, blk), 0) < lax.broadcasted_iota(I32, (blk, blk), 1),
                    1.0, 0.0).astype(BF16)

    def prefix_count(mask):
        ones = jnp.where(mask, 1.0, 0.0)
        run = jnp.zeros((bsz * E, 1), F32)
        outs = []
        for j in range(L // blk):
            piece = ones[:, j * blk:(j + 1) * blk]
            outs.append(jnp.dot(piece.astype(BF16), tri, preferred_element_type=F32) + run)
            run = run + jnp.sum(piece, axis=-1, keepdims=True)
        return jnp.concatenate(outs, axis=-1)

    sel = gt | (eq & (prefix_count(eq) < need))
    pos = prefix_count(sel)
    pos_ref[...] = jnp.where(sel, pos.astype(I32), -1).reshape(bsz, E, L)


def _topk(lg_t, cap):
    bsz, E, L = lg_t.shape
    spec = pl.BlockSpec((bsz, E, L), lambda i: (0, 0, 0))
    return pl.pallas_call(
        functools.partial(_topk_kernel, cap=cap),
        grid=(1,),
        in_specs=[spec],
        out_specs=[spec, spec],
        out_shape=[jax.ShapeDtypeStruct((bsz, E, L), F32), jax.ShapeDtypeStruct((bsz, E, L), I32)],
        compiler_params=_params(("arbitrary",)),
        name="topk",
    )(lg_t)


def _gather_kernel(pos_ref, aff_ref, h_ref, xe_ref, as_ref, *, cap):
    L = h_ref.shape[1]
    slot = lax.broadcasted_iota(I32, (cap, L), 0)
    hit = pos_ref[0, 0] == slot
    xe_ref[0] = jnp.dot(jnp.where(hit, 1.0, 0.0).astype(BF16), h_ref[0],
                        preferred_element_type=F32).astype(BF16)
    as_ref[0] = jnp.sum(jnp.where(hit, aff_ref[0, 0], 0.0), axis=-1, keepdims=True)


def _gather(pos, aff, h2, cap):
    bsz, E, L = pos.shape
    D = h2.shape[2]
    pos4 = pos.reshape(bsz, E, 1, L)
    aff4 = aff.reshape(bsz, E, 1, L)
    return pl.pallas_call(
        functools.partial(_gather_kernel, cap=cap),
        grid=(bsz, E),
        in_specs=[pl.BlockSpec((1, 1, 1, L), lambda b, e: (b, e, 0, 0)),
                  pl.BlockSpec((1, 1, 1, L), lambda b, e: (b, e, 0, 0)),
                  pl.BlockSpec((1, L, D), lambda b, e: (b, 0, 0))],
        out_specs=[pl.BlockSpec((1, cap, D), lambda b, e: (e, b, 0)),
                   pl.BlockSpec((1, cap, 1), lambda b, e: (e, b, 0))],
        out_shape=[jax.ShapeDtypeStruct((E, bsz * cap, D), BF16),
                   jax.ShapeDtypeStruct((E, bsz * cap, 1), F32)],
        compiler_params=_params(("arbitrary", "arbitrary")),
        name="gather",
    )(pos4, aff4, h2)


def _ffn_kernel(xe_ref, wg_ref, wu_ref, wd_ref, as_ref, ye_ref, acc_ref):
    f = pl.program_id(1)

    @pl.when(f == 0)
    def _():
        acc_ref[...] = jnp.zeros_like(acc_ref)

    x = xe_ref[0]
    g = jnp.dot(x, wg_ref[0].astype(BF16), preferred_element_type=F32)
    up = jnp.dot(x, wu_ref[0].astype(BF16), preferred_element_type=F32)
    hid = (g * _sigmoid(g) * up).astype(BF16)
    acc_ref[...] += jnp.dot(hid, wd_ref[0].astype(BF16), preferred_element_type=F32)

    @pl.when(f == pl.num_programs(1) - 1)
    def _():
        ye_ref[0] = (acc_ref[...] * as_ref[0]).astype(BF16)


def _ffn(xe, w_gate, w_up, w_down, aff_slot):
    E, R, D = xe.shape
    F = w_gate.shape[2]
    tf = 256
    return pl.pallas_call(
        _ffn_kernel,
        grid=(E, F // tf),
        in_specs=[pl.BlockSpec((1, R, D), lambda e, f: (e, 0, 0)),
                  pl.BlockSpec((1, D, tf), lambda e, f: (e, 0, f)),
                  pl.BlockSpec((1, D, tf), lambda e, f: (e, 0, f)),
                  pl.BlockSpec((1, tf, D), lambda e, f: (e, f, 0)),
                  pl.BlockSpec((1, R, 1), lambda e, f: (e, 0, 0))],
        out_specs=pl.BlockSpec((1, R, D), lambda e, f: (e, 0, 0)),
        out_shape=jax.ShapeDtypeStruct((E, R, D), BF16),
        scratch_shapes=[pltpu.VMEM((R, D), F32)],
        compiler_params=_params(("arbitrary", "arbitrary")),
        name="ffn",
    )(xe, w_gate, w_up, w_down, aff_slot)


def _combine_kernel(pos_ref, ye_ref, x1_ref, gt_ref, g_ref, o_ref, acc_ref, *, cap, final_norm):
    e = pl.program_id(2)

    @pl.when(e == 0)
    def _():
        acc_ref[...] = jnp.zeros_like(acc_ref)

    pos_t = pos_ref[0]
    lane = lax.broadcasted_iota(I32, pos_t.shape, 1)
    pcol = jnp.sum(jnp.where(lane == e, pos_t, 0).astype(F32), axis=-1, keepdims=True).astype(I32)
    slot = lax.broadcasted_iota(I32, (pos_t.shape[0], cap), 1)
    onehot = jnp.where(pcol == slot, 1.0, 0.0).astype(BF16)
    acc_ref[...] += jnp.dot(onehot, ye_ref[0], preferred_element_type=F32)

    @pl.when(e == pl.num_programs(2) - 1)
    def _():
        o = x1_ref[0] + gt_ref[0] * acc_ref[...]
        o_ref[0] = _rms(o, g_ref[...]) if final_norm else o


def _combine(pos_t, ye, x1, gt2, g_final, cap, final_norm):
    bsz, L, E = pos_t.shape
    D = x1.shape[2]
    tl = 1024
    return pl.pallas_call(
        functools.partial(_combine_kernel, cap=cap, final_norm=final_norm),
        grid=(bsz, L // tl, E),
        in_specs=[pl.BlockSpec((1, tl, E), lambda b, i, e: (b, i, 0)),
                  pl.BlockSpec((1, cap, D), lambda b, i, e: (e, b, 0)),
                  pl.BlockSpec((1, tl, D), lambda b, i, e: (b, i, 0)),
                  pl.BlockSpec((1, 1, D), lambda b, i, e: (b, 0, 0)),
                  pl.BlockSpec((1, D), lambda b, i, e: (0, 0))],
        out_specs=pl.BlockSpec((1, tl, D), lambda b, i, e: (b, i, 0)),
        out_shape=jax.ShapeDtypeStruct((bsz, L, D), F32),
        scratch_shapes=[pltpu.VMEM((tl, D), F32)],
        compiler_params=_params(("arbitrary", "arbitrary", "arbitrary")),
        name="combine",
    )(pos_t, ye, x1, gt2, g_final)


def kernel(x, c, w_ada, b_ada, g_mix, w_in, ssm_a_re, ssm_a_im, ssm_log_dt, ssm_b_re, ssm_b_im,
           ssm_c_re, ssm_c_im, ssm_d, w_glu, b_glu, rpb, g_ssm_out, g_attn_out, w_out, g_ffn,
           w_router, w_gate, w_up, w_down, g_final):
    bsz, L, D = x.shape
    depth = w_ada.shape[0]
    d_ssm = ssm_d.shape[1]
    E = w_router.shape[2]
    cap = CAPACITY_FACTOR * L // E
    rows = L // GRID_W
    c8 = jnp.zeros((8, D), F32).at[:bsz].set(c)
    row = lambda v: v.reshape(1, -1)
    for layer in range(depth):
        mod = _ada(c8, w_ada[layer], row(b_ada[layer]))[:bsz]
        sh1, sc1, gt1, sh2, sc2, gt2 = [m.reshape(bsz, 1, D) for m in jnp.split(mod, 6, axis=-1)]

        u, qkv = _inproj(x, sh1, sc1, row(g_mix[layer]), _cast_bf16(w_in[layer]), d_ssm)
        ops = _ssm_operators(ssm_a_re[layer], ssm_a_im[layer], ssm_log_dt[layer], ssm_b_re[layer],
                             ssm_b_im[layer], ssm_c_re[layer], ssm_c_im[layer])
        y_ssm = _ssm(u.reshape(-1, d_ssm), row(ssm_d[layer]), ops, bsz).reshape(u.shape)
        y_attn = _attn(qkv, *_attn_tables(rpb[layer], rows))
        x1, h2, logits = _mix(y_ssm, _cast_bf16(w_glu[layer]), row(b_glu[layer]),
                              row(g_ssm_out[layer]), y_attn, row(g_attn_out[layer]), _cast_bf16(w_out[layer]),
                              x, gt1, row(g_ffn[layer]), sh2, sc2, w_router[layer])

        aff, pos = _topk(jnp.swapaxes(logits, 1, 2), cap)
        xe, aff_slot = _gather(pos, aff, h2, cap)
        ye = _ffn(xe, w_gate[layer], w_up[layer], w_down[layer], aff_slot)
        x = _combine(jnp.swapaxes(pos, 1, 2), ye, x1, gt2, row(g_final), cap, layer == depth - 1)
    return x
```

```python
import functools
import math

import jax
import jax.numpy as jnp
from jax import lax
from jax.experimental import pallas as pl
from jax.experimental.pallas import tpu as pltpu

F32 = jnp.float32
BF16 = jnp.bfloat16
I32 = jnp.int32

EPS = 1e-6
GRID_W = 64
SSM_GROUP = 16
SSM_STATE = 64
N_HEADS = 16
HEAD_DIM = 64
WIN_H_MAX = 8
WIN_W = 16
N_EXPERTS = 16
CAPACITY_FACTOR = 2
SSM_T = 16
LANES = 128
NEG = -1e30
LOG2E = math.log2(math.e)

ATT_ROWS = 4
ATT_WIN = 12

VMEM_LIMIT = 56 * 1024 * 1024


def _params(sem, vmem=VMEM_LIMIT):
    return pltpu.CompilerParams(dimension_semantics=sem, vmem_limit_bytes=vmem)


def _sigmoid(x):
    return 1.0 / (1.0 + jnp.exp(-x))


def _rms(x, g):
    return x * lax.rsqrt(jnp.mean(x * x, axis=-1, keepdims=True) + EPS) * g


def _ada_kernel(c_ref, w_ref, b_ref, o_ref):
    c = c_ref[...]
    ca = (c * _sigmoid(c)).astype(BF16)
    o_ref[...] = jnp.dot(ca, w_ref[...].astype(BF16), preferred_element_type=F32) + b_ref[...]


def _ada(c8, w, b):
    k, n = w.shape
    tn = 1024
    return pl.pallas_call(
        _ada_kernel,
        grid=(n // tn,),
        in_specs=[pl.BlockSpec((8, k), lambda j: (0, 0)),
                  pl.BlockSpec((k, tn), lambda j: (0, j)),
                  pl.BlockSpec((1, tn), lambda j: (0, j))],
        out_specs=pl.BlockSpec((8, tn), lambda j: (0, j)),
        out_shape=jax.ShapeDtypeStruct((8, n), F32),
        compiler_params=_params(("arbitrary",)),
        name="ada",
    )(c8, w, b)


def _cast_kernel(w_ref, o_ref):
    o_ref[...] = w_ref[...].astype(BF16)


def _cast_bf16(w):
    k, n = w.shape
    tk = 256
    return pl.pallas_call(
        _cast_kernel,
        grid=(k // tk,),
        in_specs=[pl.BlockSpec((tk, n), lambda i: (i, 0))],
        out_specs=pl.BlockSpec((tk, n), lambda i: (i, 0)),
        out_shape=jax.ShapeDtypeStruct((k, n), BF16),
        compiler_params=_params(("arbitrary",)),
        name="cast",
    )(w)


def _inproj_kernel(x_ref, sh_ref, sc_ref, g_ref, w_ref, u_ref, qkv_ref, *, d_ssm):
    h = (_rms(x_ref[0], g_ref[...]) * (1.0 + sc_ref[0]) + sh_ref[0]).astype(BF16)
    n_total = w_ref.shape[1]
    for n in range(n_total // d_ssm):
        r = jnp.dot(h, w_ref[:, n * d_ssm:(n + 1) * d_ssm], preferred_element_type=F32)
        if n == 0:
            for cc in range(u_ref.shape[0]):
                u_ref[cc, 0] = r[cc * SSM_T:(cc + 1) * SSM_T]
        else:
            if n == 1:
                r = r * (HEAD_DIM ** -0.5 * LOG2E)
            qkv_ref[0, :, (n - 1) * d_ssm:n * d_ssm] = r.astype(BF16)


def _inproj(x, sh, sc, g, w_bf, d_ssm):
    bsz, L, D = x.shape
    n = w_bf.shape[1]
    tm = 256
    T = SSM_T
    return pl.pallas_call(
        functools.partial(_inproj_kernel, d_ssm=d_ssm),
        grid=(bsz, L // tm),
        in_specs=[pl.BlockSpec((1, tm, D), lambda b, i: (b, i, 0)),
                  pl.BlockSpec((1, 1, D), lambda b, i: (b, 0, 0)),
                  pl.BlockSpec((1, 1, D), lambda b, i: (b, 0, 0)),
                  pl.BlockSpec((1, D), lambda b, i: (0, 0)),
                  pl.BlockSpec((D, n), lambda b, i: (0, 0))],
        out_specs=[pl.BlockSpec((tm // T, 1, T, d_ssm), lambda b, i: (i, b, 0, 0)),
                   pl.BlockSpec((1, tm, n - d_ssm), lambda b, i: (b, i, 0))],
        out_shape=[jax.ShapeDtypeStruct((L // T, bsz, T, d_ssm), F32),
                   jax.ShapeDtypeStruct((bsz, L, n - d_ssm), BF16)],
        compiler_params=_params(("arbitrary", "arbitrary")),
        name="inproj",
    )(x, sh, sc, g, w_bf)


def _ssm_operators(a_re, a_im, log_dt, b_re, b_im, c_re, c_im):
    T = SSM_T
    G, P = a_re.shape[1], a_re.shape[2]
    H = b_re.shape[3]
    dt = jnp.exp(log_dt)[..., None]
    mag = jnp.exp(a_re * dt)
    lbr = mag * jnp.cos(a_im * dt)
    lbi = mag * jnp.sin(a_im * dt)
    den = a_re * a_re + a_im * a_im
    nr = lbr - 1.0
    fr = ((nr * a_re + lbi * a_im) / den)[:, :, None, :]
    fi = ((lbi * a_re - nr * a_im) / den)[:, :, None, :]
    b_re_t, b_im_t = jnp.swapaxes(b_re, 2, 3), jnp.swapaxes(b_im, 2, 3)
    bbr = fr * b_re_t - fi * b_im_t
    bbi = fr * b_im_t + fi * b_re_t
    prs, pis = [jnp.ones_like(lbr)], [jnp.zeros_like(lbi)]
    for _ in range(T):
        pr, pi = prs[-1], pis[-1]
        prs.append(pr * lbr - pi * lbi)
        pis.append(pr * lbi + pi * lbr)
    pr = jnp.stack(prs, axis=2)
    pi = jnp.stack(pis, axis=2)
    assert H == T
    rev = lambda x: x[:, ::-1]
    sections = [bbr[0], bbi[0], bbr[1], bbi[1], c_re[0], c_im[0], c_re[1], c_im[1],
                rev(pr[0, :, :T]), rev(pi[0, :, :T]), pr[1, :, :T], pi[1, :, :T],
                pr[0, :, 1:], pi[0, :, 1:], rev(pr[1, :, 1:]), rev(pi[1, :, 1:]),
                pr[0, :, :T], pi[0, :, :T]]
    m, wpair, vtpair = _ssm_prep(jnp.concatenate(sections, axis=1))
    a4 = jnp.stack([pr[0, :, T], pi[0, :, T], pr[1, :, T], pi[1, :, T]])
    a_chunk = jnp.transpose(a4.reshape(4, G // 2, 2, P), (1, 0, 2, 3)).reshape(1, G // 2 * 8 * P)
    return m, wpair, vtpair, a_chunk


def _nt_dot(a, b):
    return lax.dot_general(a, b, (((1,), (1,)), ((), ())), preferred_element_type=F32)


def _ssm_prep_kernel(pk_ref, m_ref, w_ref, vt_ref, mats_ref, rows_ref, kmats_ref, krows_ref):
    T = SSM_T
    H = T
    kw = T * H
    half = kmats_ref.shape[2] // 2
    lane = lax.broadcasted_iota(I32, (1, 2 * half), 1)
    cat = lambda xs: jnp.concatenate(xs, axis=-1)

    def group(g, carry):
        (bbr_f, bbi_f, bbr_b, bbi_b, cre_f, cim_f, cre_b, cim_b, qrf, qif, qrb, qib,
         rrf, rif, rrb, rib, krf, kif) = [pk_ref[g, i * T:(i + 1) * T, :] for i in range(18)]
        zero = jnp.zeros_like(bbr_f)
        even = g % 2 == 0

        def slots(v4):
            ev = cat([v4[0], zero, v4[1], zero, v4[2], zero, v4[3], zero])
            od = cat([zero, v4[0], zero, v4[1], zero, v4[2], zero, v4[3]])
            return jnp.where(even, ev, od)

        mats_ref[0] = slots([bbr_f, bbi_f, bbr_b, bbi_b])
        mats_ref[1] = slots([-bbi_f, bbr_f, -bbi_b, bbr_b])
        mats_ref[2] = slots([cre_f, -cim_f, cre_b, -cim_b])
        mats_ref[3] = slots([-cim_f, -cre_f, -cim_b, -cre_b])
        rows_ref[0] = cat([qrf] * 4 + [qrb] * 4)
        rows_ref[1] = cat([qif] * 4 + [qib] * 4)
        rows_ref[2] = cat([rrf] * 4 + [rrb] * 4)
        rows_ref[3] = cat([rif] * 4 + [rib] * 4)
        kmats_ref[0] = cat([bbr_f, -bbi_f, bbr_b, -bbi_b])
        kmats_ref[1] = cat([cre_f, cre_f, cre_b, cre_b])
        kmats_ref[2] = cat([-cim_f, cim_f, -cim_b, cim_b])
        krows_ref[0] = cat([krf, kif, qrb, qib])
        krows_ref[1] = cat([kif, krf, qib, qrb])

        pair, row0 = g // 2, (g % 2) * kw
        aw, bw, av, bv = [mats_ref[i] for i in range(4)]
        for s in range(T):
            r = pl.ds(pl.multiple_of(row0 + s * H, H), H)
            w_ref[pair, r, :] = (aw * rows_ref[0, pl.ds(s, 1), :] + bw * rows_ref[1, pl.ds(s, 1), :]).astype(BF16)
            vt_ref[pair, r, :] = (av * rows_ref[2, pl.ds(s, 1), :] + bv * rows_ref[3, pl.ds(s, 1), :]).astype(BF16)
        lhs, a4, b4 = [kmats_ref[i] for i in range(3)]
        blocks = []
        for i in range(2 * T):
            k = abs(i - (T - 1))
            if i == 2 * T - 1:
                blocks.append(jnp.zeros_like(a4))
                continue
            cp = a4 * krows_ref[0, pl.ds(k, 1), :] + b4 * krows_ref[1, pl.ds(k, 1), :]
            if i < T - 1:
                cp = jnp.where(lane >= half, cp, 0.0)
            elif i > T - 1:
                cp = jnp.where(lane < half, cp, 0.0)
            blocks.append(cp)
        rhs = jnp.concatenate(blocks, axis=0)
        lhs_hi = lhs.astype(BF16)
        lhs_lo = (lhs - lhs_hi.astype(F32)).astype(BF16)
        rhs_hi = rhs.astype(BF16)
        rhs_lo = (rhs - rhs_hi.astype(F32)).astype(BF16)
        z = _nt_dot(lhs_hi, rhs_hi) + _nt_dot(lhs_lo, rhs_hi) + _nt_dot(lhs_hi, rhs_lo)
        for s in range(T):
            off = (T - 1 - s) * H
            m_ref[g, s * H:(s + 1) * H, :] = z[:, off:off + kw].astype(BF16)
        return carry

    lax.fori_loop(0, pk_ref.shape[0], group, 0)


def _ssm_prep(packed):
    G, n_rows, P = packed.shape
    T = H = SSM_T
    gs = LANES // SSM_GROUP
    return pl.pallas_call(
        _ssm_prep_kernel,
        grid=(G // gs,),
        in_specs=[pl.BlockSpec((gs, n_rows, P), lambda j: (j, 0, 0))],
        out_specs=[pl.BlockSpec((gs, T * H, T * H), lambda j: (j, 0, 0)),
                   pl.BlockSpec((gs // 2, 2 * T * H, 8 * P), lambda j: (j, 0, 0)),
                   pl.BlockSpec((gs // 2, 2 * T * H, 8 * P), lambda j: (j, 0, 0))],
        out_shape=[jax.ShapeDtypeStruct((G, T * H, T * H), BF16),
                   jax.ShapeDtypeStruct((G // 2, 2 * T * H, 8 * P), BF16),
                   jax.ShapeDtypeStruct((G // 2, 2 * T * H, 8 * P), BF16)],
        scratch_shapes=[pltpu.VMEM((4, H, 8 * P), F32), pltpu.VMEM((4, T, 8 * P), F32),
                        pltpu.VMEM((3, H, 4 * P), F32), pltpu.VMEM((2, T, 4 * P), F32)],
        compiler_params=_params(("arbitrary",)),
        name="ssm_prep",
    )(packed)


def _slab_permutation():
    idx = jnp.arange(8 * LANES)
    t, g, h = idx // LANES, (idx % LANES) // SSM_GROUP, idx % SSM_GROUP
    dst = g * LANES + t * SSM_GROUP + h
    return (dst[:, None] == jnp.arange(8 * LANES)[None, :]).astype(BF16)


def _chunk_scan(s_ref, a_ref, *, bsz, n_pairs):
    n_it = s_ref.shape[0] // (2 * bsz)
    cols = [[pl.ds((4 * j + q) * LANES, LANES) for q in range(4)] for j in range(n_pairs)]
    decay = [[a_ref[:, c] for c in cols[j]] for j in range(n_pairs)]

    def step(i, carry):
        rf = pl.ds(pl.multiple_of(i * 2 * bsz, 2 * bsz), 2 * bsz)
        rb = pl.ds(pl.multiple_of((n_it - 1 - i) * 2 * bsz, 2 * bsz), 2 * bsz)
        out = []
        for j in range(n_pairs):
            ar_f, ai_f, ar_b, ai_b = decay[j]
            xr, xi, yr, yi = carry[4 * j:4 * j + 4]
            sr, si = s_ref[rf, cols[j][0]], s_ref[rf, cols[j][1]]
            xr1 = ar_f * xr - ai_f * xi + sr[:bsz]
            xi1 = ar_f * xi + ai_f * xr + si[:bsz]
            s_ref[rf, cols[j][0]] = jnp.concatenate([xr, xr1], axis=0)
            s_ref[rf, cols[j][1]] = jnp.concatenate([xi, xi1], axis=0)
            xr2 = ar_f * xr1 - ai_f * xi1 + sr[bsz:]
            xi2 = ar_f * xi1 + ai_f * xr1 + si[bsz:]
            tr, ti = s_ref[rb, cols[j][2]], s_ref[rb, cols[j][3]]
            yr1 = ar_b * yr - ai_b * yi + tr[bsz:]
            yi1 = ar_b * yi + ai_b * yr + ti[bsz:]
            s_ref[rb, cols[j][2]] = jnp.concatenate([yr1, yr], axis=0)
            s_ref[rb, cols[j][3]] = jnp.concatenate([yi1, yi], axis=0)
            yr2 = ar_b * yr1 - ai_b * yi1 + tr[:bsz]
            yi2 = ar_b * yi1 + ai_b * yr1 + ti[:bsz]
            out += [xr2, xi2, yr2, yi2]
        return tuple(out)

    z = jnp.zeros((bsz, LANES), F32)
    lax.fori_loop(0, n_it, step, (z,) * (4 * n_pairs))


def _ssm_kernel(u_ref, d_ref, perm_ref, iperm_ref, w_ref, m_ref, v_ref, a_ref, y_ref,
                ug_ref, s_ref, yg_ref, *, bsz):
    T = SSM_T
    R = u_ref.shape[0] // T
    n_groups = LANES // SSM_GROUP
    kw = T * SSM_GROUP
    for tq in range(T // 8):
        cat = jnp.concatenate([u_ref[pl.ds(tq * 8 + t, R, stride=T), :].astype(BF16) for t in range(8)], axis=-1)
        grouped = jnp.dot(cat, perm_ref[...], preferred_element_type=F32).astype(BF16)
        for g in range(n_groups):
            ug_ref[g, :, tq * LANES:(tq + 1) * LANES] = grouped[:, g * LANES:(g + 1) * LANES]
    for pr in range(n_groups // 2):
        u2 = jnp.concatenate([ug_ref[2 * pr], ug_ref[2 * pr + 1]], axis=-1)
        s_ref[:, pr * 4 * LANES:(pr + 1) * 4 * LANES] = jnp.dot(u2, w_ref[pr], preferred_element_type=F32)
    _chunk_scan(s_ref, a_ref, bsz=bsz, n_pairs=n_groups // 2)
    for pr in range(n_groups // 2):
        inter = _nt_dot(s_ref[:, pr * 4 * LANES:(pr + 1) * 4 * LANES].astype(BF16), v_ref[pr])
        for gl in range(2):
            g = 2 * pr + gl
            yg_ref[g] = (jnp.dot(ug_ref[g], m_ref[g], preferred_element_type=F32)
                         + inter[:, gl * kw:(gl + 1) * kw])
    for tq in range(T // 8):
        ycat = jnp.concatenate([yg_ref[g, :, tq * LANES:(tq + 1) * LANES] for g in range(n_groups)], axis=-1)
        y_hi = ycat.astype(BF16)
        y_lo = (ycat - y_hi.astype(F32)).astype(BF16)
        back = (jnp.dot(y_hi, iperm_ref[...], preferred_element_type=F32)
                + jnp.dot(y_lo, iperm_ref[...], preferred_element_type=F32))
        for t in range(8):
            rows = pl.ds(tq * 8 + t, R, stride=T)
            y_ref[rows, :] = back[:, t * LANES:(t + 1) * LANES] + d_ref[...] * u_ref[rows, :]


def _ssm(u_tok, d_skip, ops, bsz):
    m, wpair, vpair, a_chunk = ops
    N, d_ssm = u_tok.shape
    T = SSM_T
    R = N // T
    n_groups = LANES // SSM_GROUP
    perm = _slab_permutation()
    const = lambda shape: pl.BlockSpec(shape, lambda j: (0,) * len(shape))
    return pl.pallas_call(
        functools.partial(_ssm_kernel, bsz=bsz),
        grid=(d_ssm // LANES,),
        in_specs=[pl.BlockSpec((N, LANES), lambda j: (0, j)),
                  pl.BlockSpec((1, LANES), lambda j: (0, j)),
                  const(perm.shape), const(perm.shape),
                  pl.BlockSpec((n_groups // 2,) + wpair.shape[1:], lambda j: (j, 0, 0)),
                  pl.BlockSpec((n_groups,) + m.shape[1:], lambda j: (j, 0, 0)),
                  pl.BlockSpec((n_groups // 2,) + vpair.shape[1:], lambda j: (j, 0, 0)),
                  pl.BlockSpec((1, n_groups // 2 * 4 * LANES), lambda j: (0, j))],
        out_specs=pl.BlockSpec((N, LANES), lambda j: (0, j)),
        out_shape=jax.ShapeDtypeStruct((N, d_ssm), F32),
        scratch_shapes=[pltpu.VMEM((n_groups, R, T * SSM_GROUP), BF16),
                        pltpu.VMEM((R, n_groups // 2 * 4 * LANES), F32),
                        pltpu.VMEM((n_groups, R, T * SSM_GROUP), F32)],
        compiler_params=_params(("arbitrary",)),
        name="ssm",
    )(u_tok, d_skip, perm, perm.T, wpair, m, vpair, a_chunk)


def _attn_tables(rpb, rows):
    kh = min(WIN_H_MAX, rows)
    H = rpb.shape[0]
    w = jnp.arange(GRID_W)
    col_start = jnp.clip(w - WIN_W // 2, 0, GRID_W - WIN_W)
    col_mask = (w[None, :] >= col_start[:, None]) & (w[None, :] < col_start[:, None] + WIN_W)
    dc = jnp.clip(w[None, :] - w[:, None] + (WIN_W - 1), 0, 2 * WIN_W - 2)
    col_sel = (dc[:, :, None] == jnp.arange(2 * WIN_W - 1)[None, None, :]).astype(F32)
    tile = jnp.einsum('hdc,qkc->hdqk', rpb, col_sel, precision=lax.Precision.HIGHEST)
    tile = jnp.where(col_mask[None, None], tile * LOG2E, NEG)
    lo_pad, hi_pad = ATT_ROWS, ATT_WIN - kh + 1
    tile = jnp.pad(tile, ((0, 0), (lo_pad, hi_pad), (0, 0), (0, 0)))
    n_dr = tile.shape[1] - 1
    tiles = jnp.concatenate([tile[:, :-1], tile[:, 1:]], axis=-1).reshape(H // 2, 2, n_dr, GRID_W, 2 * GRID_W)
    i = jnp.arange(ATT_ROWS)
    last_union = rows - ATT_WIN
    last_rel = (rows - kh) - last_union
    rel = jnp.stack([jnp.zeros_like(i), i, jnp.full_like(i, last_rel)])
    j = jnp.arange(ATT_WIN)
    valid = (j[None, None, :] >= rel[:, :, None]) & (j[None, None, :] < rel[:, :, None] + kh)
    masks = jnp.where(valid, 0.0, NEG).astype(F32)
    masks = jnp.repeat(masks.reshape(3 * ATT_ROWS * ATT_WIN // 2, 2), GRID_W, axis=-1)
    return tiles, masks.reshape(-1, 1, 2 * GRID_W)


def _attn_kernel(q_ref, k_ref, v_ref, t_ref, m_ref, o_ref, *, rows, kh):
    n_blocks = rows // ATT_ROWS
    nq = ATT_ROWS * GRID_W
    nk = ATT_WIN * GRID_W
    n_pairs = ATT_WIN // 2
    pw = 2 * GRID_W
    lane = lax.broadcasted_iota(I32, (1, 2 * HEAD_DIM), 1)
    head_lanes = [lane < HEAD_DIM, lane >= HEAD_DIM]
    rel = [[0] * ATT_ROWS, list(range(ATT_ROWS)), [(rows - kh) - (rows - ATT_WIN)] * ATT_ROWS]

    def block(rb, pattern):
        r0 = rb * ATT_ROWS
        union = [0, r0 - kh // 2, rows - ATT_WIN][pattern]
        dr0 = [0, -(kh // 2), rows - ATT_WIN - (n_blocks - 1) * ATT_ROWS][pattern] + (WIN_H_MAX - 1) + ATT_ROWS
        aligned = lambda x, m: x if isinstance(x, int) else pl.multiple_of(x, m)
        q_rows = pl.ds(aligned(r0 * GRID_W, nq), nq)
        k_rows = pl.ds(aligned(union * GRID_W, GRID_W), nk)
        q2, k2, v2 = q_ref[0, q_rows, :], k_ref[0, k_rows, :], v_ref[0, k_rows, :]
        out = jnp.zeros((nq, 2 * HEAD_DIM), F32)
        for hl in range(2):
            qm = jnp.where(head_lanes[hl], q2, 0.0).astype(BF16)
            s = _nt_dot(qm, k2)
            probs, denoms = [], []
            for i in range(ATT_ROWS):
                first, last = rel[pattern][i], rel[pattern][i] + kh - 1
                pairs = range(first // 2, last // 2 + 1)
                bias = []
                for jp in pairs:
                    tile = t_ref[0, hl, dr0 + 2 * jp - i]
                    if 2 * jp < first or 2 * jp + 1 > last:
                        tile = tile + m_ref[(pattern * ATT_ROWS + i) * n_pairs + jp]
                    bias.append(tile)
                si = s[i * GRID_W:(i + 1) * GRID_W, pairs[0] * pw:(pairs[-1] + 1) * pw] + jnp.concatenate(bias, axis=-1)
                pi = jnp.exp2(si - jnp.max(si, axis=-1, keepdims=True))
                denoms.append(jnp.sum(pi, axis=-1, keepdims=True))
                pieces = [pi.astype(BF16)]
                if pairs[0] > 0:
                    pieces.insert(0, jnp.zeros((GRID_W, pairs[0] * pw), BF16))
                if pairs[-1] + 1 < n_pairs:
                    pieces.append(jnp.zeros((GRID_W, (n_pairs - 1 - pairs[-1]) * pw), BF16))
                probs.append(jnp.concatenate(pieces, axis=-1))
            vm = jnp.where(head_lanes[hl], v2, 0.0).astype(BF16)
            pv = jnp.dot(jnp.concatenate(probs, axis=0), vm, preferred_element_type=F32)
            out = out + pv / jnp.concatenate(denoms, axis=0)
        o_ref[0, q_rows, :] = out

    def interior(rb, carry):
        block(rb, 1)
        return carry

    block(0, 0)
    lax.fori_loop(1, n_blocks - 1, interior, 0, unroll=2)
    block(n_blocks - 1, 2)


def _attn(qkv, tiles, masks):
    bsz, L, n3 = qkv.shape
    d = n3 // 3
    rows = L // GRID_W
    kh = min(WIN_H_MAX, rows)
    assert rows % ATT_ROWS == 0 and rows // ATT_ROWS >= 3 and ATT_ROWS == kh // 2
    assert ATT_WIN % 2 == 0 and kh + ATT_ROWS - 1 <= ATT_WIN <= rows
    n_hp = d // (2 * HEAD_DIM)
    blk = pl.BlockSpec((1, L, 2 * HEAD_DIM), lambda b, hp: (b, 0, hp))
    return pl.pallas_call(
        functools.partial(_attn_kernel, rows=rows, kh=kh),
        grid=(bsz, n_hp),
        in_specs=[blk,
                  pl.BlockSpec((1, L, 2 * HEAD_DIM), lambda b, hp: (b, 0, n_hp + hp)),
                  pl.BlockSpec((1, L, 2 * HEAD_DIM), lambda b, hp: (b, 0, 2 * n_hp + hp)),
                  pl.BlockSpec((1,) + tiles.shape[1:], lambda b, hp: (hp, 0, 0, 0, 0)),
                  pl.BlockSpec(masks.shape, lambda b, hp: (0, 0, 0))],
        out_specs=blk,
        out_shape=jax.ShapeDtypeStruct((bsz, L, d), F32),
        compiler_params=_params(("arbitrary", "arbitrary")),
        name="attn",
    )(qkv, qkv, qkv, tiles, masks)


def _mix_kernel(ys_ref, wglu_ref, bglu_ref, gs_ref, ya_ref, ga_ref, wout_ref,
                x_ref, gt_ref, gf_ref, sh_ref, sc_ref, wr_ref,
                x1_ref, h2_ref, lg_ref):
    d_ssm = ys_ref.shape[3]
    y = ys_ref[:, 0].reshape(-1, d_ssm)
    z = jax.nn.gelu(y)
    gate = _sigmoid(jnp.dot(z.astype(BF16), wglu_ref[...], preferred_element_type=F32) + bglu_ref[...])
    a = _rms(z * gate, gs_ref[...]).astype(BF16)
    t = _rms(ya_ref[0], ga_ref[...]).astype(BF16)
    mixed = (jnp.dot(a, wout_ref[:d_ssm, :], preferred_element_type=F32)
             + jnp.dot(t, wout_ref[d_ssm:, :], preferred_element_type=F32))
    x1 = x_ref[0] + gt_ref[0] * mixed
    x1_ref[0] = x1
    h2 = _rms(x1, gf_ref[...]) * (1.0 + sc_ref[0]) + sh_ref[0]
    h2_hi = h2.astype(BF16)
    h2_ref[0] = h2_hi
    h2_lo = (h2 - h2_hi.astype(F32)).astype(BF16)
    wr = wr_ref[...]
    wr_hi = wr.astype(BF16)
    wr_lo = (wr - wr_hi.astype(F32)).astype(BF16)
    n_e = wr.shape[1]
    tm = h2.shape[0]
    r = jnp.dot(jnp.concatenate([h2_hi, h2_lo], axis=0), jnp.concatenate([wr_hi, wr_lo], axis=1),
                preferred_element_type=F32)
    lg_ref[0] = r[:tm, :n_e] + r[:tm, n_e:] + r[tm:, :n_e]


def _mix(ys, wglu_bf, b_glu, g_ssm, ya, g_attn, wout_bf, x, gt1, g_ffn, sh2, sc2, w_router):
    bsz, L, D = x.shape
    d_ssm = ys.shape[3]
    E = w_router.shape[1]
    tm = 256
    T = SSM_T
    tile = lambda n: pl.BlockSpec((1, tm, n), lambda b, i: (b, i, 0))
    per_b = lambda n: pl.BlockSpec((1, 1, n), lambda b, i: (b, 0, 0))
    full = lambda r, n: pl.BlockSpec((r, n), lambda b, i: (0, 0))
    return pl.pallas_call(
        _mix_kernel,
        grid=(bsz, L // tm),
        in_specs=[pl.BlockSpec((tm // T, 1, T, d_ssm), lambda b, i: (i, b, 0, 0)),
                  full(d_ssm, d_ssm), full(1, d_ssm), full(1, d_ssm),
                  tile(d_ssm), full(1, d_ssm), full(D, D),
                  tile(D), per_b(D), full(1, D), per_b(D), per_b(D), full(D, E)],
        out_specs=[tile(D), tile(D), tile(E)],
        out_shape=[jax.ShapeDtypeStruct((bsz, L, D), F32),
                   jax.ShapeDtypeStruct((bsz, L, D), BF16),
                   jax.ShapeDtypeStruct((bsz, L, E), F32)],
        compiler_params=_params(("arbitrary", "arbitrary")),
        name="mix",
    )(ys, wglu_bf, b_glu, g_ssm, ya, g_attn, wout_bf, x, gt1, g_ffn, sh2, sc2, w_router)


TOPK_BISECTIONS = 160


def _topk_kernel(lg_ref, aff_ref, pos_ref, *, cap):
    lg = lg_ref[...]
    bsz, E, L = lg.shape
    e = jnp.exp(lg - jnp.max(lg, axis=1, keepdims=True))
    aff3 = e / jnp.sum(e, axis=1, keepdims=True)
    aff_ref[...] = aff3
    aff = aff3.reshape(bsz * E, L)

    def count(mask):
        return jnp.sum(jnp.where(mask, 1.0, 0.0), axis=-1, keepdims=True)

    def halve(_, carry):
        lo, hi = carry
        mid = lo + (hi - lo) * 0.5
        ok = count(aff >= mid) >= cap
        return jnp.where(ok, mid, lo), jnp.where(ok, hi, mid)

    lo0 = jnp.zeros((bsz * E, 1), F32)
    thr, _ = lax.fori_loop(0, TOPK_BISECTIONS, halve, (lo0, lo0 + 2.0))
    gt = aff > thr
    eq = aff == thr
    need = cap - count(gt)
    blk = LANES
    tri = jnp.where(lax.broadcasted_iota(I32, (blk, blk), 0) < lax.broadcasted_iota(I32, (blk, blk), 1),
                    1.0, 0.0).astype(BF16)

    def prefix_count(mask):
        ones = jnp.where(mask, 1.0, 0.0)
        run = jnp.zeros((bsz * E, 1), F32)
        outs = []
        for j in range(L // blk):
            piece = ones[:, j * blk:(j + 1) * blk]
            outs.append(jnp.dot(piece.astype(BF16), tri, preferred_element_type=F32) + run)
            run = run + jnp.sum(piece, axis=-1, keepdims=True)
        return jnp.concatenate(outs, axis=-1)

    sel = gt | (eq & (prefix_count(eq) < need))
    pos = prefix_count(sel)
    pos_ref[...] = jnp.where(sel, pos.astype(I32), -1).reshape(bsz, E, L)


def _topk(lg_t, cap):
    bsz, E, L = lg_t.shape
    spec = pl.BlockSpec((bsz, E, L), lambda i: (0, 0, 0))
    return pl.pallas_call(
        functools.partial(_topk_kernel, cap=cap),
        grid=(1,),
        in_specs=[spec],
        out_specs=[spec, spec],
        out_shape=[jax.ShapeDtypeStruct((bsz, E, L), F32), jax.ShapeDtypeStruct((bsz, E, L), I32)],
        compiler_params=_params(("arbitrary",)),
        name="topk",
    )(lg_t)


def _gather_kernel(pos_ref, aff_ref, h_ref, xe_ref, as_ref, *, cap):
    L = h_ref.shape[1]
    slot = lax.broadcasted_iota(I32, (cap, L), 0)
    hit = pos_ref[0, 0] == slot
    xe_ref[0] = jnp.dot(jnp.where(hit, 1.0, 0.0).astype(BF16), h_ref[0],
                        preferred_element_type=F32).astype(BF16)
    as_ref[0] = jnp.sum(jnp.where(hit, aff_ref[0, 0], 0.0), axis=-1, keepdims=True)


def _gather(pos, aff, h2, cap):
    bsz, E, L = pos.shape
    D = h2.shape[2]
    pos4 = pos.reshape(bsz, E, 1, L)
    aff4 = aff.reshape(bsz, E, 1, L)
    return pl.pallas_call(
        functools.partial(_gather_kernel, cap=cap),
        grid=(bsz, E),
        in_specs=[pl.BlockSpec((1, 1, 1, L), lambda b, e: (b, e, 0, 0)),
                  pl.BlockSpec((1, 1, 1, L), lambda b, e: (b, e, 0, 0)),
                  pl.BlockSpec((1, L, D), lambda b, e: (b, 0, 0))],
        out_specs=[pl.BlockSpec((1, cap, D), lambda b, e: (e, b, 0)),
                   pl.BlockSpec((1, cap, 1), lambda b, e: (e, b, 0))],
        out_shape=[jax.ShapeDtypeStruct((E, bsz * cap, D), BF16),
                   jax.ShapeDtypeStruct((E, bsz * cap, 1), F32)],
        compiler_params=_params(("arbitrary", "arbitrary")),
        name="gather",
    )(pos4, aff4, h2)


def _ffn_kernel(xe_ref, wg_ref, wu_ref, wd_ref, as_ref, ye_ref, acc_ref):
    f = pl.program_id(1)

    @pl.when(f == 0)
    def _():
        acc_ref[...] = jnp.zeros_like(acc_ref)

    x = xe_ref[0]
    g = jnp.dot(x, wg_ref[0].astype(BF16), preferred_element_type=F32)
    up = jnp.dot(x, wu_ref[0].astype(BF16), preferred_element_type=F32)
    hid = (g * _sigmoid(g) * up).astype(BF16)
    acc_ref[...] += jnp.dot(hid, wd_ref[0].astype(BF16), preferred_element_type=F32)

    @pl.when(f == pl.num_programs(1) - 1)
    def _():
        ye_ref[0] = (acc_ref[...] * as_ref[0]).astype(BF16)


def _ffn(xe, w_gate, w_up, w_down, aff_slot):
    E, R, D = xe.shape
    F = w_gate.shape[2]
    tf = 256
    return pl.pallas_call(
        _ffn_kernel,
        grid=(E, F // tf),
        in_specs=[pl.BlockSpec((1, R, D), lambda e, f: (e, 0, 0)),
                  pl.BlockSpec((1, D, tf), lambda e, f: (e, 0, f)),
                  pl.BlockSpec((1, D, tf), lambda e, f: (e, 0, f)),
                  pl.BlockSpec((1, tf, D), lambda e, f: (e, f, 0)),
                  pl.BlockSpec((1, R, 1), lambda e, f: (e, 0, 0))],
        out_specs=pl.BlockSpec((1, R, D), lambda e, f: (e, 0, 0)),
        out_shape=jax.ShapeDtypeStruct((E, R, D), BF16),
        scratch_shapes=[pltpu.VMEM((R, D), F32)],
        compiler_params=_params(("arbitrary", "arbitrary")),
        name="ffn",
    )(xe, w_gate, w_up, w_down, aff_slot)


def _combine_kernel(pos_ref, ye_ref, x1_ref, gt_ref, g_ref, o_ref, acc_ref, *, cap, final_norm):
    e = pl.program_id(2)

    @pl.when(e == 0)
    def _():
        acc_ref[...] = jnp.zeros_like(acc_ref)

    pos_t = pos_ref[0]
    lane = lax.broadcasted_iota(I32, pos_t.shape, 1)
    pcol = jnp.sum(jnp.where(lane == e, pos_t, 0).astype(F32), axis=-1, keepdims=True).astype(I32)
    slot = lax.broadcasted_iota(I32, (pos_t.shape[0], cap), 1)
    onehot = jnp.where(pcol == slot, 1.0, 0.0).astype(BF16)
    acc_ref[...] += jnp.dot(onehot, ye_ref[0], preferred_element_type=F32)

    @pl.when(e == pl.num_programs(2) - 1)
    def _():
        o = x1_ref[0] + gt_ref[0] * acc_ref[...]
        o_ref[0] = _rms(o, g_ref[...]) if final_norm else o


def _combine(pos_t, ye, x1, gt2, g_final, cap, final_norm):
    bsz, L, E = pos_t.shape
    D = x1.shape[2]
    tl = 1024
    return pl.pallas_call(
        functools.partial(_combine_kernel, cap=cap, final_norm=final_norm),
        grid=(bsz, L // tl, E),
        in_specs=[pl.BlockSpec((1, tl, E), lambda b, i, e: (b, i, 0)),
                  pl.BlockSpec((1, cap, D), lambda b, i, e: (e, b, 0)),
                  pl.BlockSpec((1, tl, D), lambda b, i, e: (b, i, 0)),
                  pl.BlockSpec((1, 1, D), lambda b, i, e: (b, 0, 0)),
                  pl.BlockSpec((1, D), lambda b, i, e: (0, 0))],
        out_specs=pl.BlockSpec((1, tl, D), lambda b, i, e: (b, i, 0)),
        out_shape=jax.ShapeDtypeStruct((bsz, L, D), F32),
        scratch_shapes=[pltpu.VMEM((tl, D), F32)],
        compiler_params=_params(("arbitrary", "arbitrary", "arbitrary")),
        name="combine",
    )(pos_t, ye, x1, gt2, g_final)


def kernel(x, c, w_ada, b_ada, g_mix, w_in, ssm_a_re, ssm_a_im, ssm_log_dt, ssm_b_re, ssm_b_im,
           ssm_c_re, ssm_c_im, ssm_d, w_glu, b_glu, rpb, g_ssm_out, g_attn_out, w_out, g_ffn,
           w_router, w_gate, w_up, w_down, g_final):
    bsz, L, D = x.shape
    depth = w_ada.shape[0]
    d_ssm = ssm_d.shape[1]
    E = w_router.shape[2]
    cap = CAPACITY_FACTOR * L // E
    rows = L // GRID_W
    c8 = jnp.zeros((8, D), F32).at[:bsz].set(c)
    row = lambda v: v.reshape(1, -1)
    for layer in range(depth):
        mod = _ada(c8, w_ada[layer], row(b_ada[layer]))[:bsz]
        sh1, sc1, gt1, sh2, sc2, gt2 = [m.reshape(bsz, 1, D) for m in jnp.split(mod, 6, axis=-1)]

        u, qkv = _inproj(x, sh1, sc1, row(g_mix[layer]), _cast_bf16(w_in[layer]), d_ssm)
        ops = _ssm_operators(ssm_a_re[layer], ssm_a_im[layer], ssm_log_dt[layer], ssm_b_re[layer],
                             ssm_b_im[layer], ssm_c_re[layer], ssm_c_im[layer])
        y_ssm = _ssm(u.reshape(-1, d_ssm), row(ssm_d[layer]), ops, bsz).reshape(u.shape)
        y_attn = _attn(qkv, *_attn_tables(rpb[layer], rows))
        x1, h2, logits = _mix(y_ssm, _cast_bf16(w_glu[layer]), row(b_glu[layer]),
                              row(g_ssm_out[layer]), y_attn, row(g_attn_out[layer]), _cast_bf16(w_out[layer]),
                              x, gt1, row(g_ffn[layer]), sh2, sc2, w_router[layer])

        aff, pos = _topk(jnp.swapaxes(logits, 1, 2), cap)
        xe, aff_slot = _gather(pos, aff, h2, cap)
        ye = _ffn(xe, w_gate[layer], w_up[layer], w_down[layer], aff_slot)
        x = _combine(jnp.swapaxes(pos, 1, 2), ye, x1, gt2, row(g_final), cap, layer == depth - 1)
    return x
```

```python
import functools
import math

import jax
import jax.numpy as jnp
from jax import lax
from jax.experimental import pallas as pl
from jax.experimental.pallas import tpu as pltpu

F32 = jnp.float32
BF16 = jnp.bfloat16
I32 = jnp.int32

EPS = 1e-6
GRID_W = 64
SSM_GROUP = 16
SSM_STATE = 64
N_HEADS = 16
HEAD_DIM = 64
WIN_H_MAX = 8
WIN_W = 16
N_EXPERTS = 16
CAPACITY_FACTOR = 2
SSM_T = 16
LANES = 128
NEG = -1e30
LOG2E = math.log2(math.e)

ATT_ROWS = 4
ATT_WIN = 12

VMEM_LIMIT = 56 * 1024 * 1024


def _params(sem, vmem=VMEM_LIMIT):
    return pltpu.CompilerParams(dimension_semantics=sem, vmem_limit_bytes=vmem)


def _sigmoid(x):
    return 1.0 / (1.0 + jnp.exp(-x))


def _rms(x, g):
    return x * lax.rsqrt(jnp.mean(x * x, axis=-1, keepdims=True) + EPS) * g


def _ada_kernel(c_ref, w_ref, b_ref, o_ref):
    c = c_ref[...]
    ca = (c * _sigmoid(c)).astype(BF16)
    o_ref[...] = jnp.dot(ca, w_ref[...].astype(BF16), preferred_element_type=F32) + b_ref[...]


def _ada(c8, w, b):
    k, n = w.shape
    tn = 1024
    return pl.pallas_call(
        _ada_kernel,
        grid=(n // tn,),
        in_specs=[pl.BlockSpec((8, k), lambda j: (0, 0)),
                  pl.BlockSpec((k, tn), lambda j: (0, j)),
                  pl.BlockSpec((1, tn), lambda j: (0, j))],
        out_specs=pl.BlockSpec((8, tn), lambda j: (0, j)),
        out_shape=jax.ShapeDtypeStruct((8, n), F32),
        compiler_params=_params(("arbitrary",)),
        name="ada",
    )(c8, w, b)


def _cast_kernel(w_ref, o_ref):
    o_ref[...] = w_ref[...].astype(BF16)


def _cast_bf16(w):
    k, n = w.shape
    tk = 256
    return pl.pallas_call(
        _cast_kernel,
        grid=(k // tk,),
        in_specs=[pl.BlockSpec((tk, n), lambda i: (i, 0))],
        out_specs=pl.BlockSpec((tk, n), lambda i: (i, 0)),
        out_shape=jax.ShapeDtypeStruct((k, n), BF16),
        compiler_params=_params(("arbitrary",)),
        name="cast",
    )(w)


def _inproj_kernel(x_ref, sh_ref, sc_ref, g_ref, w_ref, u_ref, qkv_ref, *, d_ssm):
    h = (_rms(x_ref[0], g_ref[...]) * (1.0 + sc_ref[0]) + sh_ref[0]).astype(BF16)
    n_total = w_ref.shape[1]
    for n in range(n_total // d_ssm):
        r = jnp.dot(h, w_ref[:, n * d_ssm:(n + 1) * d_ssm], preferred_element_type=F32)
        if n == 0:
            for cc in range(u_ref.shape[0]):
                u_ref[cc, 0] = r[cc * SSM_T:(cc + 1) * SSM_T]
        else:
            if n == 1:
                r = r * (HEAD_DIM ** -0.5 * LOG2E)
            qkv_ref[0, :, (n - 1) * d_ssm:n * d_ssm] = r.astype(BF16)


def _inproj(x, sh, sc, g, w_bf, d_ssm):
    bsz, L, D = x.shape
    n = w_bf.shape[1]
    tm = 256
    T = SSM_T
    return pl.pallas_call(
        functools.partial(_inproj_kernel, d_ssm=d_ssm),
        grid=(bsz, L // tm),
        in_specs=[pl.BlockSpec((1, tm, D), lambda b, i: (b, i, 0)),
                  pl.BlockSpec((1, 1, D), lambda b, i: (b, 0, 0)),
                  pl.BlockSpec((1, 1, D), lambda b, i: (b, 0, 0)),
                  pl.BlockSpec((1, D), lambda b, i: (0, 0)),
                  pl.BlockSpec((D, n), lambda b, i: (0, 0))],
        out_specs=[pl.BlockSpec((tm // T, 1, T, d_ssm), lambda b, i: (i, b, 0, 0)),
                   pl.BlockSpec((1, tm, n - d_ssm), lambda b, i: (b, i, 0))],
        out_shape=[jax.ShapeDtypeStruct((L // T, bsz, T, d_ssm), F32),
                   jax.ShapeDtypeStruct((bsz, L, n - d_ssm), BF16)],
        compiler_params=_params(("arbitrary", "arbitrary")),
        name="inproj",
    )(x, sh, sc, g, w_bf)


def _ssm_operators(a_re, a_im, log_dt, b_re, b_im, c_re, c_im):
    T = SSM_T
    G, P = a_re.shape[1], a_re.shape[2]
    H = b_re.shape[3]
    dt = jnp.exp(log_dt)[..., None]
    mag = jnp.exp(a_re * dt)
    lbr = mag * jnp.cos(a_im * dt)
    lbi = mag * jnp.sin(a_im * dt)
    den = a_re * a_re + a_im * a_im
    nr = lbr - 1.0
    fr = ((nr * a_re + lbi * a_im) / den)[:, :, None, :]
    fi = ((lbi * a_re - nr * a_im) / den)[:, :, None, :]
    b_re_t, b_im_t = jnp.swapaxes(b_re, 2, 3), jnp.swapaxes(b_im, 2, 3)
    bbr = fr * b_re_t - fi * b_im_t
    bbi = fr * b_im_t + fi * b_re_t
    prs, pis = [jnp.ones_like(lbr)], [jnp.zeros_like(lbi)]
    for _ in range(T):
        pr, pi = prs[-1], pis[-1]
        prs.append(pr * lbr - pi * lbi)
        pis.append(pr * lbi + pi * lbr)
    pr = jnp.stack(prs, axis=2)
    pi = jnp.stack(pis, axis=2)
    assert H == T
    rev = lambda x: x[:, ::-1]
    sections = [bbr[0], bbi[0], bbr[1], bbi[1], c_re[0], c_im[0], c_re[1], c_im[1],
                rev(pr[0, :, :T]), rev(pi[0, :, :T]), pr[1, :, :T], pi[1, :, :T],
                pr[0, :, 1:], pi[0, :, 1:], rev(pr[1, :, 1:]), rev(pi[1, :, 1:]),
                pr[0, :, :T], pi[0, :, :T]]
    m, wpair, vtpair = _ssm_prep(jnp.concatenate(sections, axis=1))
    a4 = jnp.stack([pr[0, :, T], pi[0, :, T], pr[1, :, T], pi[1, :, T]])
    a_chunk = jnp.transpose(a4.reshape(4, G // 2, 2, P), (1, 0, 2, 3)).reshape(1, G // 2 * 8 * P)
    return m, wpair, vtpair, a_chunk


def _nt_dot(a, b):
    return lax.dot_general(a, b, (((1,), (1,)), ((), ())), preferred_element_type=F32)


def _ssm_prep_kernel(pk_ref, m_ref, w_ref, vt_ref, mats_ref, rows_ref, kmats_ref, krows_ref):
    T = SSM_T
    H = T
    kw = T * H
    half = kmats_ref.shape[2] // 2
    lane = lax.broadcasted_iota(I32, (1, 2 * half), 1)
    cat = lambda xs: jnp.concatenate(xs, axis=-1)

    def group(g, carry):
        (bbr_f, bbi_f, bbr_b, bbi_b, cre_f, cim_f, cre_b, cim_b, qrf, qif, qrb, qib,
         rrf, rif, rrb, rib, krf, kif) = [pk_ref[g, i * T:(i + 1) * T, :] for i in range(18)]
        zero = jnp.zeros_like(bbr_f)
        even = g % 2 == 0

        def slots(v4):
            ev = cat([v4[0], zero, v4[1], zero, v4[2], zero, v4[3], zero])
            od = cat([zero, v4[0], zero, v4[1], zero, v4[2], zero, v4[3]])
            return jnp.where(even, ev, od)

        mats_ref[0] = slots([bbr_f, bbi_f, bbr_b, bbi_b])
        mats_ref[1] = slots([-bbi_f, bbr_f, -bbi_b, bbr_b])
        mats_ref[2] = slots([cre_f, -cim_f, cre_b, -cim_b])
        mats_ref[3] = slots([-cim_f, -cre_f, -cim_b, -cre_b])
        rows_ref[0] = cat([qrf] * 4 + [qrb] * 4)
        rows_ref[1] = cat([qif] * 4 + [qib] * 4)
        rows_ref[2] = cat([rrf] * 4 + [rrb] * 4)
        rows_ref[3] = cat([rif] * 4 + [rib] * 4)
        kmats_ref[0] = cat([bbr_f, -bbi_f, bbr_b, -bbi_b])
        kmats_ref[1] = cat([cre_f, cre_f, cre_b, cre_b])
        kmats_ref[2] = cat([-cim_f, cim_f, -cim_b, cim_b])
        krows_ref[0] = cat([krf, kif, qrb, qib])
        krows_ref[1] = cat([kif, krf, qib, qrb])

        pair, row0 = g // 2, (g % 2) * kw
        aw, bw, av, bv = [mats_ref[i] for i in range(4)]
        for s in range(T):
            r = pl.ds(pl.multiple_of(row0 + s * H, H), H)
            w_ref[pair, r, :] = (aw * rows_ref[0, pl.ds(s, 1), :] + bw * rows_ref[1, pl.ds(s, 1), :]).astype(BF16)
            vt_ref[pair, r, :] = (av * rows_ref[2, pl.ds(s, 1), :] + bv * rows_ref[3, pl.ds(s, 1), :]).astype(BF16)
        lhs, a4, b4 = [kmats_ref[i] for i in range(3)]
        blocks = []
        for i in range(2 * T):
            k = abs(i - (T - 1))
            if i == 2 * T - 1:
                blocks.append(jnp.zeros_like(a4))
                continue
            cp = a4 * krows_ref[0, pl.ds(k, 1), :] + b4 * krows_ref[1, pl.ds(k, 1), :]
            if i < T - 1:
                cp = jnp.where(lane >= half, cp, 0.0)
            elif i > T - 1:
                cp = jnp.where(lane < half, cp, 0.0)
            blocks.append(cp)
        rhs = jnp.concatenate(blocks, axis=0)
        lhs_hi = lhs.astype(BF16)
        lhs_lo = (lhs - lhs_hi.astype(F32)).astype(BF16)
        rhs_hi = rhs.astype(BF16)
        rhs_lo = (rhs - rhs_hi.astype(F32)).astype(BF16)
        z = _nt_dot(lhs_hi, rhs_hi) + _nt_dot(lhs_lo, rhs_hi) + _nt_dot(lhs_hi, rhs_lo)
        for s in range(T):
            off = (T - 1 - s) * H
            m_ref[g, s * H:(s + 1) * H, :] = z[:, off:off + kw].astype(BF16)
        return carry

    lax.fori_loop(0, pk_ref.shape[0], group, 0)


def _ssm_prep(packed):
    G, n_rows, P = packed.shape
    T = H = SSM_T
    gs = LANES // SSM_GROUP
    return pl.pallas_call(
        _ssm_prep_kernel,
        grid=(G // gs,),
        in_specs=[pl.BlockSpec((gs, n_rows, P), lambda j: (j, 0, 0))],
        out_specs=[pl.BlockSpec((gs, T * H, T * H), lambda j: (j, 0, 0)),
                   pl.BlockSpec((gs // 2, 2 * T * H, 8 * P), lambda j: (j, 0, 0)),
                   pl.BlockSpec((gs // 2, 2 * T * H, 8 * P), lambda j: (j, 0, 0))],
        out_shape=[jax.ShapeDtypeStruct((G, T * H, T * H), BF16),
                   jax.ShapeDtypeStruct((G // 2, 2 * T * H, 8 * P), BF16),
                   jax.ShapeDtypeStruct((G // 2, 2 * T * H, 8 * P), BF16)],
        scratch_shapes=[pltpu.VMEM((4, H, 8 * P), F32), pltpu.VMEM((4, T, 8 * P), F32),
                        pltpu.VMEM((3, H, 4 * P), F32), pltpu.VMEM((2, T, 4 * P), F32)],
        compiler_params=_params(("arbitrary",)),
        name="ssm_prep",
    )(packed)


def _slab_permutation():
    idx = jnp.arange(8 * LANES)
    t, g, h = idx // LANES, (idx % LANES) // SSM_GROUP, idx % SSM_GROUP
    dst = g * LANES + t * SSM_GROUP + h
    return (dst[:, None] == jnp.arange(8 * LANES)[None, :]).astype(BF16)


def _chunk_scan(s_ref, a_ref, *, bsz, n_pairs):
    n_it = s_ref.shape[0] // (2 * bsz)
    cols = [[pl.ds((4 * j + q) * LANES, LANES) for q in range(4)] for j in range(n_pairs)]
    decay = [[a_ref[:, c] for c in cols[j]] for j in range(n_pairs)]

    def step(i, carry):
        rf = pl.ds(pl.multiple_of(i * 2 * bsz, 2 * bsz), 2 * bsz)
        rb = pl.ds(pl.multiple_of((n_it - 1 - i) * 2 * bsz, 2 * bsz), 2 * bsz)
        out = []
        for j in range(n_pairs):
            ar_f, ai_f, ar_b, ai_b = decay[j]
            xr, xi, yr, yi = carry[4 * j:4 * j + 4]
            sr, si = s_ref[rf, cols[j][0]], s_ref[rf, cols[j][1]]
            xr1 = ar_f * xr - ai_f * xi + sr[:bsz]
            xi1 = ar_f * xi + ai_f * xr + si[:bsz]
            s_ref[rf, cols[j][0]] = jnp.concatenate([xr, xr1], axis=0)
            s_ref[rf, cols[j][1]] = jnp.concatenate([xi, xi1], axis=0)
            xr2 = ar_f * xr1 - ai_f * xi1 + sr[bsz:]
            xi2 = ar_f * xi1 + ai_f * xr1 + si[bsz:]
            tr, ti = s_ref[rb, cols[j][2]], s_ref[rb, cols[j][3]]
            yr1 = ar_b * yr - ai_b * yi + tr[bsz:]
            yi1 = ar_b * yi + ai_b * yr + ti[bsz:]
            s_ref[rb, cols[j][2]] = jnp.concatenate([yr1, yr], axis=0)
            s_ref[rb, cols[j][3]] = jnp.concatenate([yi1, yi], axis=0)
            yr2 = ar_b * yr1 - ai_b * yi1 + tr[:bsz]
            yi2 = ar_b * yi1 + ai_b * yr1 + ti[:bsz]
            out += [xr2, xi2, yr2, yi2]
        return tuple(out)

    z = jnp.zeros((bsz, LANES), F32)
    lax.fori_loop(0, n_it, step, (z,) * (4 * n_pairs))


def _ssm_kernel(u_ref, d_ref, perm_ref, iperm_ref, w_ref, m_ref, v_ref, a_ref, y_ref,
                ug_ref, s_ref, yg_ref, *, bsz):
    T = SSM_T
    R = u_ref.shape[0] // T
    n_groups = LANES // SSM_GROUP
    kw = T * SSM_GROUP
    for tq in range(T // 8):
        cat = jnp.concatenate([u_ref[pl.ds(tq * 8 + t, R, stride=T), :].astype(BF16) for t in range(8)], axis=-1)
        grouped = jnp.dot(cat, perm_ref[...], preferred_element_type=F32).astype(BF16)
        for g in range(n_groups):
            ug_ref[g, :, tq * LANES:(tq + 1) * LANES] = grouped[:, g * LANES:(g + 1) * LANES]
    for pr in range(n_groups // 2):
        u2 = jnp.concatenate([ug_ref[2 * pr], ug_ref[2 * pr + 1]], axis=-1)
        s_ref[:, pr * 4 * LANES:(pr + 1) * 4 * LANES] = jnp.dot(u2, w_ref[pr], preferred_element_type=F32)
    _chunk_scan(s_ref, a_ref, bsz=bsz, n_pairs=n_groups // 2)
    for pr in range(n_groups // 2):
        inter = _nt_dot(s_ref[:, pr * 4 * LANES:(pr + 1) * 4 * LANES].astype(BF16), v_ref[pr])
        for gl in range(2):
            g = 2 * pr + gl
            yg_ref[g] = (jnp.dot(ug_ref[g], m_ref[g], preferred_element_type=F32)
                         + inter[:, gl * kw:(gl + 1) * kw])
    for tq in range(T // 8):
        ycat = jnp.concatenate([yg_ref[g, :, tq * LANES:(tq + 1) * LANES] for g in range(n_groups)], axis=-1)
        y_hi = ycat.astype(BF16)
        y_lo = (ycat - y_hi.astype(F32)).astype(BF16)
        back = (jnp.dot(y_hi, iperm_ref[...], preferred_element_type=F32)
                + jnp.dot(y_lo, iperm_ref[...], preferred_element_type=F32))
        for t in range(8):
            rows = pl.ds(tq * 8 + t, R, stride=T)
            y_ref[rows, :] = back[:, t * LANES:(t + 1) * LANES] + d_ref[...] * u_ref[rows, :]


def _ssm(u_tok, d_skip, ops, bsz):
    m, wpair, vpair, a_chunk = ops
    N, d_ssm = u_tok.shape
    T = SSM_T
    R = N // T
    n_groups = LANES // SSM_GROUP
    perm = _slab_permutation()
    const = lambda shape: pl.BlockSpec(shape, lambda j: (0,) * len(shape))
    return pl.pallas_call(
        functools.partial(_ssm_kernel, bsz=bsz),
        grid=(d_ssm // LANES,),
        in_specs=[pl.BlockSpec((N, LANES), lambda j: (0, j)),
                  pl.BlockSpec((1, LANES), lambda j: (0, j)),
                  const(perm.shape), const(perm.shape),
                  pl.BlockSpec((n_groups // 2,) + wpair.shape[1:], lambda j: (j, 0, 0)),
                  pl.BlockSpec((n_groups,) + m.shape[1:], lambda j: (j, 0, 0)),
                  pl.BlockSpec((n_groups // 2,) + vpair.shape[1:], lambda j: (j, 0, 0)),
                  pl.BlockSpec((1, n_groups // 2 * 4 * LANES), lambda j: (0, j))],
        out_specs=pl.BlockSpec((N, LANES), lambda j: (0, j)),
        out_shape=jax.ShapeDtypeStruct((N, d_ssm), F32),
        scratch_shapes=[pltpu.VMEM((n_groups, R, T * SSM_GROUP), BF16),
                        pltpu.VMEM((R, n_groups // 2 * 4 * LANES), F32),
                        pltpu.VMEM((n_groups, R, T * SSM_GROUP), F32)],
        compiler_params=_params(("arbitrary",)),
        name="ssm",
    )(u_tok, d_skip, perm, perm.T, wpair, m, vpair, a_chunk)


def _attn_tables(rpb, rows):
    kh = min(WIN_H_MAX, rows)
    H = rpb.shape[0]
    w = jnp.arange(GRID_W)
    col_start = jnp.clip(w - WIN_W // 2, 0, GRID_W - WIN_W)
    col_mask = (w[None, :] >= col_start[:, None]) & (w[None, :] < col_start[:, None] + WIN_W)
    dc = jnp.clip(w[None, :] - w[:, None] + (WIN_W - 1), 0, 2 * WIN_W - 2)
    col_sel = (dc[:, :, None] == jnp.arange(2 * WIN_W - 1)[None, None, :]).astype(F32)
    tile = jnp.einsum('hdc,qkc->hdqk', rpb, col_sel, precision=lax.Precision.HIGHEST)
    tile = jnp.where(col_mask[None, None], tile * LOG2E, NEG)
    lo_pad, hi_pad = ATT_ROWS, ATT_WIN - kh + 1
    tile = jnp.pad(tile, ((0, 0), (lo_pad, hi_pad), (0, 0), (0, 0)))
    n_dr = tile.shape[1] - 1
    tiles = jnp.concatenate([tile[:, :-1], tile[:, 1:]], axis=-1).reshape(H // 2, 2, n_dr, GRID_W, 2 * GRID_W)
    i = jnp.arange(ATT_ROWS)
    last_union = rows - ATT_WIN
    last_rel = (rows - kh) - last_union
    rel = jnp.stack([jnp.zeros_like(i), i, jnp.full_like(i, last_rel)])
    j = jnp.arange(ATT_WIN)
    valid = (j[None, None, :] >= rel[:, :, None]) & (j[None, None, :] < rel[:, :, None] + kh)
    masks = jnp.where(valid, 0.0, NEG).astype(F32)
    masks = jnp.repeat(masks.reshape(3 * ATT_ROWS * ATT_WIN // 2, 2), GRID_W, axis=-1)
    return tiles, masks.reshape(-1, 1, 2 * GRID_W)


def _attn_kernel(q_ref, k_ref, v_ref, t_ref, m_ref, o_ref, *, rows, kh):
    n_blocks = rows // ATT_ROWS
    nq = ATT_ROWS * GRID_W
    nk = ATT_WIN * GRID_W
    n_pairs = ATT_WIN // 2
    pw = 2 * GRID_W
    lane = lax.broadcasted_iota(I32, (1, 2 * HEAD_DIM), 1)
    head_lanes = [lane < HEAD_DIM, lane >= HEAD_DIM]
    rel = [[0] * ATT_ROWS, list(range(ATT_ROWS)), [(rows - kh) - (rows - ATT_WIN)] * ATT_ROWS]

    def block(rb, pattern):
        r0 = rb * ATT_ROWS
        union = [0, r0 - kh // 2, rows - ATT_WIN][pattern]
        dr0 = [0, -(kh // 2), rows - ATT_WIN - (n_blocks - 1) * ATT_ROWS][pattern] + (WIN_H_MAX - 1) + ATT_ROWS
        aligned = lambda x, m: x if isinstance(x, int) else pl.multiple_of(x, m)
        q_rows = pl.ds(aligned(r0 * GRID_W, nq), nq)
        k_rows = pl.ds(aligned(union * GRID_W, GRID_W), nk)
        q2, k2, v2 = q_ref[0, q_rows, :], k_ref[0, k_rows, :], v_ref[0, k_rows, :]
        out = jnp.zeros((nq, 2 * HEAD_DIM), F32)
        for hl in range(2):
            qm = jnp.where(head_lanes[hl], q2, 0.0).astype(BF16)
            s = _nt_dot(qm, k2)
            probs, denoms = [], []
            for i in range(ATT_ROWS):
                first, last = rel[pattern][i], rel[pattern][i] + kh - 1
                pairs = range(first // 2, last // 2 + 1)
                bias = []
                for jp in pairs:
                    tile = t_ref[0, hl, dr0 + 2 * jp - i]
                    if 2 * jp < first or 2 * jp + 1 > last:
                        tile = tile + m_ref[(pattern * ATT_ROWS + i) * n_pairs + jp]
                    bias.append(tile)
                si = s[i * GRID_W:(i + 1) * GRID_W, pairs[0] * pw:(pairs[-1] + 1) * pw] + jnp.concatenate(bias, axis=-1)
                pi = jnp.exp2(si - jnp.max(si, axis=-1, keepdims=True))
                denoms.append(jnp.sum(pi, axis=-1, keepdims=True))
                pieces = [pi.astype(BF16)]
                if pairs[0] > 0:
                    pieces.insert(0, jnp.zeros((GRID_W, pairs[0] * pw), BF16))
                if pairs[-1] + 1 < n_pairs:
                    pieces.append(jnp.zeros((GRID_W, (n_pairs - 1 - pairs[-1]) * pw), BF16))
                probs.append(jnp.concatenate(pieces, axis=-1))
            vm = jnp.where(head_lanes[hl], v2, 0.0).astype(BF16)
            pv = jnp.dot(jnp.concatenate(probs, axis=0), vm, preferred_element_type=F32)
            out = out + pv / jnp.concatenate(denoms, axis=0)
        o_ref[0, q_rows, :] = out

    block(0, 0)
    for rb in range(1, n_blocks - 1):
        block(rb, 1)
    block(n_blocks - 1, 2)


def _attn(qkv, tiles, masks):
    bsz, L, n3 = qkv.shape
    d = n3 // 3
    rows = L // GRID_W
    kh = min(WIN_H_MAX, rows)
    assert rows % ATT_ROWS == 0 and rows // ATT_ROWS >= 3 and ATT_ROWS == kh // 2
    assert ATT_WIN % 2 == 0 and kh + ATT_ROWS - 1 <= ATT_WIN <= rows
    n_hp = d // (2 * HEAD_DIM)
    blk = pl.BlockSpec((1, L, 2 * HEAD_DIM), lambda b, hp: (b, 0, hp))
    return pl.pallas_call(
        functools.partial(_attn_kernel, rows=rows, kh=kh),
        grid=(bsz, n_hp),
        in_specs=[blk,
                  pl.BlockSpec((1, L, 2 * HEAD_DIM), lambda b, hp: (b, 0, n_hp + hp)),
                  pl.BlockSpec((1, L, 2 * HEAD_DIM), lambda b, hp: (b, 0, 2 * n_hp + hp)),
                  pl.BlockSpec((1,) + tiles.shape[1:], lambda b, hp: (hp, 0, 0, 0, 0)),
                  pl.BlockSpec(masks.shape, lambda b, hp: (0, 0, 0))],
        out_specs=blk,
        out_shape=jax.ShapeDtypeStruct((bsz, L, d), F32),
        compiler_params=_params(("arbitrary", "arbitrary")),
        name="attn",
    )(qkv, qkv, qkv, tiles, masks)


def _mix_kernel(ys_ref, wglu_ref, bglu_ref, gs_ref, ya_ref, ga_ref, wout_ref,
                x_ref, gt_ref, gf_ref, sh_ref, sc_ref, wr_ref,
                x1_ref, h2_ref, lg_ref):
    d_ssm = ys_ref.shape[3]
    y = ys_ref[:, 0].reshape(-1, d_ssm)
    z = jax.nn.gelu(y)
    gate = _sigmoid(jnp.dot(z.astype(BF16), wglu_ref[...], preferred_element_type=F32) + bglu_ref[...])
    a = _rms(z * gate, gs_ref[...]).astype(BF16)
    t = _rms(ya_ref[0], ga_ref[...]).astype(BF16)
    mixed = (jnp.dot(a, wout_ref[:d_ssm, :], preferred_element_type=F32)
             + jnp.dot(t, wout_ref[d_ssm:, :], preferred_element_type=F32))
    x1 = x_ref[0] + gt_ref[0] * mixed
    x1_ref[0] = x1
    h2 = _rms(x1, gf_ref[...]) * (1.0 + sc_ref[0]) + sh_ref[0]
    h2_hi = h2.astype(BF16)
    h2_ref[0] = h2_hi
    h2_lo = (h2 - h2_hi.astype(F32)).astype(BF16)
    wr = wr_ref[...]
    wr_hi = wr.astype(BF16)
    wr_lo = (wr - wr_hi.astype(F32)).astype(BF16)
    n_e = wr.shape[1]
    tm = h2.shape[0]
    r = jnp.dot(jnp.concatenate([h2_hi, h2_lo], axis=0), jnp.concatenate([wr_hi, wr_lo], axis=1),
                preferred_element_type=F32)
    lg_ref[0] = r[:tm, :n_e] + r[:tm, n_e:] + r[tm:, :n_e]


def _mix(ys, wglu_bf, b_glu, g_ssm, ya, g_attn, wout_bf, x, gt1, g_ffn, sh2, sc2, w_router):
    bsz, L, D = x.shape
    d_ssm = ys.shape[3]
    E = w_router.shape[1]
    tm = 256
    T = SSM_T
    tile = lambda n: pl.BlockSpec((1, tm, n), lambda b, i: (b, i, 0))
    per_b = lambda n: pl.BlockSpec((1, 1, n), lambda b, i: (b, 0, 0))
    full = lambda r, n: pl.BlockSpec((r, n), lambda b, i: (0, 0))
    return pl.pallas_call(
        _mix_kernel,
        grid=(bsz, L // tm),
        in_specs=[pl.BlockSpec((tm // T, 1, T, d_ssm), lambda b, i: (i, b, 0, 0)),
                  full(d_ssm, d_ssm), full(1, d_ssm), full(1, d_ssm),
                  tile(d_ssm), full(1, d_ssm), full(D, D),
                  tile(D), per_b(D), full(1, D), per_b(D), per_b(D), full(D, E)],
        out_specs=[tile(D), tile(D), tile(E)],
        out_shape=[jax.ShapeDtypeStruct((bsz, L, D), F32),
                   jax.ShapeDtypeStruct((bsz, L, D), BF16),
                   jax.ShapeDtypeStruct((bsz, L, E), F32)],
        compiler_params=_params(("arbitrary", "arbitrary")),
        name="mix",
    )(ys, wglu_bf, b_glu, g_ssm, ya, g_attn, wout_bf, x, gt1, g_ffn, sh2, sc2, w_router)


TOPK_BISECTIONS = 160


def _topk_kernel(lg_ref, aff_ref, pos_ref, *, cap):
    lg = lg_ref[...]
    bsz, E, L = lg.shape
    e = jnp.exp(lg - jnp.max(lg, axis=1, keepdims=True))
    aff3 = e / jnp.sum(e, axis=1, keepdims=True)
    aff_ref[...] = aff3
    aff = aff3.reshape(bsz * E, L)

    def count(mask):
        return jnp.sum(jnp.where(mask, 1.0, 0.0), axis=-1, keepdims=True)

    def halve(_, carry):
        lo, hi = carry
        mid = lo + (hi - lo) * 0.5
        ok = count(aff >= mid) >= cap
        return jnp.where(ok, mid, lo), jnp.where(ok, hi, mid)

    lo0 = jnp.zeros((bsz * E, 1), F32)
    thr, _ = lax.fori_loop(0, TOPK_BISECTIONS, halve, (lo0, lo0 + 2.0))
    gt = aff > thr
    eq = aff == thr
    need = cap - count(gt)
    blk = LANES
    tri = jnp.where(lax.broadcasted_iota(I32, (blk, blk), 0) < lax.broadcasted_iota(I32, (blk, blk), 1),
                    1.0, 0.0).astype(BF16)

    def prefix_count(mask):
        ones = jnp.where(mask, 1.0, 0.0)
        run = jnp.zeros((bsz * E, 1), F32)
        outs = []
        for j in range(L // blk):
            piece = ones[:, j * blk:(j + 1) * blk]
            outs.append(jnp.dot(piece.astype(BF16), tri, preferred_element_type=F32) + run)
            run = run + jnp.sum(piece, axis=-1, keepdims=True)
        return jnp.concatenate(outs, axis=-1)

    sel = gt | (eq & (prefix_count(eq) < need))
    pos = prefix_count(sel)
    pos_ref[...] = jnp.where(sel, pos.astype(I32), -1).reshape(bsz, E, L)


def _topk(lg_t, cap):
    bsz, E, L = lg_t.shape
    spec = pl.BlockSpec((bsz, E, L), lambda i: (0, 0, 0))
    return pl.pallas_call(
        functools.partial(_topk_kernel, cap=cap),
        grid=(1,),
        in_specs=[spec],
        out_specs=[spec, spec],
        out_shape=[jax.ShapeDtypeStruct((bsz, E, L), F32), jax.ShapeDtypeStruct((bsz, E, L), I32)],
        compiler_params=_params(("arbitrary",)),
        name="topk",
    )(lg_t)


def _gather_kernel(pos_ref, aff_ref, h_ref, xe_ref, as_ref, *, cap):
    L = h_ref.shape[1]
    n_e = pos_ref.shape[1]
    slot = lax.broadcasted_iota(I32, (cap, L), 0)
    hits = [pos_ref[0, e] == slot for e in range(n_e)]
    onehot = jnp.concatenate([jnp.where(h, 1.0, 0.0).astype(BF16) for h in hits], axis=0)
    xe = jnp.dot(onehot, h_ref[0], preferred_element_type=F32).astype(BF16)
    for e in range(n_e):
        xe_ref[e] = xe[e * cap:(e + 1) * cap]
        as_ref[e] = jnp.sum(jnp.where(hits[e], aff_ref[0, e], 0.0), axis=-1, keepdims=True)


def _gather(pos, aff, h2, cap):
    bsz, E, L = pos.shape
    D = h2.shape[2]
    pos4 = pos.reshape(bsz, E, 1, L)
    aff4 = aff.reshape(bsz, E, 1, L)
    ge = 4
    return pl.pallas_call(
        functools.partial(_gather_kernel, cap=cap),
        grid=(bsz, E // ge),
        in_specs=[pl.BlockSpec((1, ge, 1, L), lambda b, e: (b, e, 0, 0)),
                  pl.BlockSpec((1, ge, 1, L), lambda b, e: (b, e, 0, 0)),
                  pl.BlockSpec((1, L, D), lambda b, e: (b, 0, 0))],
        out_specs=[pl.BlockSpec((ge, cap, D), lambda b, e: (e, b, 0)),
                   pl.BlockSpec((ge, cap, 1), lambda b, e: (e, b, 0))],
        out_shape=[jax.ShapeDtypeStruct((E, bsz * cap, D), BF16),
                   jax.ShapeDtypeStruct((E, bsz * cap, 1), F32)],
        compiler_params=_params(("arbitrary", "arbitrary")),
        name="gather",
    )(pos4, aff4, h2)


def _ffn_kernel(xe_ref, wg_ref, wu_ref, wd_ref, as_ref, ye_ref, acc_ref):
    f = pl.program_id(1)

    @pl.when(f == 0)
    def _():
        acc_ref[...] = jnp.zeros_like(acc_ref)

    x = xe_ref[0]
    g = jnp.dot(x, wg_ref[0].astype(BF16), preferred_element_type=F32)
    up = jnp.dot(x, wu_ref[0].astype(BF16), preferred_element_type=F32)
    hid = (g * _sigmoid(g) * up).astype(BF16)
    acc_ref[...] += jnp.dot(hid, wd_ref[0].astype(BF16), preferred_element_type=F32)

    @pl.when(f == pl.num_programs(1) - 1)
    def _():
        bsz, _, cap, _ = ye_ref.shape
        y = (acc_ref[...] * as_ref[0]).astype(BF16)
        for b in range(bsz):
            ye_ref[b, 0] = y[b * cap:(b + 1) * cap]


def _ffn(xe, w_gate, w_up, w_down, aff_slot, bsz):
    E, R, D = xe.shape
    F = w_gate.shape[2]
    cap = R // bsz
    tf = 256
    return pl.pallas_call(
        _ffn_kernel,
        grid=(E, F // tf),
        in_specs=[pl.BlockSpec((1, R, D), lambda e, f: (e, 0, 0)),
                  pl.BlockSpec((1, D, tf), lambda e, f: (e, 0, f)),
                  pl.BlockSpec((1, D, tf), lambda e, f: (e, 0, f)),
                  pl.BlockSpec((1, tf, D), lambda e, f: (e, f, 0)),
                  pl.BlockSpec((1, R, 1), lambda e, f: (e, 0, 0))],
        out_specs=pl.BlockSpec((bsz, 1, cap, D), lambda e, f: (0, e, 0, 0)),
        out_shape=jax.ShapeDtypeStruct((bsz, E, cap, D), BF16),
        scratch_shapes=[pltpu.VMEM((R, D), F32)],
        compiler_params=_params(("arbitrary", "arbitrary")),
        name="ffn",
    )(xe, w_gate, w_up, w_down, aff_slot)


def _combine_kernel(pos_ref, ye_ref, x1_ref, gt_ref, g_ref, o_ref, *, final_norm):
    pos_t = pos_ref[0]
    n_e, cap, d = ye_ref.shape[1:]
    slot = lax.broadcasted_iota(I32, (pos_t.shape[0], cap), 1)
    onehot = jnp.concatenate([jnp.where(pos_t[:, e:e + 1] == slot, 1.0, 0.0).astype(BF16) for e in range(n_e)],
                             axis=-1)
    moe = jnp.dot(onehot, ye_ref[0].reshape(n_e * cap, d), preferred_element_type=F32)
    o = x1_ref[0] + gt_ref[0] * moe
    o_ref[0] = _rms(o, g_ref[...]) if final_norm else o


def _combine(pos_t, ye, x1, gt2, g_final, final_norm):
    bsz, L, E = pos_t.shape
    D = x1.shape[2]
    cap = ye.shape[2]
    tl = 512
    return pl.pallas_call(
        functools.partial(_combine_kernel, final_norm=final_norm),
        grid=(bsz, L // tl),
        in_specs=[pl.BlockSpec((1, tl, E), lambda b, i: (b, i, 0)),
                  pl.BlockSpec((1, E, cap, D), lambda b, i: (b, 0, 0, 0), pipeline_mode=pl.Buffered(1)),
                  pl.BlockSpec((1, tl, D), lambda b, i: (b, i, 0)),
                  pl.BlockSpec((1, 1, D), lambda b, i: (b, 0, 0)),
                  pl.BlockSpec((1, D), lambda b, i: (0, 0))],
        out_specs=pl.BlockSpec((1, tl, D), lambda b, i: (b, i, 0)),
        out_shape=jax.ShapeDtypeStruct((bsz, L, D), F32),
        compiler_params=_params(("arbitrary", "arbitrary")),
        name="combine",
    )(pos_t, ye, x1, gt2, g_final)


def kernel(x, c, w_ada, b_ada, g_mix, w_in, ssm_a_re, ssm_a_im, ssm_log_dt, ssm_b_re, ssm_b_im,
           ssm_c_re, ssm_c_im, ssm_d, w_glu, b_glu, rpb, g_ssm_out, g_attn_out, w_out, g_ffn,
           w_router, w_gate, w_up, w_down, g_final):
    bsz, L, D = x.shape
    depth = w_ada.shape[0]
    d_ssm = ssm_d.shape[1]
    E = w_router.shape[2]
    cap = CAPACITY_FACTOR * L // E
    rows = L // GRID_W
    c8 = jnp.zeros((8, D), F32).at[:bsz].set(c)
    row = lambda v: v.reshape(1, -1)
    for layer in range(depth):
        mod = _ada(c8, w_ada[layer], row(b_ada[layer]))[:bsz]
        sh1, sc1, gt1, sh2, sc2, gt2 = [m.reshape(bsz, 1, D) for m in jnp.split(mod, 6, axis=-1)]

        u, qkv = _inproj(x, sh1, sc1, row(g_mix[layer]), _cast_bf16(w_in[layer]), d_ssm)
        ops = _ssm_operators(ssm_a_re[layer], ssm_a_im[layer], ssm_log_dt[layer], ssm_b_re[layer],
                             ssm_b_im[layer], ssm_c_re[layer], ssm_c_im[layer])
        y_ssm = _ssm(u.reshape(-1, d_ssm), row(ssm_d[layer]), ops, bsz).reshape(u.shape)
        y_attn = _attn(qkv, *_attn_tables(rpb[layer], rows))
        x1, h2, logits = _mix(y_ssm, _cast_bf16(w_glu[layer]), row(b_glu[layer]),
                              row(g_ssm_out[layer]), y_attn, row(g_attn_out[layer]), _cast_bf16(w_out[layer]),
                              x, gt1, row(g_ffn[layer]), sh2, sc2, w_router[layer])

        aff, pos = _topk(jnp.swapaxes(logits, 1, 2), cap)
        xe, aff_slot = _gather(pos, aff, h2, cap)
        ye = _ffn(xe, w_gate[layer], w_up[layer], w_down[layer], aff_slot, bsz)
        x = _combine(jnp.swapaxes(pos, 1, 2), ye, x1, gt2, row(g_final), layer == depth - 1)
    return x
```

```python
import functools
import math

import jax
import jax.numpy as jnp
from jax import lax
from jax.experimental import pallas as pl
from jax.experimental.pallas import tpu as pltpu

F32 = jnp.float32
BF16 = jnp.bfloat16
I32 = jnp.int32

EPS = 1e-6
GRID_W = 64
SSM_GROUP = 16
SSM_STATE = 64
N_HEADS = 16
HEAD_DIM = 64
WIN_H_MAX = 8
WIN_W = 16
N_EXPERTS = 16
CAPACITY_FACTOR = 2
SSM_T = 16
LANES = 128
NEG = -1e30
LOG2E = math.log2(math.e)

ATT_ROWS = 4
ATT_WIN = 12
VMEM_LIMIT = 56 * 1024 * 1024


def _params(sem, vmem=VMEM_LIMIT):
    return pltpu.CompilerParams(dimension_semantics=sem, vmem_limit_bytes=vmem)


def _sigmoid(x):
    return 1.0 / (1.0 + jnp.exp(-x))


def _rms(x, g):
    return x * lax.rsqrt(jnp.mean(x * x, axis=-1, keepdims=True) + EPS) * g


def _ada_kernel(c_ref, w_ref, b_ref, o_ref):
    c = c_ref[...]
    ca = (c * _sigmoid(c)).astype(BF16)
    o_ref[...] = jnp.dot(ca, w_ref[...].astype(BF16), preferred_element_type=F32) + b_ref[...]


def _ada(c8, w, b):
    k, n = w.shape
    tn = 1024
    return pl.pallas_call(
        _ada_kernel,
        grid=(n // tn,),
        in_specs=[pl.BlockSpec((8, k), lambda j: (0, 0)),
                  pl.BlockSpec((k, tn), lambda j: (0, j)),
                  pl.BlockSpec((1, tn), lambda j: (0, j))],
        out_specs=pl.BlockSpec((8, tn), lambda j: (0, j)),
        out_shape=jax.ShapeDtypeStruct((8, n), F32),
        compiler_params=_params(("arbitrary",)),
        name="ada",
    )(c8, w, b)


def _cast_kernel(w_ref, o_ref):
    o_ref[...] = w_ref[...].astype(BF16)


def _cast_bf16(w):
    k, n = w.shape
    tk = 256
    return pl.pallas_call(
        _cast_kernel,
        grid=(k // tk,),
        in_specs=[pl.BlockSpec((tk, n), lambda i: (i, 0))],
        out_specs=pl.BlockSpec((tk, n), lambda i: (i, 0)),
        out_shape=jax.ShapeDtypeStruct((k, n), BF16),
        compiler_params=_params(("arbitrary",)),
        name="cast",
    )(w)


def _inproj_kernel(x_ref, sh_ref, sc_ref, g_ref, w_ref, u_ref, qkv_ref, *, d_ssm):
    h = (_rms(x_ref[0], g_ref[...]) * (1.0 + sc_ref[0]) + sh_ref[0]).astype(BF16)
    n_total = w_ref.shape[1]
    for n in range(n_total // d_ssm):
        r = jnp.dot(h, w_ref[:, n * d_ssm:(n + 1) * d_ssm], preferred_element_type=F32)
        if n == 0:
            for cc in range(u_ref.shape[0]):
                u_ref[cc, 0] = r[cc * SSM_T:(cc + 1) * SSM_T]
        else:
            if n == 1:
                r = r * (HEAD_DIM ** -0.5 * LOG2E)
            qkv_ref[0, :, (n - 1) * d_ssm:n * d_ssm] = r.astype(BF16)


def _inproj(x, sh, sc, g, w_bf, d_ssm):
    bsz, L, D = x.shape
    n = w_bf.shape[1]
    tm = 256
    T = SSM_T
    return pl.pallas_call(
        functools.partial(_inproj_kernel, d_ssm=d_ssm),
        grid=(bsz, L // tm),
        in_specs=[pl.BlockSpec((1, tm, D), lambda b, i: (b, i, 0)),
                  pl.BlockSpec((1, 1, D), lambda b, i: (b, 0, 0)),
                  pl.BlockSpec((1, 1, D), lambda b, i: (b, 0, 0)),
                  pl.BlockSpec((1, D), lambda b, i: (0, 0)),
                  pl.BlockSpec((D, n), lambda b, i: (0, 0))],
        out_specs=[pl.BlockSpec((tm // T, 1, T, d_ssm), lambda b, i: (i, b, 0, 0)),
                   pl.BlockSpec((1, tm, n - d_ssm), lambda b, i: (b, i, 0))],
        out_shape=[jax.ShapeDtypeStruct((L // T, bsz, T, d_ssm), F32),
                   jax.ShapeDtypeStruct((bsz, L, n - d_ssm), BF16)],
        compiler_params=_params(("arbitrary", "arbitrary")),
        name="inproj",
    )(x, sh, sc, g, w_bf)


def _ssm_operators(a_re, a_im, log_dt, b_re, b_im, c_re, c_im):
    T = SSM_T
    G, P = a_re.shape[1], a_re.shape[2]
    H = b_re.shape[3]
    dt = jnp.exp(log_dt)[..., None]
    mag = jnp.exp(a_re * dt)
    lbr = mag * jnp.cos(a_im * dt)
    lbi = mag * jnp.sin(a_im * dt)
    den = a_re * a_re + a_im * a_im
    nr = lbr - 1.0
    fr = ((nr * a_re + lbi * a_im) / den)[:, :, None, :]
    fi = ((lbi * a_re - nr * a_im) / den)[:, :, None, :]
    b_re_t, b_im_t = jnp.swapaxes(b_re, 2, 3), jnp.swapaxes(b_im, 2, 3)
    bbr = fr * b_re_t - fi * b_im_t
    bbi = fr * b_im_t + fi * b_re_t
    prs, pis = [jnp.ones_like(lbr)], [jnp.zeros_like(lbi)]
    for _ in range(T):
        pr, pi = prs[-1], pis[-1]
        prs.append(pr * lbr - pi * lbi)
        pis.append(pr * lbi + pi * lbr)
    pr = jnp.stack(prs, axis=2)
    pi = jnp.stack(pis, axis=2)
    assert H == T
    rev = lambda x: x[:, ::-1]
    sections = [bbr[0], bbi[0], bbr[1], bbi[1], c_re[0], c_im[0], c_re[1], c_im[1],
                rev(pr[0, :, :T]), rev(pi[0, :, :T]), pr[1, :, :T], pi[1, :, :T],
                pr[0, :, 1:], pi[0, :, 1:], rev(pr[1, :, 1:]), rev(pi[1, :, 1:]),
                pr[0, :, :T], pi[0, :, :T]]
    m, wpair, vtpair = _ssm_prep(jnp.concatenate(sections, axis=1))
    a4 = jnp.stack([pr[0, :, T], pi[0, :, T], pr[1, :, T], pi[1, :, T]])
    a_chunk = jnp.transpose(a4.reshape(4, G // 2, 2, P), (1, 0, 2, 3)).reshape(1, G // 2 * 8 * P)
    return m, wpair, vtpair, a_chunk


def _nt_dot(a, b):
    return lax.dot_general(a, b, (((1,), (1,)), ((), ())), preferred_element_type=F32)


def _ssm_prep_kernel(pk_ref, m_ref, w_ref, vt_ref, mats_ref, rows_ref, kmats_ref, krows_ref):
    T = SSM_T
    H = T
    kw = T * H
    half = kmats_ref.shape[2] // 2
    lane = lax.broadcasted_iota(I32, (1, 2 * half), 1)
    cat = lambda xs: jnp.concatenate(xs, axis=-1)

    def group(g, carry):
        (bbr_f, bbi_f, bbr_b, bbi_b, cre_f, cim_f, cre_b, cim_b, qrf, qif, qrb, qib,
         rrf, rif, rrb, rib, krf, kif) = [pk_ref[g, i * T:(i + 1) * T, :] for i in range(18)]
        zero = jnp.zeros_like(bbr_f)
        even = g % 2 == 0

        def slots(v4):
            ev = cat([v4[0], zero, v4[1], zero, v4[2], zero, v4[3], zero])
            od = cat([zero, v4[0], zero, v4[1], zero, v4[2], zero, v4[3]])
            return jnp.where(even, ev, od)

        mats_ref[0] = slots([bbr_f, bbi_f, bbr_b, bbi_b])
        mats_ref[1] = slots([-bbi_f, bbr_f, -bbi_b, bbr_b])
        mats_ref[2] = slots([cre_f, -cim_f, cre_b, -cim_b])
        mats_ref[3] = slots([-cim_f, -cre_f, -cim_b, -cre_b])
        rows_ref[0] = cat([qrf] * 4 + [qrb] * 4)
        rows_ref[1] = cat([qif] * 4 + [qib] * 4)
        rows_ref[2] = cat([rrf] * 4 + [rrb] * 4)
        rows_ref[3] = cat([rif] * 4 + [rib] * 4)
        kmats_ref[0] = cat([bbr_f, -bbi_f, bbr_b, -bbi_b])
        kmats_ref[1] = cat([cre_f, cre_f, cre_b, cre_b])
        kmats_ref[2] = cat([-cim_f, cim_f, -cim_b, cim_b])
        krows_ref[0] = cat([krf, kif, qrb, qib])
        krows_ref[1] = cat([kif, krf, qib, qrb])

        pair, row0 = g // 2, (g % 2) * kw
        aw, bw, av, bv = [mats_ref[i] for i in range(4)]
        for s in range(T):
            r = pl.ds(pl.multiple_of(row0 + s * H, H), H)
            w_ref[pair, r, :] = (aw * rows_ref[0, pl.ds(s, 1), :] + bw * rows_ref[1, pl.ds(s, 1), :]).astype(BF16)
            vt_ref[pair, r, :] = (av * rows_ref[2, pl.ds(s, 1), :] + bv * rows_ref[3, pl.ds(s, 1), :]).astype(BF16)
        lhs, a4, b4 = [kmats_ref[i] for i in range(3)]
        blocks = []
        for i in range(2 * T):
            k = abs(i - (T - 1))
            if i == 2 * T - 1:
                blocks.append(jnp.zeros_like(a4))
                continue
            cp = a4 * krows_ref[0, pl.ds(k, 1), :] + b4 * krows_ref[1, pl.ds(k, 1), :]
            if i < T - 1:
                cp = jnp.where(lane >= half, cp, 0.0)
            elif i > T - 1:
                cp = jnp.where(lane < half, cp, 0.0)
            blocks.append(cp)
        rhs = jnp.concatenate(blocks, axis=0)
        lhs_hi = lhs.astype(BF16)
        lhs_lo = (lhs - lhs_hi.astype(F32)).astype(BF16)
        rhs_hi = rhs.astype(BF16)
        rhs_lo = (rhs - rhs_hi.astype(F32)).astype(BF16)
        z = _nt_dot(lhs_hi, rhs_hi) + _nt_dot(lhs_lo, rhs_hi) + _nt_dot(lhs_hi, rhs_lo)
        for s in range(T):
            off = (T - 1 - s) * H
            m_ref[g, s * H:(s + 1) * H, :] = z[:, off:off + kw].astype(BF16)
        return carry

    lax.fori_loop(0, pk_ref.shape[0], group, 0)


def _ssm_prep(packed):
    G, n_rows, P = packed.shape
    T = H = SSM_T
    gs = LANES // SSM_GROUP
    return pl.pallas_call(
        _ssm_prep_kernel,
        grid=(G // gs,),
        in_specs=[pl.BlockSpec((gs, n_rows, P), lambda j: (j, 0, 0))],
        out_specs=[pl.BlockSpec((gs, T * H, T * H), lambda j: (j, 0, 0)),
                   pl.BlockSpec((gs // 2, 2 * T * H, 8 * P), lambda j: (j, 0, 0)),
                   pl.BlockSpec((gs // 2, 2 * T * H, 8 * P), lambda j: (j, 0, 0))],
        out_shape=[jax.ShapeDtypeStruct((G, T * H, T * H), BF16),
                   jax.ShapeDtypeStruct((G // 2, 2 * T * H, 8 * P), BF16),
                   jax.ShapeDtypeStruct((G // 2, 2 * T * H, 8 * P), BF16)],
        scratch_shapes=[pltpu.VMEM((4, H, 8 * P), F32), pltpu.VMEM((4, T, 8 * P), F32),
                        pltpu.VMEM((3, H, 4 * P), F32), pltpu.VMEM((2, T, 4 * P), F32)],
        compiler_params=_params(("arbitrary",)),
        name="ssm_prep",
    )(packed)


def _slab_permutation():
    idx = jnp.arange(8 * LANES)
    t, g, h = idx // LANES, (idx % LANES) // SSM_GROUP, idx % SSM_GROUP
    dst = g * LANES + t * SSM_GROUP + h
    return (dst[:, None] == jnp.arange(8 * LANES)[None, :]).astype(BF16)


def _chunk_scan(s_ref, a_ref, *, bsz, n_pairs):
    n_it = s_ref.shape[0] // (2 * bsz)
    cols = [[pl.ds((4 * j + q) * LANES, LANES) for q in range(4)] for j in range(n_pairs)]
    decay = [[a_ref[:, c] for c in cols[j]] for j in range(n_pairs)]

    def step(i, carry):
        rf = pl.ds(pl.multiple_of(i * 2 * bsz, 2 * bsz), 2 * bsz)
        rb = pl.ds(pl.multiple_of((n_it - 1 - i) * 2 * bsz, 2 * bsz), 2 * bsz)
        out = []
        for j in range(n_pairs):
            ar_f, ai_f, ar_b, ai_b = decay[j]
            xr, xi, yr, yi = carry[4 * j:4 * j + 4]
            sr, si = s_ref[rf, cols[j][0]], s_ref[rf, cols[j][1]]
            xr1 = ar_f * xr - ai_f * xi + sr[:bsz]
            xi1 = ar_f * xi + ai_f * xr + si[:bsz]
            s_ref[rf, cols[j][0]] = jnp.concatenate([xr, xr1], axis=0)
            s_ref[rf, cols[j][1]] = jnp.concatenate([xi, xi1], axis=0)
            xr2 = ar_f * xr1 - ai_f * xi1 + sr[bsz:]
            xi2 = ar_f * xi1 + ai_f * xr1 + si[bsz:]
            tr, ti = s_ref[rb, cols[j][2]], s_ref[rb, cols[j][3]]
            yr1 = ar_b * yr - ai_b * yi + tr[bsz:]
            yi1 = ar_b * yi + ai_b * yr + ti[bsz:]
            s_ref[rb, cols[j][2]] = jnp.concatenate([yr1, yr], axis=0)
            s_ref[rb, cols[j][3]] = jnp.concatenate([yi1, yi], axis=0)
            yr2 = ar_b * yr1 - ai_b * yi1 + tr[:bsz]
            yi2 = ar_b * yi1 + ai_b * yr1 + ti[:bsz]
            out += [xr2, xi2, yr2, yi2]
        return tuple(out)

    z = jnp.zeros((bsz, LANES), F32)
    lax.fori_loop(0, n_it, step, (z,) * (4 * n_pairs))


def _ssm_kernel(u_ref, d_ref, perm_ref, iperm_ref, w_ref, m_ref, v_ref, a_ref, y_ref,
                ug_ref, s_ref, yg_ref, *, bsz):
    T = SSM_T
    R = u_ref.shape[0] // T
    n_groups = LANES // SSM_GROUP
    kw = T * SSM_GROUP
    for tq in range(T // 8):
        cat = jnp.concatenate([u_ref[pl.ds(tq * 8 + t, R, stride=T), :].astype(BF16) for t in range(8)], axis=-1)
        grouped = jnp.dot(cat, perm_ref[...], preferred_element_type=F32).astype(BF16)
        for g in range(n_groups):
            ug_ref[g, :, tq * LANES:(tq + 1) * LANES] = grouped[:, g * LANES:(g + 1) * LANES]
    for pr in range(n_groups // 2):
        u2 = jnp.concatenate([ug_ref[2 * pr], ug_ref[2 * pr + 1]], axis=-1)
        s_ref[:, pr * 4 * LANES:(pr + 1) * 4 * LANES] = jnp.dot(u2, w_ref[pr], preferred_element_type=F32)
    _chunk_scan(s_ref, a_ref, bsz=bsz, n_pairs=n_groups // 2)
    for pr in range(n_groups // 2):
        inter = _nt_dot(s_ref[:, pr * 4 * LANES:(pr + 1) * 4 * LANES].astype(BF16), v_ref[pr])
        for gl in range(2):
            g = 2 * pr + gl
            yg_ref[g] = (jnp.dot(ug_ref[g], m_ref[g], preferred_element_type=F32)
                         + inter[:, gl * kw:(gl + 1) * kw])
    for tq in range(T // 8):
        ycat = jnp.concatenate([yg_ref[g, :, tq * LANES:(tq + 1) * LANES] for g in range(n_groups)], axis=-1)
        y_hi = ycat.astype(BF16)
        y_lo = (ycat - y_hi.astype(F32)).astype(BF16)
        back = (jnp.dot(y_hi, iperm_ref[...], preferred_element_type=F32)
                + jnp.dot(y_lo, iperm_ref[...], preferred_element_type=F32))
        for t in range(8):
            rows = pl.ds(tq * 8 + t, R, stride=T)
            y_ref[rows, :] = back[:, t * LANES:(t + 1) * LANES] + d_ref[...] * u_ref[rows, :]


def _ssm(u_tok, d_skip, ops, bsz):
    m, wpair, vpair, a_chunk = ops
    N, d_ssm = u_tok.shape
    T = SSM_T
    R = N // T
    n_groups = LANES // SSM_GROUP
    perm = _slab_permutation()
    const = lambda shape: pl.BlockSpec(shape, lambda j: (0,) * len(shape))
    return pl.pallas_call(
        functools.partial(_ssm_kernel, bsz=bsz),
        grid=(d_ssm // LANES,),
        in_specs=[pl.BlockSpec((N, LANES), lambda j: (0, j)),
                  pl.BlockSpec((1, LANES), lambda j: (0, j)),
                  const(perm.shape), const(perm.shape),
                  pl.BlockSpec((n_groups // 2,) + wpair.shape[1:], lambda j: (j, 0, 0)),
                  pl.BlockSpec((n_groups,) + m.shape[1:], lambda j: (j, 0, 0)),
                  pl.BlockSpec((n_groups // 2,) + vpair.shape[1:], lambda j: (j, 0, 0)),
                  pl.BlockSpec((1, n_groups // 2 * 4 * LANES), lambda j: (0, j))],
        out_specs=pl.BlockSpec((N, LANES), lambda j: (0, j)),
        out_shape=jax.ShapeDtypeStruct((N, d_ssm), F32),
        scratch_shapes=[pltpu.VMEM((n_groups, R, T * SSM_GROUP), BF16),
                        pltpu.VMEM((R, n_groups // 2 * 4 * LANES), F32),
                        pltpu.VMEM((n_groups, R, T * SSM_GROUP), F32)],
        compiler_params=_params(("arbitrary",)),
        name="ssm",
    )(u_tok, d_skip, perm, perm.T, wpair, m, vpair, a_chunk)


def _attn_tables(rpb, rows):
    kh = min(WIN_H_MAX, rows)
    H = rpb.shape[0]
    w = jnp.arange(GRID_W)
    col_start = jnp.clip(w - WIN_W // 2, 0, GRID_W - WIN_W)
    col_mask = (w[None, :] >= col_start[:, None]) & (w[None, :] < col_start[:, None] + WIN_W)
    dc = jnp.clip(w[None, :] - w[:, None] + (WIN_W - 1), 0, 2 * WIN_W - 2)
    col_sel = (dc[:, :, None] == jnp.arange(2 * WIN_W - 1)[None, None, :]).astype(F32)
    tile = jnp.einsum('hdc,qkc->hdqk', rpb, col_sel, precision=lax.Precision.HIGHEST)
    tile = jnp.where(col_mask[None, None], tile * LOG2E, NEG)
    lo_pad, hi_pad = ATT_ROWS, ATT_WIN - kh + 1
    tile = jnp.pad(tile, ((0, 0), (lo_pad, hi_pad), (0, 0), (0, 0)))
    n_dr = tile.shape[1] - 1
    tiles = jnp.concatenate([tile[:, :-1], tile[:, 1:]], axis=-1).reshape(H // 2, 2, n_dr, GRID_W, 2 * GRID_W)
    i = jnp.arange(ATT_ROWS)
    last_union = rows - ATT_WIN
    last_rel = (rows - kh) - last_union
    rel = jnp.stack([jnp.zeros_like(i), i, jnp.full_like(i, last_rel)])
    j = jnp.arange(ATT_WIN)
    valid = (j[None, None, :] >= rel[:, :, None]) & (j[None, None, :] < rel[:, :, None] + kh)
    masks = jnp.where(valid, 0.0, NEG).astype(F32)
    masks = jnp.repeat(masks.reshape(3 * ATT_ROWS * ATT_WIN // 2, 2), GRID_W, axis=-1)
    return tiles, masks.reshape(-1, 1, 2 * GRID_W)


def _attn_kernel(q_ref, k_ref, v_ref, t_ref, m_ref, o_ref, *, rows, kh):
    n_blocks = rows // ATT_ROWS
    nq = ATT_ROWS * GRID_W
    nk = ATT_WIN * GRID_W
    n_pairs = ATT_WIN // 2
    pw = 2 * GRID_W
    lane = lax.broadcasted_iota(I32, (1, 2 * HEAD_DIM), 1)
    head_lanes = [lane < HEAD_DIM, lane >= HEAD_DIM]
    rel = [[0] * ATT_ROWS, list(range(ATT_ROWS)), [(rows - kh) - (rows - ATT_WIN)] * ATT_ROWS]

    def geometry(rb):
        pattern = 0 if rb == 0 else (2 if rb == n_blocks - 1 else 1)
        union = min(max(rb * ATT_ROWS - kh // 2, 0), rows - ATT_WIN)
        dr0 = union - rb * ATT_ROWS + (WIN_H_MAX - 1) + ATT_ROWS
        return pattern, dr0, pl.ds(rb * nq, nq), pl.ds(union * GRID_W, nk)

    def scores(rb, hl):
        _, _, q_rows, k_rows = geometry(rb)
        qm = jnp.where(head_lanes[hl], q_ref[0, q_rows, :], 0.0).astype(BF16)
        return _nt_dot(qm, k_ref[0, k_rows, :])

    def softmax(rb, hl, s):
        pattern, dr0, _, _ = geometry(rb)
        probs, denoms = [], []
        for i in range(ATT_ROWS):
            first, last = rel[pattern][i], rel[pattern][i] + kh - 1
            pairs = range(first // 2, last // 2 + 1)
            bias = []
            for jp in pairs:
                tile = t_ref[0, hl, dr0 + 2 * jp - i]
                if 2 * jp < first or 2 * jp + 1 > last:
                    tile = tile + m_ref[(pattern * ATT_ROWS + i) * n_pairs + jp]
                bias.append(tile)
            si = s[i * GRID_W:(i + 1) * GRID_W, pairs[0] * pw:(pairs[-1] + 1) * pw] + jnp.concatenate(bias, axis=-1)
            pi = jnp.exp2(si - jnp.max(si, axis=-1, keepdims=True))
            denoms.append(jnp.sum(pi, axis=-1, keepdims=True))
            pieces = [pi.astype(BF16)]
            if pairs[0] > 0:
                pieces.insert(0, jnp.zeros((GRID_W, pairs[0] * pw), BF16))
            if pairs[-1] + 1 < n_pairs:
                pieces.append(jnp.zeros((GRID_W, (n_pairs - 1 - pairs[-1]) * pw), BF16))
            probs.append(jnp.concatenate(pieces, axis=-1))
        return jnp.concatenate(probs, axis=0), jnp.concatenate(denoms, axis=0)

    def weighted_values(rb, hl, p, denom):
        _, _, _, k_rows = geometry(rb)
        vm = jnp.where(head_lanes[hl], v_ref[0, k_rows, :], 0.0).astype(BF16)
        return jnp.dot(p, vm, preferred_element_type=F32) / denom

    passes = [(rb, hl) for rb in range(n_blocks) for hl in range(2)]
    s_next = scores(*passes[0])
    pending = None
    partial = {}

    def finish(item):
        (rb, hl), p, denom = item
        o = weighted_values(rb, hl, p, denom)
        if hl == 0:
            partial[rb] = o
        else:
            o_ref[0, geometry(rb)[2], :] = partial.pop(rb) + o

    for n, (rb, hl) in enumerate(passes):
        s = s_next
        if n + 1 < len(passes):
            s_next = scores(*passes[n + 1])
        p, denom = softmax(rb, hl, s)
        if pending is not None:
            finish(pending)
        pending = ((rb, hl), p, denom)
    finish(pending)


def _attn(qkv, tiles, masks):
    bsz, L, n3 = qkv.shape
    d = n3 // 3
    rows = L // GRID_W
    kh = min(WIN_H_MAX, rows)
    assert rows % ATT_ROWS == 0 and rows // ATT_ROWS >= 3 and ATT_ROWS == kh // 2
    assert ATT_WIN % 2 == 0 and kh + ATT_ROWS - 1 <= ATT_WIN <= rows
    n_hp = d // (2 * HEAD_DIM)
    blk = pl.BlockSpec((1, L, 2 * HEAD_DIM), lambda b, hp: (b, 0, hp))
    return pl.pallas_call(
        functools.partial(_attn_kernel, rows=rows, kh=kh),
        grid=(bsz, n_hp),
        in_specs=[blk,
                  pl.BlockSpec((1, L, 2 * HEAD_DIM), lambda b, hp: (b, 0, n_hp + hp)),
                  pl.BlockSpec((1, L, 2 * HEAD_DIM), lambda b, hp: (b, 0, 2 * n_hp + hp)),
                  pl.BlockSpec((1,) + tiles.shape[1:], lambda b, hp: (hp, 0, 0, 0, 0)),
                  pl.BlockSpec(masks.shape, lambda b, hp: (0, 0, 0))],
        out_specs=blk,
        out_shape=jax.ShapeDtypeStruct((bsz, L, d), F32),
        compiler_params=_params(("arbitrary", "arbitrary")),
        name="attn",
    )(qkv, qkv, qkv, tiles, masks)


def _mix_kernel(ys_ref, wglu_ref, bglu_ref, gs_ref, ya_ref, ga_ref, wout_ref,
                x_ref, gt_ref, gf_ref, sh_ref, sc_ref, wr_ref,
                x1_ref, h2_ref, lg_ref):
    d_ssm = ys_ref.shape[3]
    y = ys_ref[:, 0].reshape(-1, d_ssm)
    z = jax.nn.gelu(y)
    gate = _sigmoid(jnp.dot(z.astype(BF16), wglu_ref[...], preferred_element_type=F32) + bglu_ref[...])
    a = _rms(z * gate, gs_ref[...]).astype(BF16)
    t = _rms(ya_ref[0], ga_ref[...]).astype(BF16)
    mixed = (jnp.dot(a, wout_ref[:d_ssm, :], preferred_element_type=F32)
             + jnp.dot(t, wout_ref[d_ssm:, :], preferred_element_type=F32))
    x1 = x_ref[0] + gt_ref[0] * mixed
    x1_ref[0] = x1
    h2 = _rms(x1, gf_ref[...]) * (1.0 + sc_ref[0]) + sh_ref[0]
    h2_hi = h2.astype(BF16)
    h2_ref[0] = h2_hi
    h2_lo = (h2 - h2_hi.astype(F32)).astype(BF16)
    wr = wr_ref[...]
    wr_hi = wr.astype(BF16)
    wr_lo = (wr - wr_hi.astype(F32)).astype(BF16)
    n_e = wr.shape[1]
    tm = h2.shape[0]
    r = jnp.dot(jnp.concatenate([h2_hi, h2_lo], axis=0), jnp.concatenate([wr_hi, wr_lo], axis=1),
                preferred_element_type=F32)
    lg_ref[0] = r[:tm, :n_e] + r[:tm, n_e:] + r[tm:, :n_e]


def _mix(ys, wglu_bf, b_glu, g_ssm, ya, g_attn, wout_bf, x, gt1, g_ffn, sh2, sc2, w_router):
    bsz, L, D = x.shape
    d_ssm = ys.shape[3]
    E = w_router.shape[1]
    tm = 256
    T = SSM_T
    tile = lambda n: pl.BlockSpec((1, tm, n), lambda b, i: (b, i, 0))
    per_b = lambda n: pl.BlockSpec((1, 1, n), lambda b, i: (b, 0, 0))
    full = lambda r, n: pl.BlockSpec((r, n), lambda b, i: (0, 0))
    return pl.pallas_call(
        _mix_kernel,
        grid=(bsz, L // tm),
        in_specs=[pl.BlockSpec((tm // T, 1, T, d_ssm), lambda b, i: (i, b, 0, 0)),
                  full(d_ssm, d_ssm), full(1, d_ssm), full(1, d_ssm),
                  tile(d_ssm), full(1, d_ssm), full(D, D),
                  tile(D), per_b(D), full(1, D), per_b(D), per_b(D), full(D, E)],
        out_specs=[tile(D), tile(D), tile(E)],
        out_shape=[jax.ShapeDtypeStruct((bsz, L, D), F32),
                   jax.ShapeDtypeStruct((bsz, L, D), BF16),
                   jax.ShapeDtypeStruct((bsz, L, E), F32)],
        compiler_params=_params(("arbitrary", "arbitrary")),
        name="mix",
    )(ys, wglu_bf, b_glu, g_ssm, ya, g_attn, wout_bf, x, gt1, g_ffn, sh2, sc2, w_router)


TOPK_BISECTIONS = 160


def _topk_kernel(lg_ref, aff_ref, pos_ref, *, cap):
    lg = lg_ref[...]
    bsz, E, L = lg.shape
    e = jnp.exp(lg - jnp.max(lg, axis=1, keepdims=True))
    aff3 = e / jnp.sum(e, axis=1, keepdims=True)
    aff_ref[...] = aff3
    aff = aff3.reshape(bsz * E, L)

    def count(mask):
        return jnp.sum(jnp.where(mask, 1.0, 0.0), axis=-1, keepdims=True)

    def midpoint(lo, hi):
        return lo + (hi - lo) * 0.5

    def unresolved(carry):
        lo, hi, it = carry
        mid = midpoint(lo, hi)
        open_rows = jnp.max(jnp.where((mid > lo) & (mid < hi), 1.0, 0.0))
        return jnp.logical_and(it < TOPK_BISECTIONS, open_rows > 0.0)

    def halve(carry):
        lo, hi, it = carry
        mid = midpoint(lo, hi)
        ok = count(aff >= mid) >= cap
        return jnp.where(ok, mid, lo), jnp.where(ok, hi, mid), it + 1

    lo0 = jnp.zeros((bsz * E, 1), F32)
    thr, _, _ = lax.while_loop(unresolved, halve, (lo0, lo0 + 2.0, jnp.int32(0)))
    gt = aff > thr
    eq = aff == thr
    need = cap - count(gt)
    blk = LANES
    tri = jnp.where(lax.broadcasted_iota(I32, (blk, blk), 0) < lax.broadcasted_iota(I32, (blk, blk), 1),
                    1.0, 0.0).astype(BF16)

    def prefix_count(mask):
        ones = jnp.where(mask, 1.0, 0.0)
        run = jnp.zeros((bsz * E, 1), F32)
        outs = []
        for j in range(L // blk):
            piece = ones[:, j * blk:(j + 1) * blk]
            outs.append(jnp.dot(piece.astype(BF16), tri, preferred_element_type=F32) + run)
            run = run + jnp.sum(piece, axis=-1, keepdims=True)
        return jnp.concatenate(outs, axis=-1)

    sel = gt | (eq & (prefix_count(eq) < need))
    pos = prefix_count(sel)
    pos_ref[...] = jnp.where(sel, pos.astype(I32), -1).reshape(bsz, E, L)


def _topk(lg_t, cap):
    bsz, E, L = lg_t.shape
    spec = pl.BlockSpec((bsz, E, L), lambda i: (0, 0, 0))
    return pl.pallas_call(
        functools.partial(_topk_kernel, cap=cap),
        grid=(1,),
        in_specs=[spec],
        out_specs=[spec, spec],
        out_shape=[jax.ShapeDtypeStruct((bsz, E, L), F32), jax.ShapeDtypeStruct((bsz, E, L), I32)],
        compiler_params=_params(("arbitrary",)),
        name="topk",
    )(lg_t)


def _gather_kernel(pos_ref, aff_ref, h_ref, xe_ref, as_ref, *, cap):
    L = h_ref.shape[1]
    n_e = pos_ref.shape[1]
    slot = lax.broadcasted_iota(I32, (cap, L), 0)
    hits = [pos_ref[0, e] == slot for e in range(n_e)]
    onehot = jnp.concatenate([jnp.where(h, 1.0, 0.0).astype(BF16) for h in hits], axis=0)
    xe = jnp.dot(onehot, h_ref[0], preferred_element_type=F32).astype(BF16)
    for e in range(n_e):
        xe_ref[e] = xe[e * cap:(e + 1) * cap]
        as_ref[e] = jnp.sum(jnp.where(hits[e], aff_ref[0, e], 0.0), axis=-1, keepdims=True)


def _gather(pos, aff, h2, cap):
    bsz, E, L = pos.shape
    D = h2.shape[2]
    pos4 = pos.reshape(bsz, E, 1, L)
    aff4 = aff.reshape(bsz, E, 1, L)
    ge = 4
    return pl.pallas_call(
        functools.partial(_gather_kernel, cap=cap),
        grid=(bsz, E // ge),
        in_specs=[pl.BlockSpec((1, ge, 1, L), lambda b, e: (b, e, 0, 0)),
                  pl.BlockSpec((1, ge, 1, L), lambda b, e: (b, e, 0, 0)),
                  pl.BlockSpec((1, L, D), lambda b, e: (b, 0, 0))],
        out_specs=[pl.BlockSpec((ge, cap, D), lambda b, e: (e, b, 0)),
                   pl.BlockSpec((ge, cap, 1), lambda b, e: (e, b, 0))],
        out_shape=[jax.ShapeDtypeStruct((E, bsz * cap, D), BF16),
                   jax.ShapeDtypeStruct((E, bsz * cap, 1), F32)],
        compiler_params=_params(("arbitrary", "arbitrary")),
        name="gather",
    )(pos4, aff4, h2)


def _ffn_kernel(xe_ref, wg_ref, wu_ref, wd_ref, as_ref, ye_ref, acc_ref):
    f = pl.program_id(1)

    @pl.when(f == 0)
    def _():
        acc_ref[...] = jnp.zeros_like(acc_ref)

    x = xe_ref[0]
    g = jnp.dot(x, wg_ref[0].astype(BF16), preferred_element_type=F32)
    up = jnp.dot(x, wu_ref[0].astype(BF16), preferred_element_type=F32)
    hid = (g * _sigmoid(g) * up).astype(BF16)
    acc_ref[...] += jnp.dot(hid, wd_ref[0].astype(BF16), preferred_element_type=F32)

    @pl.when(f == pl.num_programs(1) - 1)
    def _():
        bsz, _, cap, _ = ye_ref.shape
        y = (acc_ref[...] * as_ref[0]).astype(BF16)
        for b in range(bsz):
            ye_ref[b, 0] = y[b * cap:(b + 1) * cap]


def _ffn(xe, w_gate, w_up, w_down, aff_slot, bsz):
    E, R, D = xe.shape
    F = w_gate.shape[2]
    cap = R // bsz
    tf = 256
    return pl.pallas_call(
        _ffn_kernel,
        grid=(E, F // tf),
        in_specs=[pl.BlockSpec((1, R, D), lambda e, f: (e, 0, 0)),
                  pl.BlockSpec((1, D, tf), lambda e, f: (e, 0, f)),
                  pl.BlockSpec((1, D, tf), lambda e, f: (e, 0, f)),
                  pl.BlockSpec((1, tf, D), lambda e, f: (e, f, 0)),
                  pl.BlockSpec((1, R, 1), lambda e, f: (e, 0, 0))],
        out_specs=pl.BlockSpec((bsz, 1, cap, D), lambda e, f: (0, e, 0, 0)),
        out_shape=jax.ShapeDtypeStruct((bsz, E, cap, D), BF16),
        scratch_shapes=[pltpu.VMEM((R, D), F32)],
        compiler_params=_params(("arbitrary", "arbitrary")),
        name="ffn",
    )(xe, w_gate, w_up, w_down, aff_slot)


def _combine_kernel(pos_ref, ye_ref, x1_ref, gt_ref, g_ref, o_ref, *, final_norm):
    pos_t = pos_ref[0]
    n_e, cap, d = ye_ref.shape[1:]
    slot = lax.broadcasted_iota(I32, (pos_t.shape[0], cap), 1)
    onehot = jnp.concatenate([jnp.where(pos_t[:, e:e + 1] == slot, 1.0, 0.0).astype(BF16) for e in range(n_e)],
                             axis=-1)
    moe = jnp.dot(onehot, ye_ref[0].reshape(n_e * cap, d), preferred_element_type=F32)
    o = x1_ref[0] + gt_ref[0] * moe
    o_ref[0] = _rms(o, g_ref[...]) if final_norm else o


def _combine(pos_t, ye, x1, gt2, g_final, final_norm):
    bsz, L, E = pos_t.shape
    D = x1.shape[2]
    cap = ye.shape[2]
    tl = 512
    return pl.pallas_call(
        functools.partial(_combine_kernel, final_norm=final_norm),
        grid=(bsz, L // tl),
        in_specs=[pl.BlockSpec((1, tl, E), lambda b, i: (b, i, 0)),
                  pl.BlockSpec((1, E, cap, D), lambda b, i: (b, 0, 0, 0), pipeline_mode=pl.Buffered(1)),
                  pl.BlockSpec((1, tl, D), lambda b, i: (b, i, 0)),
                  pl.BlockSpec((1, 1, D), lambda b, i: (b, 0, 0)),
                  pl.BlockSpec((1, D), lambda b, i: (0, 0))],
        out_specs=pl.BlockSpec((1, tl, D), lambda b, i: (b, i, 0)),
        out_shape=jax.ShapeDtypeStruct((bsz, L, D), F32),
        compiler_params=_params(("arbitrary", "arbitrary")),
        name="combine",
    )(pos_t, ye, x1, gt2, g_final)


def kernel(x, c, w_ada, b_ada, g_mix, w_in, ssm_a_re, ssm_a_im, ssm_log_dt, ssm_b_re, ssm_b_im,
           ssm_c_re, ssm_c_im, ssm_d, w_glu, b_glu, rpb, g_ssm_out, g_attn_out, w_out, g_ffn,
           w_router, w_gate, w_up, w_down, g_final):
    bsz, L, D = x.shape
    depth = w_ada.shape[0]
    d_ssm = ssm_d.shape[1]
    E = w_router.shape[2]
    cap = CAPACITY_FACTOR * L // E
    rows = L // GRID_W
    c8 = jnp.zeros((8, D), F32).at[:bsz].set(c)
    row = lambda v: v.reshape(1, -1)
    for layer in range(depth):
        mod = _ada(c8, w_ada[layer], row(b_ada[layer]))[:bsz]
        sh1, sc1, gt1, sh2, sc2, gt2 = [m.reshape(bsz, 1, D) for m in jnp.split(mod, 6, axis=-1)]

        u, qkv = _inproj(x, sh1, sc1, row(g_mix[layer]), _cast_bf16(w_in[layer]), d_ssm)
        ops = _ssm_operators(ssm_a_re[layer], ssm_a_im[layer], ssm_log_dt[layer], ssm_b_re[layer],
                             ssm_b_im[layer], ssm_c_re[layer], ssm_c_im[layer])
        y_ssm = _ssm(u.reshape(-1, d_ssm), row(ssm_d[layer]), ops, bsz).reshape(u.shape)
        y_attn = _attn(qkv, *_attn_tables(rpb[layer], rows))
        x1, h2, logits = _mix(y_ssm, _cast_bf16(w_glu[layer]), row(b_glu[layer]),
                              row(g_ssm_out[layer]), y_attn, row(g_attn_out[layer]), _cast_bf16(w_out[layer]),
                              x, gt1, row(g_ffn[layer]), sh2, sc2, w_router[layer])

        aff, pos = _topk(jnp.swapaxes(logits, 1, 2), cap)
        xe, aff_slot = _gather(pos, aff, h2, cap)
        ye = _ffn(xe, w_gate[layer], w_up[layer], w_down[layer], aff_slot, bsz)
        x = _combine(jnp.swapaxes(pos, 1, 2), ye, x1, gt2, row(g_final), layer == depth - 1)
    return x
```

```python
import functools
import math

import jax
import jax.numpy as jnp
from jax import lax
from jax.experimental import pallas as pl
from jax.experimental.pallas import tpu as pltpu

F32 = jnp.float32
BF16 = jnp.bfloat16
I32 = jnp.int32

EPS = 1e-6
GRID_W = 64
SSM_GROUP = 16
SSM_STATE = 64
N_HEADS = 16
HEAD_DIM = 64
WIN_H_MAX = 8
WIN_W = 16
N_EXPERTS = 16
CAPACITY_FACTOR = 2
SSM_T = 16
LANES = 128
NEG = -1e30
LOG2E = math.log2(math.e)

ATT_ROWS = 4
ATT_WIN = 12
VMEM_LIMIT = 56 * 1024 * 1024


def _params(sem, vmem=VMEM_LIMIT):
    return pltpu.CompilerParams(dimension_semantics=sem, vmem_limit_bytes=vmem)


def _sigmoid(x):
    return 1.0 / (1.0 + jnp.exp(-x))


def _rms(x, g):
    return x * lax.rsqrt(jnp.mean(x * x, axis=-1, keepdims=True) + EPS) * g


def _ada_kernel(c_ref, w_ref, b_ref, o_ref):
    c = c_ref[...]
    ca = (c * _sigmoid(c)).astype(BF16)
    o_ref[...] = jnp.dot(ca, w_ref[...].astype(BF16), preferred_element_type=F32) + b_ref[...]


def _ada(c8, w, b):
    k, n = w.shape
    tn = 1024
    return pl.pallas_call(
        _ada_kernel,
        grid=(n // tn,),
        in_specs=[pl.BlockSpec((8, k), lambda j: (0, 0)),
                  pl.BlockSpec((k, tn), lambda j: (0, j)),
                  pl.BlockSpec((1, tn), lambda j: (0, j))],
        out_specs=pl.BlockSpec((8, tn), lambda j: (0, j)),
        out_shape=jax.ShapeDtypeStruct((8, n), F32),
        compiler_params=_params(("arbitrary",)),
        name="ada",
    )(c8, w, b)


def _cast_kernel(w_ref, o_ref):
    o_ref[...] = w_ref[...].astype(BF16)


def _cast_bf16(w):
    k, n = w.shape
    tk = 256
    return pl.pallas_call(
        _cast_kernel,
        grid=(k // tk,),
        in_specs=[pl.BlockSpec((tk, n), lambda i: (i, 0))],
        out_specs=pl.BlockSpec((tk, n), lambda i: (i, 0)),
        out_shape=jax.ShapeDtypeStruct((k, n), BF16),
        compiler_params=_params(("arbitrary",)),
        name="cast",
    )(w)


def _inproj_kernel(x_ref, sh_ref, sc_ref, g_ref, w_ref, u_ref, qkv_ref, *, d_ssm):
    h = (_rms(x_ref[0], g_ref[...]) * (1.0 + sc_ref[0]) + sh_ref[0]).astype(BF16)
    n_total = w_ref.shape[1]
    for n in range(n_total // d_ssm):
        r = jnp.dot(h, w_ref[:, n * d_ssm:(n + 1) * d_ssm], preferred_element_type=F32)
        if n == 0:
            for cc in range(u_ref.shape[0]):
                u_ref[cc, 0] = r[cc * SSM_T:(cc + 1) * SSM_T]
        else:
            if n == 1:
                r = r * (HEAD_DIM ** -0.5 * LOG2E)
            qkv_ref[0, :, (n - 1) * d_ssm:n * d_ssm] = r.astype(BF16)


def _inproj(x, sh, sc, g, w_bf, d_ssm):
    bsz, L, D = x.shape
    n = w_bf.shape[1]
    tm = 256
    T = SSM_T
    return pl.pallas_call(
        functools.partial(_inproj_kernel, d_ssm=d_ssm),
        grid=(bsz, L // tm),
        in_specs=[pl.BlockSpec((1, tm, D), lambda b, i: (b, i, 0)),
                  pl.BlockSpec((1, 1, D), lambda b, i: (b, 0, 0)),
                  pl.BlockSpec((1, 1, D), lambda b, i: (b, 0, 0)),
                  pl.BlockSpec((1, D), lambda b, i: (0, 0)),
                  pl.BlockSpec((D, n), lambda b, i: (0, 0))],
        out_specs=[pl.BlockSpec((tm // T, 1, T, d_ssm), lambda b, i: (i, b, 0, 0)),
                   pl.BlockSpec((1, tm, n - d_ssm), lambda b, i: (b, i, 0))],
        out_shape=[jax.ShapeDtypeStruct((L // T, bsz, T, d_ssm), F32),
                   jax.ShapeDtypeStruct((bsz, L, n - d_ssm), BF16)],
        compiler_params=_params(("arbitrary", "arbitrary")),
        name="inproj",
    )(x, sh, sc, g, w_bf)


def _ssm_operators(a_re, a_im, log_dt, b_re, b_im, c_re, c_im):
    T = SSM_T
    G, P = a_re.shape[1], a_re.shape[2]
    H = b_re.shape[3]
    dt = jnp.exp(log_dt)[..., None]
    mag = jnp.exp(a_re * dt)
    lbr = mag * jnp.cos(a_im * dt)
    lbi = mag * jnp.sin(a_im * dt)
    den = a_re * a_re + a_im * a_im
    nr = lbr - 1.0
    fr = ((nr * a_re + lbi * a_im) / den)[:, :, None, :]
    fi = ((lbi * a_re - nr * a_im) / den)[:, :, None, :]
    b_re_t, b_im_t = jnp.swapaxes(b_re, 2, 3), jnp.swapaxes(b_im, 2, 3)
    bbr = fr * b_re_t - fi * b_im_t
    bbi = fr * b_im_t + fi * b_re_t
    prs, pis = [jnp.ones_like(lbr)], [jnp.zeros_like(lbi)]
    for _ in range(T):
        pr, pi = prs[-1], pis[-1]
        prs.append(pr * lbr - pi * lbi)
        pis.append(pr * lbi + pi * lbr)
    pr = jnp.stack(prs, axis=2)
    pi = jnp.stack(pis, axis=2)
    assert H == T
    rev = lambda x: x[:, ::-1]
    sections = [bbr[0], bbi[0], bbr[1], bbi[1], c_re[0], c_im[0], c_re[1], c_im[1],
                rev(pr[0, :, :T]), rev(pi[0, :, :T]), pr[1, :, :T], pi[1, :, :T],
                pr[0, :, 1:], pi[0, :, 1:], rev(pr[1, :, 1:]), rev(pi[1, :, 1:]),
                pr[0, :, :T], pi[0, :, :T]]
    m, wpair, vtpair = _ssm_prep(jnp.concatenate(sections, axis=1))
    a4 = jnp.stack([pr[0, :, T], pi[0, :, T], pr[1, :, T], pi[1, :, T]])
    a_chunk = jnp.transpose(a4.reshape(4, G // 2, 2, P), (1, 0, 2, 3)).reshape(1, G // 2 * 8 * P)
    return m, wpair, vtpair, a_chunk


def _nt_dot(a, b):
    return lax.dot_general(a, b, (((1,), (1,)), ((), ())), preferred_element_type=F32)


def _ssm_prep_kernel(pk_ref, m_ref, w_ref, vt_ref, mats_ref, rows_ref, kmats_ref, krows_ref):
    T = SSM_T
    H = T
    kw = T * H
    half = kmats_ref.shape[2] // 2
    lane = lax.broadcasted_iota(I32, (1, 2 * half), 1)
    cat = lambda xs: jnp.concatenate(xs, axis=-1)

    def group(g, carry):
        (bbr_f, bbi_f, bbr_b, bbi_b, cre_f, cim_f, cre_b, cim_b, qrf, qif, qrb, qib,
         rrf, rif, rrb, rib, krf, kif) = [pk_ref[g, i * T:(i + 1) * T, :] for i in range(18)]
        zero = jnp.zeros_like(bbr_f)
        even = g % 2 == 0

        def slots(v4):
            ev = cat([v4[0], zero, v4[1], zero, v4[2], zero, v4[3], zero])
            od = cat([zero, v4[0], zero, v4[1], zero, v4[2], zero, v4[3]])
            return jnp.where(even, ev, od)

        mats_ref[0] = slots([bbr_f, bbi_f, bbr_b, bbi_b])
        mats_ref[1] = slots([-bbi_f, bbr_f, -bbi_b, bbr_b])
        mats_ref[2] = slots([cre_f, -cim_f, cre_b, -cim_b])
        mats_ref[3] = slots([-cim_f, -cre_f, -cim_b, -cre_b])
        rows_ref[0] = cat([qrf] * 4 + [qrb] * 4)
        rows_ref[1] = cat([qif] * 4 + [qib] * 4)
        rows_ref[2] = cat([rrf] * 4 + [rrb] * 4)
        rows_ref[3] = cat([rif] * 4 + [rib] * 4)
        kmats_ref[0] = cat([bbr_f, -bbi_f, bbr_b, -bbi_b])
        kmats_ref[1] = cat([cre_f, cre_f, cre_b, cre_b])
        kmats_ref[2] = cat([-cim_f, cim_f, -cim_b, cim_b])
        krows_ref[0] = cat([krf, kif, qrb, qib])
        krows_ref[1] = cat([kif, krf, qib, qrb])

        pair, row0 = g // 2, (g % 2) * kw
        aw, bw, av, bv = [mats_ref[i] for i in range(4)]
        for s in range(T):
            r = pl.ds(pl.multiple_of(row0 + s * H, H), H)
            w_ref[pair, r, :] = (aw * rows_ref[0, pl.ds(s, 1), :] + bw * rows_ref[1, pl.ds(s, 1), :]).astype(BF16)
            vt_ref[pair, r, :] = (av * rows_ref[2, pl.ds(s, 1), :] + bv * rows_ref[3, pl.ds(s, 1), :]).astype(BF16)
        lhs, a4, b4 = [kmats_ref[i] for i in range(3)]
        blocks = []
        for i in range(2 * T):
            k = abs(i - (T - 1))
            if i == 2 * T - 1:
                blocks.append(jnp.zeros_like(a4))
                continue
            cp = a4 * krows_ref[0, pl.ds(k, 1), :] + b4 * krows_ref[1, pl.ds(k, 1), :]
            if i < T - 1:
                cp = jnp.where(lane >= half, cp, 0.0)
            elif i > T - 1:
                cp = jnp.where(lane < half, cp, 0.0)
            blocks.append(cp)
        rhs = jnp.concatenate(blocks, axis=0)
        lhs_hi = lhs.astype(BF16)
        lhs_lo = (lhs - lhs_hi.astype(F32)).astype(BF16)
        rhs_hi = rhs.astype(BF16)
        rhs_lo = (rhs - rhs_hi.astype(F32)).astype(BF16)
        z = _nt_dot(lhs_hi, rhs_hi) + _nt_dot(lhs_lo, rhs_hi) + _nt_dot(lhs_hi, rhs_lo)
        for s in range(T):
            off = (T - 1 - s) * H
            m_ref[g, s * H:(s + 1) * H, :] = z[:, off:off + kw].astype(BF16)
        return carry

    lax.fori_loop(0, pk_ref.shape[0], group, 0)


def _ssm_prep(packed):
    G, n_rows, P = packed.shape
    T = H = SSM_T
    gs = LANES // SSM_GROUP
    return pl.pallas_call(
        _ssm_prep_kernel,
        grid=(G // gs,),
        in_specs=[pl.BlockSpec((gs, n_rows, P), lambda j: (j, 0, 0))],
        out_specs=[pl.BlockSpec((gs, T * H, T * H), lambda j: (j, 0, 0)),
                   pl.BlockSpec((gs // 2, 2 * T * H, 8 * P), lambda j: (j, 0, 0)),
                   pl.BlockSpec((gs // 2, 2 * T * H, 8 * P), lambda j: (j, 0, 0))],
        out_shape=[jax.ShapeDtypeStruct((G, T * H, T * H), BF16),
                   jax.ShapeDtypeStruct((G // 2, 2 * T * H, 8 * P), BF16),
                   jax.ShapeDtypeStruct((G // 2, 2 * T * H, 8 * P), BF16)],
        scratch_shapes=[pltpu.VMEM((4, H, 8 * P), F32), pltpu.VMEM((4, T, 8 * P), F32),
                        pltpu.VMEM((3, H, 4 * P), F32), pltpu.VMEM((2, T, 4 * P), F32)],
        compiler_params=_params(("arbitrary",)),
        name="ssm_prep",
    )(packed)


def _slab_permutation():
    idx = jnp.arange(8 * LANES)
    t, g, h = idx // LANES, (idx % LANES) // SSM_GROUP, idx % SSM_GROUP
    dst = g * LANES + t * SSM_GROUP + h
    return (dst[:, None] == jnp.arange(8 * LANES)[None, :]).astype(BF16)


def _chunk_scan(s_ref, a_ref, *, bsz, n_pairs):
    n_it = s_ref.shape[0] // (2 * bsz)
    cols = [[pl.ds((4 * j + q) * LANES, LANES) for q in range(4)] for j in range(n_pairs)]
    decay = [[a_ref[:, c] for c in cols[j]] for j in range(n_pairs)]

    def step(i, carry):
        rf = pl.ds(pl.multiple_of(i * 2 * bsz, 2 * bsz), 2 * bsz)
        rb = pl.ds(pl.multiple_of((n_it - 1 - i) * 2 * bsz, 2 * bsz), 2 * bsz)
        out = []
        for j in range(n_pairs):
            ar_f, ai_f, ar_b, ai_b = decay[j]
            xr, xi, yr, yi = carry[4 * j:4 * j + 4]
            sr, si = s_ref[rf, cols[j][0]], s_ref[rf, cols[j][1]]
            xr1 = ar_f * xr - ai_f * xi + sr[:bsz]
            xi1 = ar_f * xi + ai_f * xr + si[:bsz]
            s_ref[rf, cols[j][0]] = jnp.concatenate([xr, xr1], axis=0)
            s_ref[rf, cols[j][1]] = jnp.concatenate([xi, xi1], axis=0)
            xr2 = ar_f * xr1 - ai_f * xi1 + sr[bsz:]
            xi2 = ar_f * xi1 + ai_f * xr1 + si[bsz:]
            tr, ti = s_ref[rb, cols[j][2]], s_ref[rb, cols[j][3]]
            yr1 = ar_b * yr - ai_b * yi + tr[bsz:]
            yi1 = ar_b * yi + ai_b * yr + ti[bsz:]
            s_ref[rb, cols[j][2]] = jnp.concatenate([yr1, yr], axis=0)
            s_ref[rb, cols[j][3]] = jnp.concatenate([yi1, yi], axis=0)
            yr2 = ar_b * yr1 - ai_b * yi1 + tr[:bsz]
            yi2 = ar_b * yi1 + ai_b * yr1 + ti[:bsz]
            out += [xr2, xi2, yr2, yi2]
        return tuple(out)

    z = jnp.zeros((bsz, LANES), F32)
    lax.fori_loop(0, n_it, step, (z,) * (4 * n_pairs))


def _ssm_kernel(u_ref, d_ref, perm_ref, iperm_ref, w_ref, m_ref, v_ref, a_ref, y_ref,
                ug_ref, s_ref, yg_ref, *, bsz):
    T = SSM_T
    R = u_ref.shape[0] // T
    n_groups = LANES // SSM_GROUP
    kw = T * SSM_GROUP
    for tq in range(T // 8):
        cat = jnp.concatenate([u_ref[pl.ds(tq * 8 + t, R, stride=T), :].astype(BF16) for t in range(8)], axis=-1)
        grouped = jnp.dot(cat, perm_ref[...], preferred_element_type=F32).astype(BF16)
        for g in range(n_groups):
            ug_ref[g, :, tq * LANES:(tq + 1) * LANES] = grouped[:, g * LANES:(g + 1) * LANES]
    for pr in range(n_groups // 2):
        u2 = jnp.concatenate([ug_ref[2 * pr], ug_ref[2 * pr + 1]], axis=-1)
        s_ref[:, pr * 4 * LANES:(pr + 1) * 4 * LANES] = jnp.dot(u2, w_ref[pr], preferred_element_type=F32)
    _chunk_scan(s_ref, a_ref, bsz=bsz, n_pairs=n_groups // 2)
    for pr in range(n_groups // 2):
        inter = _nt_dot(s_ref[:, pr * 4 * LANES:(pr + 1) * 4 * LANES].astype(BF16), v_ref[pr])
        for gl in range(2):
            g = 2 * pr + gl
            yg_ref[g] = (jnp.dot(ug_ref[g], m_ref[g], preferred_element_type=F32)
                         + inter[:, gl * kw:(gl + 1) * kw])
    for tq in range(T // 8):
        ycat = jnp.concatenate([yg_ref[g, :, tq * LANES:(tq + 1) * LANES] for g in range(n_groups)], axis=-1)
        y_hi = ycat.astype(BF16)
        y_lo = (ycat - y_hi.astype(F32)).astype(BF16)
        back = (jnp.dot(y_hi, iperm_ref[...], preferred_element_type=F32)
                + jnp.dot(y_lo, iperm_ref[...], preferred_element_type=F32))
        for t in range(8):
            rows = pl.ds(tq * 8 + t, R, stride=T)
            y_ref[rows, :] = back[:, t * LANES:(t + 1) * LANES] + d_ref[...] * u_ref[rows, :]


def _ssm(u_tok, d_skip, ops, bsz):
    m, wpair, vpair, a_chunk = ops
    N, d_ssm = u_tok.shape
    T = SSM_T
    R = N // T
    n_groups = LANES // SSM_GROUP
    perm = _slab_permutation()
    const = lambda shape: pl.BlockSpec(shape, lambda j: (0,) * len(shape))
    return pl.pallas_call(
        functools.partial(_ssm_kernel, bsz=bsz),
        grid=(d_ssm // LANES,),
        in_specs=[pl.BlockSpec((N, LANES), lambda j: (0, j)),
                  pl.BlockSpec((1, LANES), lambda j: (0, j)),
                  const(perm.shape), const(perm.shape),
                  pl.BlockSpec((n_groups // 2,) + wpair.shape[1:], lambda j: (j, 0, 0)),
                  pl.BlockSpec((n_groups,) + m.shape[1:], lambda j: (j, 0, 0)),
                  pl.BlockSpec((n_groups // 2,) + vpair.shape[1:], lambda j: (j, 0, 0)),
                  pl.BlockSpec((1, n_groups // 2 * 4 * LANES), lambda j: (0, j))],
        out_specs=pl.BlockSpec((N, LANES), lambda j: (0, j)),
        out_shape=jax.ShapeDtypeStruct((N, d_ssm), F32),
        scratch_shapes=[pltpu.VMEM((n_groups, R, T * SSM_GROUP), BF16),
                        pltpu.VMEM((R, n_groups // 2 * 4 * LANES), F32),
                        pltpu.VMEM((n_groups, R, T * SSM_GROUP), F32)],
        compiler_params=_params(("arbitrary",)),
        name="ssm",
    )(u_tok, d_skip, perm, perm.T, wpair, m, vpair, a_chunk)


def _attn_tables(rpb, rows):
    kh = min(WIN_H_MAX, rows)
    H = rpb.shape[0]
    w = jnp.arange(GRID_W)
    col_start = jnp.clip(w - WIN_W // 2, 0, GRID_W - WIN_W)
    col_mask = (w[None, :] >= col_start[:, None]) & (w[None, :] < col_start[:, None] + WIN_W)
    dc = jnp.clip(w[None, :] - w[:, None] + (WIN_W - 1), 0, 2 * WIN_W - 2)
    col_sel = (dc[:, :, None] == jnp.arange(2 * WIN_W - 1)[None, None, :]).astype(F32)
    tile = jnp.einsum('hdc,qkc->hdqk', rpb, col_sel, precision=lax.Precision.HIGHEST)
    tile = jnp.where(col_mask[None, None], tile * LOG2E, NEG)
    lo_pad, hi_pad = ATT_ROWS, ATT_WIN - kh + 1
    tile = jnp.pad(tile, ((0, 0), (lo_pad, hi_pad), (0, 0), (0, 0)))
    n_dr = tile.shape[1] - 1
    tiles = jnp.concatenate([tile[:, :-1], tile[:, 1:]], axis=-1).reshape(H // 2, 2, n_dr, GRID_W, 2 * GRID_W)
    i = jnp.arange(ATT_ROWS)
    last_union = rows - ATT_WIN
    last_rel = (rows - kh) - last_union
    rel = jnp.stack([jnp.zeros_like(i), i, jnp.full_like(i, last_rel)])
    j = jnp.arange(ATT_WIN)
    valid = (j[None, None, :] >= rel[:, :, None]) & (j[None, None, :] < rel[:, :, None] + kh)
    masks = jnp.where(valid, 0.0, NEG).astype(F32)
    masks = jnp.repeat(masks.reshape(3 * ATT_ROWS * ATT_WIN // 2, 2), GRID_W, axis=-1)
    return tiles, masks.reshape(-1, 1, 2 * GRID_W)


def _attn_kernel(q_ref, k_ref, v_ref, t_ref, m_ref, o_ref, *, rows, kh):
    n_blocks = rows // ATT_ROWS
    nq = ATT_ROWS * GRID_W
    nk = ATT_WIN * GRID_W
    n_pairs = ATT_WIN // 2
    pw = 2 * GRID_W
    lane = lax.broadcasted_iota(I32, (1, 2 * HEAD_DIM), 1)
    head_lanes = [lane < HEAD_DIM, lane >= HEAD_DIM]
    rel = [[0] * ATT_ROWS, list(range(ATT_ROWS)), [(rows - kh) - (rows - ATT_WIN)] * ATT_ROWS]

    def geometry(rb):
        pattern = 0 if rb == 0 else (2 if rb == n_blocks - 1 else 1)
        union = min(max(rb * ATT_ROWS - kh // 2, 0), rows - ATT_WIN)
        dr0 = union - rb * ATT_ROWS + (WIN_H_MAX - 1) + ATT_ROWS
        return pattern, dr0, pl.ds(rb * nq, nq), pl.ds(union * GRID_W, nk)

    def scores(rb, hl):
        _, _, q_rows, k_rows = geometry(rb)
        qm = jnp.where(head_lanes[hl], q_ref[0, q_rows, :], 0.0).astype(BF16)
        return _nt_dot(qm, k_ref[0, k_rows, :])

    def softmax(rb, hl, s):
        pattern, dr0, _, _ = geometry(rb)
        probs, denoms = [], []
        for i in range(ATT_ROWS):
            first, last = rel[pattern][i], rel[pattern][i] + kh - 1
            pairs = range(first // 2, last // 2 + 1)
            bias = []
            for jp in pairs:
                tile = t_ref[0, hl, dr0 + 2 * jp - i]
                if 2 * jp < first or 2 * jp + 1 > last:
                    tile = tile + m_ref[(pattern * ATT_ROWS + i) * n_pairs + jp]
                bias.append(tile)
            si = s[i * GRID_W:(i + 1) * GRID_W, pairs[0] * pw:(pairs[-1] + 1) * pw] + jnp.concatenate(bias, axis=-1)
            pi = jnp.exp2(si - jnp.max(si, axis=-1, keepdims=True))
            denoms.append(jnp.sum(pi, axis=-1, keepdims=True))
            pieces = [pi.astype(BF16)]
            if pairs[0] > 0:
                pieces.insert(0, jnp.zeros((GRID_W, pairs[0] * pw), BF16))
            if pairs[-1] + 1 < n_pairs:
                pieces.append(jnp.zeros((GRID_W, (n_pairs - 1 - pairs[-1]) * pw), BF16))
            probs.append(jnp.concatenate(pieces, axis=-1))
        return jnp.concatenate(probs, axis=0), jnp.concatenate(denoms, axis=0)

    def weighted_values(rb, hl, p, denom):
        _, _, _, k_rows = geometry(rb)
        vm = jnp.where(head_lanes[hl], v_ref[0, k_rows, :], 0.0).astype(BF16)
        return jnp.dot(p, vm, preferred_element_type=F32) / denom

    passes = [(rb, hl) for rb in range(n_blocks) for hl in range(2)]
    ahead = 1
    queue = [scores(*passes[k]) for k in range(ahead)]
    pending = None
    partial = {}

    def finish(item):
        (rb, hl), p, denom = item
        o = weighted_values(rb, hl, p, denom)
        if hl == 0:
            partial[rb] = o
        else:
            o_ref[0, geometry(rb)[2], :] = partial.pop(rb) + o

    for n, (rb, hl) in enumerate(passes):
        s = queue.pop(0)
        if n + ahead < len(passes):
            queue.append(scores(*passes[n + ahead]))
        p, denom = softmax(rb, hl, s)
        if pending is not None:
            finish(pending)
        pending = ((rb, hl), p, denom)
    finish(pending)


def _attn(qkv, tiles, masks):
    bsz, L, n3 = qkv.shape
    d = n3 // 3
    rows = L // GRID_W
    kh = min(WIN_H_MAX, rows)
    assert rows % ATT_ROWS == 0 and rows // ATT_ROWS >= 3 and ATT_ROWS == kh // 2
    assert ATT_WIN % 2 == 0 and kh + ATT_ROWS - 1 <= ATT_WIN <= rows
    n_hp = d // (2 * HEAD_DIM)
    blk = pl.BlockSpec((1, L, 2 * HEAD_DIM), lambda b, hp: (b, 0, hp))
    return pl.pallas_call(
        functools.partial(_attn_kernel, rows=rows, kh=kh),
        grid=(bsz, n_hp),
        in_specs=[blk,
                  pl.BlockSpec((1, L, 2 * HEAD_DIM), lambda b, hp: (b, 0, n_hp + hp)),
                  pl.BlockSpec((1, L, 2 * HEAD_DIM), lambda b, hp: (b, 0, 2 * n_hp + hp)),
                  pl.BlockSpec((1,) + tiles.shape[1:], lambda b, hp: (hp, 0, 0, 0, 0)),
                  pl.BlockSpec(masks.shape, lambda b, hp: (0, 0, 0))],
        out_specs=blk,
        out_shape=jax.ShapeDtypeStruct((bsz, L, d), F32),
        compiler_params=_params(("arbitrary", "arbitrary")),
        name="attn",
    )(qkv, qkv, qkv, tiles, masks)


def _mix_kernel(ys_ref, wglu_ref, bglu_ref, gs_ref, ya_ref, ga_ref, wout_ref,
                x_ref, gt_ref, gf_ref, sh_ref, sc_ref, wr_ref,
                x1_ref, h2_ref, lg_ref):
    d_ssm = ys_ref.shape[3]
    y = ys_ref[:, 0].reshape(-1, d_ssm)
    z = jax.nn.gelu(y)
    gate = _sigmoid(jnp.dot(z.astype(BF16), wglu_ref[...], preferred_element_type=F32) + bglu_ref[...])
    a = _rms(z * gate, gs_ref[...]).astype(BF16)
    t = _rms(ya_ref[0], ga_ref[...]).astype(BF16)
    mixed = (jnp.dot(a, wout_ref[:d_ssm, :], preferred_element_type=F32)
             + jnp.dot(t, wout_ref[d_ssm:, :], preferred_element_type=F32))
    x1 = x_ref[0] + gt_ref[0] * mixed
    x1_ref[0] = x1
    h2 = _rms(x1, gf_ref[...]) * (1.0 + sc_ref[0]) + sh_ref[0]
    h2_hi = h2.astype(BF16)
    h2_ref[0] = h2
    h2_lo = (h2 - h2_hi.astype(F32)).astype(BF16)
    wr = wr_ref[...]
    wr_hi = wr.astype(BF16)
    wr_lo = (wr - wr_hi.astype(F32)).astype(BF16)
    n_e = wr.shape[1]
    tm = h2.shape[0]
    r = jnp.dot(jnp.concatenate([h2_hi, h2_lo], axis=0), jnp.concatenate([wr_hi, wr_lo], axis=1),
                preferred_element_type=F32)
    lg_ref[0] = r[:tm, :n_e] + r[:tm, n_e:] + r[tm:, :n_e]


def _mix(ys, wglu_bf, b_glu, g_ssm, ya, g_attn, wout_bf, x, gt1, g_ffn, sh2, sc2, w_router):
    bsz, L, D = x.shape
    d_ssm = ys.shape[3]
    E = w_router.shape[1]
    tm = 256
    T = SSM_T
    tile = lambda n: pl.BlockSpec((1, tm, n), lambda b, i: (b, i, 0))
    per_b = lambda n: pl.BlockSpec((1, 1, n), lambda b, i: (b, 0, 0))
    full = lambda r, n: pl.BlockSpec((r, n), lambda b, i: (0, 0))
    return pl.pallas_call(
        _mix_kernel,
        grid=(bsz, L // tm),
        in_specs=[pl.BlockSpec((tm // T, 1, T, d_ssm), lambda b, i: (i, b, 0, 0)),
                  full(d_ssm, d_ssm), full(1, d_ssm), full(1, d_ssm),
                  tile(d_ssm), full(1, d_ssm), full(D, D),
                  tile(D), per_b(D), full(1, D), per_b(D), per_b(D), full(D, E)],
        out_specs=[tile(D), tile(D), tile(E)],
        out_shape=[jax.ShapeDtypeStruct((bsz, L, D), F32),
                   jax.ShapeDtypeStruct((bsz, L, D), F32),
                   jax.ShapeDtypeStruct((bsz, L, E), F32)],
        compiler_params=_params(("arbitrary", "arbitrary")),
        name="mix",
    )(ys, wglu_bf, b_glu, g_ssm, ya, g_attn, wout_bf, x, gt1, g_ffn, sh2, sc2, w_router)


TOPK_BISECTIONS = 160


def _topk_kernel(lg_ref, pos_ref, idx_ref, as_ref, posm_scr, aff_scr, *, cap):
    lg = lg_ref[...]
    bsz, E, L = lg.shape
    e = jnp.exp(lg - jnp.max(lg, axis=1, keepdims=True))
    aff3 = e / jnp.sum(e, axis=1, keepdims=True)
    aff = aff3.reshape(bsz * E, L)

    def count(mask):
        return jnp.sum(jnp.where(mask, 1.0, 0.0), axis=-1, keepdims=True)

    def midpoint(lo, hi):
        return lo + (hi - lo) * 0.5

    def unresolved(carry):
        lo, hi, it = carry
        mid = midpoint(lo, hi)
        open_rows = jnp.max(jnp.where((mid > lo) & (mid < hi), 1.0, 0.0))
        return jnp.logical_and(it < TOPK_BISECTIONS, open_rows > 0.0)

    def halve(carry):
        lo, hi, it = carry
        mid = midpoint(lo, hi)
        ok = count(aff >= mid) >= cap
        return jnp.where(ok, mid, lo), jnp.where(ok, hi, mid), it + 1

    lo0 = jnp.zeros((bsz * E, 1), F32)
    thr, _, _ = lax.while_loop(unresolved, halve, (lo0, lo0 + 2.0, jnp.int32(0)))
    gt = aff > thr
    eq = aff == thr
    need = cap - count(gt)
    blk = LANES
    tri = jnp.where(lax.broadcasted_iota(I32, (blk, blk), 0) < lax.broadcasted_iota(I32, (blk, blk), 1),
                    1.0, 0.0).astype(BF16)

    def prefix_count(mask):
        ones = jnp.where(mask, 1.0, 0.0)
        run = jnp.zeros((bsz * E, 1), F32)
        outs = []
        for j in range(L // blk):
            piece = ones[:, j * blk:(j + 1) * blk]
            outs.append(jnp.dot(piece.astype(BF16), tri, preferred_element_type=F32) + run)
            run = run + jnp.sum(piece, axis=-1, keepdims=True)
        return jnp.concatenate(outs, axis=-1)

    sel = gt | (eq & (prefix_count(eq) < need))
    posm = jnp.where(sel, prefix_count(sel), -1.0)
    pos_ref[...] = posm.astype(I32).reshape(bsz, E, L)

    posm_scr[...] = posm
    aff_scr[...] = aff
    idx_ref[...] = jnp.zeros_like(idx_ref)
    as_ref[...] = jnp.zeros_like(as_ref)
    col = lax.broadcasted_iota(I32, idx_ref.shape, 1)

    def row(r, carry):
        slot = lax.broadcasted_iota(I32, (cap, L), 0).astype(F32)
        token = lax.broadcasted_iota(I32, (cap, L), 1).astype(F32)
        hit = posm_scr[pl.ds(r, 1), :] == slot
        tok = jnp.sum(jnp.where(hit, token, 0.0), axis=-1, keepdims=True)
        a = jnp.sum(jnp.where(hit, aff_scr[pl.ds(r, 1), :], 0.0), axis=-1, keepdims=True)
        idx_ref[...] = jnp.where(col == r, tok.astype(I32), idx_ref[...])
        as_ref[...] = jnp.where(col == r, a, as_ref[...])
        return carry

    lax.fori_loop(0, bsz * E, row, 0)


def _topk(lg_t, cap):
    bsz, E, L = lg_t.shape
    assert bsz * E <= LANES
    spec = pl.BlockSpec((bsz, E, L), lambda i: (0, 0, 0))
    cols = pl.BlockSpec((cap, LANES), lambda i: (0, 0))
    return pl.pallas_call(
        functools.partial(_topk_kernel, cap=cap),
        grid=(1,),
        in_specs=[spec],
        out_specs=[spec, cols, cols],
        out_shape=[jax.ShapeDtypeStruct((bsz, E, L), I32), jax.ShapeDtypeStruct((cap, LANES), I32),
                   jax.ShapeDtypeStruct((cap, LANES), F32)],
        scratch_shapes=[pltpu.VMEM((bsz * E, L), F32), pltpu.VMEM((bsz * E, L), F32)],
        compiler_params=_params(("arbitrary",)),
        name="topk",
    )(lg_t)


def _ffn_kernel(rows_ref, h_hbm, wg_ref, wu_ref, wd_ref, as_ref, ye_ref, land_ref, xe_ref, acc_ref, sem, *, n_f):
    e, f = pl.program_id(0), pl.program_id(1)
    n_rows = xe_ref.shape[0]
    n_land = land_ref.shape[0]
    per_step = n_land // n_f

    def row_copy(src_row, dst_row):
        return pltpu.make_async_copy(h_hbm.at[src_row], land_ref.at[dst_row], sem)

    def wait_gather():
        pltpu.make_async_copy(h_hbm.at[pl.ds(0, n_land)], land_ref, sem).wait()

    @pl.when(f == 0)
    def _():
        @pl.when(e == 0)
        def _():
            def one(i, carry):
                row_copy(rows_ref[i], i).start()
                return carry
            lax.fori_loop(0, n_land, one, 0, unroll=8)

        wait_gather()
        xe_ref[...] = land_ref[:n_rows, :].astype(BF16)
        acc_ref[...] = jnp.zeros_like(acc_ref)

    for i in range(per_step):
        dst = f * per_step + i
        row_copy(rows_ref[(e + 1) * n_land + dst], dst).start(priority=i % 2)

    x = xe_ref[...]
    g = jnp.dot(x, wg_ref[0].astype(BF16), preferred_element_type=F32)
    up = jnp.dot(x, wu_ref[0].astype(BF16), preferred_element_type=F32)
    hid = (g * _sigmoid(g) * up).astype(BF16)
    acc_ref[...] += jnp.dot(hid, wd_ref[0].astype(BF16), preferred_element_type=F32)

    @pl.when(f == n_f - 1)
    def _():
        bsz, _, cap, _ = ye_ref.shape
        y = (acc_ref[...] * as_ref[0]).astype(BF16)
        for b in range(bsz):
            ye_ref[b, 0] = y[b * cap:(b + 1) * cap]

        @pl.when(e == pl.num_programs(0) - 1)
        def _():
            wait_gather()


def _ffn(rows, h_tok, w_gate, w_up, w_down, aff_slot, bsz):
    E, R = aff_slot.shape[:2]
    D = h_tok.shape[1]
    F = w_gate.shape[2]
    cap = R // bsz
    tf = 256
    n_f = F // tf
    n_land = pl.cdiv(pl.cdiv(R, n_f), 8) * 8 * n_f
    rows = jnp.pad(rows, ((0, 1), (0, n_land - R))).reshape(-1)
    grid_spec = pltpu.PrefetchScalarGridSpec(
        num_scalar_prefetch=1,
        grid=(E, n_f),
        in_specs=[pl.BlockSpec(memory_space=pl.ANY),
                  pl.BlockSpec((1, D, tf), lambda e, f, rows: (e, 0, f)),
                  pl.BlockSpec((1, D, tf), lambda e, f, rows: (e, 0, f)),
                  pl.BlockSpec((1, tf, D), lambda e, f, rows: (e, f, 0)),
                  pl.BlockSpec((1, R, 1), lambda e, f, rows: (e, 0, 0))],
        out_specs=pl.BlockSpec((bsz, 1, cap, D), lambda e, f, rows: (0, e, 0, 0)),
        scratch_shapes=[pltpu.VMEM((n_land, D), F32), pltpu.VMEM((R, D), BF16), pltpu.VMEM((R, D), F32),
                        pltpu.SemaphoreType.DMA(())],
    )
    return pl.pallas_call(
        functools.partial(_ffn_kernel, n_f=n_f),
        grid_spec=grid_spec,
        out_shape=jax.ShapeDtypeStruct((bsz, E, cap, D), BF16),
        compiler_params=_params(("arbitrary", "arbitrary")),
        name="ffn",
    )(rows, h_tok, w_gate, w_up, w_down, aff_slot)


def _combine_kernel(pos_ref, ye_ref, x1_ref, gt_ref, g_ref, o_ref, *, final_norm):
    pos_t = pos_ref[0]
    n_e, cap, d = ye_ref.shape[1:]
    slot = lax.broadcasted_iota(I32, (pos_t.shape[0], cap), 1)
    onehot = jnp.concatenate([jnp.where(pos_t[:, e:e + 1] == slot, 1.0, 0.0).astype(BF16) for e in range(n_e)],
                             axis=-1)
    moe = jnp.dot(onehot, ye_ref[0].reshape(n_e * cap, d), preferred_element_type=F32)
    o = x1_ref[0] + gt_ref[0] * moe
    o_ref[0] = _rms(o, g_ref[...]) if final_norm else o


def _combine(pos_t, ye, x1, gt2, g_final, final_norm):
    bsz, L, E = pos_t.shape
    D = x1.shape[2]
    cap = ye.shape[2]
    tl = 512
    return pl.pallas_call(
        functools.partial(_combine_kernel, final_norm=final_norm),
        grid=(bsz, L // tl),
        in_specs=[pl.BlockSpec((1, tl, E), lambda b, i: (b, i, 0)),
                  pl.BlockSpec((1, E, cap, D), lambda b, i: (b, 0, 0, 0), pipeline_mode=pl.Buffered(1)),
                  pl.BlockSpec((1, tl, D), lambda b, i: (b, i, 0)),
                  pl.BlockSpec((1, 1, D), lambda b, i: (b, 0, 0)),
                  pl.BlockSpec((1, D), lambda b, i: (0, 0))],
        out_specs=pl.BlockSpec((1, tl, D), lambda b, i: (b, i, 0)),
        out_shape=jax.ShapeDtypeStruct((bsz, L, D), F32),
        compiler_params=_params(("arbitrary", "arbitrary")),
        name="combine",
    )(pos_t, ye, x1, gt2, g_final)


def kernel(x, c, w_ada, b_ada, g_mix, w_in, ssm_a_re, ssm_a_im, ssm_log_dt, ssm_b_re, ssm_b_im,
           ssm_c_re, ssm_c_im, ssm_d, w_glu, b_glu, rpb, g_ssm_out, g_attn_out, w_out, g_ffn,
           w_router, w_gate, w_up, w_down, g_final):
    bsz, L, D = x.shape
    depth = w_ada.shape[0]
    d_ssm = ssm_d.shape[1]
    E = w_router.shape[2]
    cap = CAPACITY_FACTOR * L // E
    rows = L // GRID_W
    c8 = jnp.zeros((8, D), F32).at[:bsz].set(c)
    row = lambda v: v.reshape(1, -1)
    for layer in range(depth):
        mod = _ada(c8, w_ada[layer], row(b_ada[layer]))[:bsz]
        sh1, sc1, gt1, sh2, sc2, gt2 = [m.reshape(bsz, 1, D) for m in jnp.split(mod, 6, axis=-1)]

        u, qkv = _inproj(x, sh1, sc1, row(g_mix[layer]), _cast_bf16(w_in[layer]), d_ssm)
        ops = _ssm_operators(ssm_a_re[layer], ssm_a_im[layer], ssm_log_dt[layer], ssm_b_re[layer],
                             ssm_b_im[layer], ssm_c_re[layer], ssm_c_im[layer])
        y_ssm = _ssm(u.reshape(-1, d_ssm), row(ssm_d[layer]), ops, bsz).reshape(u.shape)
        y_attn = _attn(qkv, *_attn_tables(rpb[layer], rows))
        x1, h2, logits = _mix(y_ssm, _cast_bf16(w_glu[layer]), row(b_glu[layer]),
                              row(g_ssm_out[layer]), y_attn, row(g_attn_out[layer]), _cast_bf16(w_out[layer]),
                              x, gt1, row(g_ffn[layer]), sh2, sc2, w_router[layer])

        pos, idx_cols, aff_cols = _topk(jnp.swapaxes(logits, 1, 2), cap)
        per_expert = lambda cols: jnp.transpose(cols[:, :bsz * E].reshape(cap, bsz, E), (2, 1, 0))
        rows = (per_expert(idx_cols) + (jnp.arange(bsz, dtype=I32) * L)[None, :, None]).reshape(E, bsz * cap)
        aff_slot = per_expert(aff_cols).reshape(E, bsz * cap, 1)
        ye = _ffn(rows, h2.reshape(bsz * L, D), w_gate[layer], w_up[layer], w_down[layer], aff_slot, bsz)
        x = _combine(jnp.swapaxes(pos, 1, 2), ye, x1, gt2, row(g_final), layer == depth - 1)
    return x
```

```python
import functools
import math

import jax
import jax.numpy as jnp
from jax import lax
from jax.experimental import pallas as pl
from jax.experimental.pallas import tpu as pltpu

F32 = jnp.float32
BF16 = jnp.bfloat16
I32 = jnp.int32

EPS = 1e-6
GRID_W = 64
SSM_GROUP = 16
SSM_STATE = 64
N_HEADS = 16
HEAD_DIM = 64
WIN_H_MAX = 8
WIN_W = 16
N_EXPERTS = 16
CAPACITY_FACTOR = 2
SSM_T = 16
LANES = 128
NEG = -1e30
LOG2E = math.log2(math.e)

ATT_ROWS = 4
ATT_WIN = 12
VMEM_LIMIT = 56 * 1024 * 1024


def _params(sem, vmem=VMEM_LIMIT):
    return pltpu.CompilerParams(dimension_semantics=sem, vmem_limit_bytes=vmem)


def _sigmoid(x):
    return 1.0 / (1.0 + jnp.exp(-x))


def _rms(x, g):
    return x * lax.rsqrt(jnp.mean(x * x, axis=-1, keepdims=True) + EPS) * g


def _ada_kernel(c_ref, w_ref, b_ref, o_ref):
    c = c_ref[...]
    ca = (c * _sigmoid(c)).astype(BF16)
    o_ref[...] = jnp.dot(ca, w_ref[...].astype(BF16), preferred_element_type=F32) + b_ref[...]


def _ada(c8, w, b):
    k, n = w.shape
    tn = 1024
    return pl.pallas_call(
        _ada_kernel,
        grid=(n // tn,),
        in_specs=[pl.BlockSpec((8, k), lambda j: (0, 0)),
                  pl.BlockSpec((k, tn), lambda j: (0, j)),
                  pl.BlockSpec((1, tn), lambda j: (0, j))],
        out_specs=pl.BlockSpec((8, tn), lambda j: (0, j)),
        out_shape=jax.ShapeDtypeStruct((8, n), F32),
        compiler_params=_params(("arbitrary",)),
        name="ada",
    )(c8, w, b)


def _cast_kernel(w_ref, o_ref):
    o_ref[...] = w_ref[...].astype(BF16)


def _cast_bf16(w):
    k, n = w.shape
    tk = 256
    return pl.pallas_call(
        _cast_kernel,
        grid=(k // tk,),
        in_specs=[pl.BlockSpec((tk, n), lambda i: (i, 0))],
        out_specs=pl.BlockSpec((tk, n), lambda i: (i, 0)),
        out_shape=jax.ShapeDtypeStruct((k, n), BF16),
        compiler_params=_params(("arbitrary",)),
        name="cast",
    )(w)


def _inproj_kernel(x_ref, sh_ref, sc_ref, g_ref, w_ref, u_ref, qkv_ref, *, d_ssm):
    h = (_rms(x_ref[0], g_ref[...]) * (1.0 + sc_ref[0]) + sh_ref[0]).astype(BF16)
    n_total = w_ref.shape[1]
    for n in range(n_total // d_ssm):
        r = jnp.dot(h, w_ref[:, n * d_ssm:(n + 1) * d_ssm], preferred_element_type=F32)
        if n == 0:
            for cc in range(u_ref.shape[0]):
                u_ref[cc, 0] = r[cc * SSM_T:(cc + 1) * SSM_T]
        else:
            if n == 1:
                r = r * (HEAD_DIM ** -0.5 * LOG2E)
            qkv_ref[0, :, (n - 1) * d_ssm:n * d_ssm] = r.astype(BF16)


def _inproj(x, sh, sc, g, w_bf, d_ssm):
    bsz, L, D = x.shape
    n = w_bf.shape[1]
    tm = 512
    T = SSM_T
    return pl.pallas_call(
        functools.partial(_inproj_kernel, d_ssm=d_ssm),
        grid=(bsz, L // tm),
        in_specs=[pl.BlockSpec((1, tm, D), lambda b, i: (b, i, 0)),
                  pl.BlockSpec((1, 1, D), lambda b, i: (b, 0, 0)),
                  pl.BlockSpec((1, 1, D), lambda b, i: (b, 0, 0)),
                  pl.BlockSpec((1, D), lambda b, i: (0, 0)),
                  pl.BlockSpec((D, n), lambda b, i: (0, 0), pipeline_mode=pl.Buffered(1))],
        out_specs=[pl.BlockSpec((tm // T, 1, T, d_ssm), lambda b, i: (i, b, 0, 0)),
                   pl.BlockSpec((1, tm, n - d_ssm), lambda b, i: (b, i, 0))],
        out_shape=[jax.ShapeDtypeStruct((L // T, bsz, T, d_ssm), F32),
                   jax.ShapeDtypeStruct((bsz, L, n - d_ssm), BF16)],
        compiler_params=_params(("arbitrary", "arbitrary")),
        name="inproj",
    )(x, sh, sc, g, w_bf)


def _ssm_operators(a_re, a_im, log_dt, b_re, b_im, c_re, c_im):
    T = SSM_T
    G, P = a_re.shape[1], a_re.shape[2]
    H = b_re.shape[3]
    dt = jnp.exp(log_dt)[..., None]
    mag = jnp.exp(a_re * dt)
    lbr = mag * jnp.cos(a_im * dt)
    lbi = mag * jnp.sin(a_im * dt)
    den = a_re * a_re + a_im * a_im
    nr = lbr - 1.0
    fr = ((nr * a_re + lbi * a_im) / den)[:, :, None, :]
    fi = ((lbi * a_re - nr * a_im) / den)[:, :, None, :]
    b_re_t, b_im_t = jnp.swapaxes(b_re, 2, 3), jnp.swapaxes(b_im, 2, 3)
    bbr = fr * b_re_t - fi * b_im_t
    bbi = fr * b_im_t + fi * b_re_t
    prs, pis = [jnp.ones_like(lbr)], [jnp.zeros_like(lbi)]
    for _ in range(T):
        pr, pi = prs[-1], pis[-1]
        prs.append(pr * lbr - pi * lbi)
        pis.append(pr * lbi + pi * lbr)
    pr = jnp.stack(prs, axis=2)
    pi = jnp.stack(pis, axis=2)
    assert H == T
    rev = lambda x: x[:, ::-1]
    sections = [bbr[0], bbi[0], bbr[1], bbi[1], c_re[0], c_im[0], c_re[1], c_im[1],
                rev(pr[0, :, :T]), rev(pi[0, :, :T]), pr[1, :, :T], pi[1, :, :T],
                pr[0, :, 1:], pi[0, :, 1:], rev(pr[1, :, 1:]), rev(pi[1, :, 1:]),
                pr[0, :, :T], pi[0, :, :T]]
    m, wpair, vtpair = _ssm_prep(jnp.concatenate(sections, axis=1))
    a4 = jnp.stack([pr[0, :, T], pi[0, :, T], pr[1, :, T], pi[1, :, T]])
    a_chunk = jnp.transpose(a4.reshape(4, G // 2, 2, P), (1, 0, 2, 3)).reshape(1, G // 2 * 8 * P)
    return m, wpair, vtpair, a_chunk


def _nt_dot(a, b):
    return lax.dot_general(a, b, (((1,), (1,)), ((), ())), preferred_element_type=F32)


def _ssm_prep_kernel(pk_ref, m_ref, w_ref, vt_ref, mats_ref, rows_ref, kmats_ref, krows_ref):
    T = SSM_T
    H = T
    kw = T * H
    half = kmats_ref.shape[2] // 2
    lane = lax.broadcasted_iota(I32, (1, 2 * half), 1)
    cat = lambda xs: jnp.concatenate(xs, axis=-1)

    def group(g, carry):
        (bbr_f, bbi_f, bbr_b, bbi_b, cre_f, cim_f, cre_b, cim_b, qrf, qif, qrb, qib,
         rrf, rif, rrb, rib, krf, kif) = [pk_ref[g, i * T:(i + 1) * T, :] for i in range(18)]
        zero = jnp.zeros_like(bbr_f)
        even = g % 2 == 0

        def slots(v4):
            ev = cat([v4[0], zero, v4[1], zero, v4[2], zero, v4[3], zero])
            od = cat([zero, v4[0], zero, v4[1], zero, v4[2], zero, v4[3]])
            return jnp.where(even, ev, od)

        mats_ref[0] = slots([bbr_f, bbi_f, bbr_b, bbi_b])
        mats_ref[1] = slots([-bbi_f, bbr_f, -bbi_b, bbr_b])
        mats_ref[2] = slots([cre_f, -cim_f, cre_b, -cim_b])
        mats_ref[3] = slots([-cim_f, -cre_f, -cim_b, -cre_b])
        rows_ref[0] = cat([qrf] * 4 + [qrb] * 4)
        rows_ref[1] = cat([qif] * 4 + [qib] * 4)
        rows_ref[2] = cat([rrf] * 4 + [rrb] * 4)
        rows_ref[3] = cat([rif] * 4 + [rib] * 4)
        kmats_ref[0] = cat([bbr_f, -bbi_f, bbr_b, -bbi_b])
        kmats_ref[1] = cat([cre_f, cre_f, cre_b, cre_b])
        kmats_ref[2] = cat([-cim_f, cim_f, -cim_b, cim_b])
        krows_ref[0] = cat([krf, kif, qrb, qib])
        krows_ref[1] = cat([kif, krf, qib, qrb])

        pair, row0 = g // 2, (g % 2) * kw
        aw, bw, av, bv = [mats_ref[i] for i in range(4)]
        for s in range(T):
            r = pl.ds(pl.multiple_of(row0 + s * H, H), H)
            w_ref[pair, r, :] = (aw * rows_ref[0, pl.ds(s, 1), :] + bw * rows_ref[1, pl.ds(s, 1), :]).astype(BF16)
            vt_ref[pair, r, :] = (av * rows_ref[2, pl.ds(s, 1), :] + bv * rows_ref[3, pl.ds(s, 1), :]).astype(BF16)
        lhs, a4, b4 = [kmats_ref[i] for i in range(3)]
        blocks = []
        for i in range(2 * T):
            k = abs(i - (T - 1))
            if i == 2 * T - 1:
                blocks.append(jnp.zeros_like(a4))
                continue
            cp = a4 * krows_ref[0, pl.ds(k, 1), :] + b4 * krows_ref[1, pl.ds(k, 1), :]
            if i < T - 1:
                cp = jnp.where(lane >= half, cp, 0.0)
            elif i > T - 1:
                cp = jnp.where(lane < half, cp, 0.0)
            blocks.append(cp)
        rhs = jnp.concatenate(blocks, axis=0)
        lhs_hi = lhs.astype(BF16)
        lhs_lo = (lhs - lhs_hi.astype(F32)).astype(BF16)
        rhs_hi = rhs.astype(BF16)
        rhs_lo = (rhs - rhs_hi.astype(F32)).astype(BF16)
        z = _nt_dot(lhs_hi, rhs_hi) + _nt_dot(lhs_lo, rhs_hi) + _nt_dot(lhs_hi, rhs_lo)
        for s in range(T):
            off = (T - 1 - s) * H
            m_ref[g, s * H:(s + 1) * H, :] = z[:, off:off + kw].astype(BF16)
        return carry

    lax.fori_loop(0, pk_ref.shape[0], group, 0)


def _ssm_prep(packed):
    G, n_rows, P = packed.shape
    T = H = SSM_T
    gs = LANES // SSM_GROUP
    return pl.pallas_call(
        _ssm_prep_kernel,
        grid=(G // gs,),
        in_specs=[pl.BlockSpec((gs, n_rows, P), lambda j: (j, 0, 0))],
        out_specs=[pl.BlockSpec((gs, T * H, T * H), lambda j: (j, 0, 0)),
                   pl.BlockSpec((gs // 2, 2 * T * H, 8 * P), lambda j: (j, 0, 0)),
                   pl.BlockSpec((gs // 2, 2 * T * H, 8 * P), lambda j: (j, 0, 0))],
        out_shape=[jax.ShapeDtypeStruct((G, T * H, T * H), BF16),
                   jax.ShapeDtypeStruct((G // 2, 2 * T * H, 8 * P), BF16),
                   jax.ShapeDtypeStruct((G // 2, 2 * T * H, 8 * P), BF16)],
        scratch_shapes=[pltpu.VMEM((4, H, 8 * P), F32), pltpu.VMEM((4, T, 8 * P), F32),
                        pltpu.VMEM((3, H, 4 * P), F32), pltpu.VMEM((2, T, 4 * P), F32)],
        compiler_params=_params(("arbitrary",)),
        name="ssm_prep",
    )(packed)


def _slab_permutation():
    idx = jnp.arange(8 * LANES)
    t, g, h = idx // LANES, (idx % LANES) // SSM_GROUP, idx % SSM_GROUP
    dst = g * LANES + t * SSM_GROUP + h
    return (dst[:, None] == jnp.arange(8 * LANES)[None, :]).astype(BF16)


def _chunk_scan(s_ref, a_ref, *, bsz, n_pairs):
    n_it = s_ref.shape[0] // (2 * bsz)
    cols = [[pl.ds((4 * j + q) * LANES, LANES) for q in range(4)] for j in range(n_pairs)]
    decay = [[a_ref[:, c] for c in cols[j]] for j in range(n_pairs)]

    def step(i, carry):
        rf = pl.ds(pl.multiple_of(i * 2 * bsz, 2 * bsz), 2 * bsz)
        rb = pl.ds(pl.multiple_of((n_it - 1 - i) * 2 * bsz, 2 * bsz), 2 * bsz)
        out = []
        for j in range(n_pairs):
            ar_f, ai_f, ar_b, ai_b = decay[j]
            xr, xi, yr, yi = carry[4 * j:4 * j + 4]
            sr, si = s_ref[rf, cols[j][0]], s_ref[rf, cols[j][1]]
            xr1 = ar_f * xr - ai_f * xi + sr[:bsz]
            xi1 = ar_f * xi + ai_f * xr + si[:bsz]
            s_ref[rf, cols[j][0]] = jnp.concatenate([xr, xr1], axis=0)
            s_ref[rf, cols[j][1]] = jnp.concatenate([xi, xi1], axis=0)
            xr2 = ar_f * xr1 - ai_f * xi1 + sr[bsz:]
            xi2 = ar_f * xi1 + ai_f * xr1 + si[bsz:]
            tr, ti = s_ref[rb, cols[j][2]], s_ref[rb, cols[j][3]]
            yr1 = ar_b * yr - ai_b * yi + tr[bsz:]
            yi1 = ar_b * yi + ai_b * yr + ti[bsz:]
            s_ref[rb, cols[j][2]] = jnp.concatenate([yr1, yr], axis=0)
            s_ref[rb, cols[j][3]] = jnp.concatenate([yi1, yi], axis=0)
            yr2 = ar_b * yr1 - ai_b * yi1 + tr[:bsz]
            yi2 = ar_b * yi1 + ai_b * yr1 + ti[:bsz]
            out += [xr2, xi2, yr2, yi2]
        return tuple(out)

    z = jnp.zeros((bsz, LANES), F32)
    lax.fori_loop(0, n_it, step, (z,) * (4 * n_pairs))


def _ssm_kernel(u_ref, d_ref, perm_ref, iperm_ref, w_ref, m_ref, v_ref, a_ref, y_ref,
                ug_ref, s_ref, yg_ref, *, bsz):
    T = SSM_T
    R = u_ref.shape[0] // T
    n_groups = LANES // SSM_GROUP
    kw = T * SSM_GROUP
    for tq in range(T // 8):
        cat = jnp.concatenate([u_ref[pl.ds(tq * 8 + t, R, stride=T), :].astype(BF16) for t in range(8)], axis=-1)
        grouped = jnp.dot(cat, perm_ref[...], preferred_element_type=F32).astype(BF16)
        for g in range(n_groups):
            ug_ref[g, :, tq * LANES:(tq + 1) * LANES] = grouped[:, g * LANES:(g + 1) * LANES]
    for pr in range(n_groups // 2):
        u2 = jnp.concatenate([ug_ref[2 * pr], ug_ref[2 * pr + 1]], axis=-1)
        s_ref[:, pr * 4 * LANES:(pr + 1) * 4 * LANES] = jnp.dot(u2, w_ref[pr], preferred_element_type=F32)
    _chunk_scan(s_ref, a_ref, bsz=bsz, n_pairs=n_groups // 2)
    for pr in range(n_groups // 2):
        inter = _nt_dot(s_ref[:, pr * 4 * LANES:(pr + 1) * 4 * LANES].astype(BF16), v_ref[pr])
        for gl in range(2):
            g = 2 * pr + gl
            yg_ref[g] = (jnp.dot(ug_ref[g], m_ref[g], preferred_element_type=F32)
                         + inter[:, gl * kw:(gl + 1) * kw])
    for tq in range(T // 8):
        ycat = jnp.concatenate([yg_ref[g, :, tq * LANES:(tq + 1) * LANES] for g in range(n_groups)], axis=-1)
        y_hi = ycat.astype(BF16)
        y_lo = (ycat - y_hi.astype(F32)).astype(BF16)
        back = (jnp.dot(y_hi, iperm_ref[...], preferred_element_type=F32)
                + jnp.dot(y_lo, iperm_ref[...], preferred_element_type=F32))
        for t in range(8):
            rows = pl.ds(tq * 8 + t, R, stride=T)
            y_ref[rows, :] = back[:, t * LANES:(t + 1) * LANES] + d_ref[...] * u_ref[rows, :]


def _ssm(u_tok, d_skip, ops, bsz):
    m, wpair, vpair, a_chunk = ops
    N, d_ssm = u_tok.shape
    T = SSM_T
    R = N // T
    n_groups = LANES // SSM_GROUP
    perm = _slab_permutation()
    const = lambda shape: pl.BlockSpec(shape, lambda j: (0,) * len(shape))
    return pl.pallas_call(
        functools.partial(_ssm_kernel, bsz=bsz),
        grid=(d_ssm // LANES,),
        in_specs=[pl.BlockSpec((N, LANES), lambda j: (0, j)),
                  pl.BlockSpec((1, LANES), lambda j: (0, j)),
                  const(perm.shape), const(perm.shape),
                  pl.BlockSpec((n_groups // 2,) + wpair.shape[1:], lambda j: (j, 0, 0)),
                  pl.BlockSpec((n_groups,) + m.shape[1:], lambda j: (j, 0, 0)),
                  pl.BlockSpec((n_groups // 2,) + vpair.shape[1:], lambda j: (j, 0, 0)),
                  pl.BlockSpec((1, n_groups // 2 * 4 * LANES), lambda j: (0, j))],
        out_specs=pl.BlockSpec((N, LANES), lambda j: (0, j)),
        out_shape=jax.ShapeDtypeStruct((N, d_ssm), F32),
        scratch_shapes=[pltpu.VMEM((n_groups, R, T * SSM_GROUP), BF16),
                        pltpu.VMEM((R, n_groups // 2 * 4 * LANES), F32),
                        pltpu.VMEM((n_groups, R, T * SSM_GROUP), F32)],
        compiler_params=_params(("arbitrary",)),
        name="ssm",
    )(u_tok, d_skip, perm, perm.T, wpair, m, vpair, a_chunk)


def _attn_tables(rpb, rows):
    kh = min(WIN_H_MAX, rows)
    H = rpb.shape[0]
    w = jnp.arange(GRID_W)
    col_start = jnp.clip(w - WIN_W // 2, 0, GRID_W - WIN_W)
    col_mask = (w[None, :] >= col_start[:, None]) & (w[None, :] < col_start[:, None] + WIN_W)
    dc = jnp.clip(w[None, :] - w[:, None] + (WIN_W - 1), 0, 2 * WIN_W - 2)
    col_sel = (dc[:, :, None] == jnp.arange(2 * WIN_W - 1)[None, None, :]).astype(F32)
    tile = jnp.einsum('hdc,qkc->hdqk', rpb, col_sel, precision=lax.Precision.HIGHEST)
    tile = jnp.where(col_mask[None, None], tile * LOG2E, NEG)
    lo_pad, hi_pad = ATT_ROWS, ATT_WIN - kh + 1
    tile = jnp.pad(tile, ((0, 0), (lo_pad, hi_pad), (0, 0), (0, 0)))
    n_dr = tile.shape[1] - 1
    tiles = jnp.concatenate([tile[:, :-1], tile[:, 1:]], axis=-1).reshape(H // 2, 2, n_dr, GRID_W, 2 * GRID_W)
    i = jnp.arange(ATT_ROWS)
    last_union = rows - ATT_WIN
    last_rel = (rows - kh) - last_union
    rel = jnp.stack([jnp.zeros_like(i), i, jnp.full_like(i, last_rel)])
    j = jnp.arange(ATT_WIN)
    valid = (j[None, None, :] >= rel[:, :, None]) & (j[None, None, :] < rel[:, :, None] + kh)
    masks = jnp.where(valid, 0.0, NEG).astype(F32)
    masks = jnp.repeat(masks.reshape(3 * ATT_ROWS * ATT_WIN // 2, 2), GRID_W, axis=-1)
    return tiles, masks.reshape(-1, 1, 2 * GRID_W)


def _attn_kernel(q_ref, k_ref, v_ref, t_ref, m_ref, o_ref, *, rows, kh):
    n_blocks = rows // ATT_ROWS
    nq = ATT_ROWS * GRID_W
    nk = ATT_WIN * GRID_W
    n_pairs = ATT_WIN // 2
    pw = 2 * GRID_W
    lane = lax.broadcasted_iota(I32, (1, 2 * HEAD_DIM), 1)
    head_lanes = [lane < HEAD_DIM, lane >= HEAD_DIM]
    rel = [[0] * ATT_ROWS, list(range(ATT_ROWS)), [(rows - kh) - (rows - ATT_WIN)] * ATT_ROWS]

    def geometry(rb):
        pattern = 0 if rb == 0 else (2 if rb == n_blocks - 1 else 1)
        union = min(max(rb * ATT_ROWS - kh // 2, 0), rows - ATT_WIN)
        dr0 = union - rb * ATT_ROWS + (WIN_H_MAX - 1) + ATT_ROWS
        return pattern, dr0, pl.ds(rb * nq, nq), pl.ds(union * GRID_W, nk)

    def scores(rb, hl):
        _, _, q_rows, k_rows = geometry(rb)
        qm = jnp.where(head_lanes[hl], q_ref[0, q_rows, :], 0.0).astype(BF16)
        return _nt_dot(qm, k_ref[0, k_rows, :])

    def softmax(rb, hl, s):
        pattern, dr0, _, _ = geometry(rb)
        probs, denoms = [], []
        for i in range(ATT_ROWS):
            first, last = rel[pattern][i], rel[pattern][i] + kh - 1
            pairs = range(first // 2, last // 2 + 1)
            bias = []
            for jp in pairs:
                tile = t_ref[0, hl, dr0 + 2 * jp - i]
                if 2 * jp < first or 2 * jp + 1 > last:
                    tile = tile + m_ref[(pattern * ATT_ROWS + i) * n_pairs + jp]
                bias.append(tile)
            si = s[i * GRID_W:(i + 1) * GRID_W, pairs[0] * pw:(pairs[-1] + 1) * pw] + jnp.concatenate(bias, axis=-1)
            pi = jnp.exp2(si - jnp.max(si, axis=-1, keepdims=True))
            denoms.append(jnp.sum(pi, axis=-1, keepdims=True))
            pieces = [pi.astype(BF16)]
            if pairs[0] > 0:
                pieces.insert(0, jnp.zeros((GRID_W, pairs[0] * pw), BF16))
            if pairs[-1] + 1 < n_pairs:
                pieces.append(jnp.zeros((GRID_W, (n_pairs - 1 - pairs[-1]) * pw), BF16))
            probs.append(jnp.concatenate(pieces, axis=-1))
        return jnp.concatenate(probs, axis=0), jnp.concatenate(denoms, axis=0)

    def weighted_values(rb, hl, p, denom):
        _, _, _, k_rows = geometry(rb)
        vm = jnp.where(head_lanes[hl], v_ref[0, k_rows, :], 0.0).astype(BF16)
        return jnp.dot(p, vm, preferred_element_type=F32) / denom

    passes = [(rb, hl) for rb in range(n_blocks) for hl in range(2)]
    ahead = 1
    queue = [scores(*passes[k]) for k in range(ahead)]
    pending = None
    partial = {}

    def finish(item):
        (rb, hl), p, denom = item
        o = weighted_values(rb, hl, p, denom)
        if hl == 0:
            partial[rb] = o
        else:
            o_ref[0, geometry(rb)[2], :] = partial.pop(rb) + o

    for n, (rb, hl) in enumerate(passes):
        s = queue.pop(0)
        if n + ahead < len(passes):
            queue.append(scores(*passes[n + ahead]))
        p, denom = softmax(rb, hl, s)
        if pending is not None:
            finish(pending)
        pending = ((rb, hl), p, denom)
    finish(pending)


def _attn(qkv, tiles, masks):
    bsz, L, n3 = qkv.shape
    d = n3 // 3
    rows = L // GRID_W
    kh = min(WIN_H_MAX, rows)
    assert rows % ATT_ROWS == 0 and rows // ATT_ROWS >= 3 and ATT_ROWS == kh // 2
    assert ATT_WIN % 2 == 0 and kh + ATT_ROWS - 1 <= ATT_WIN <= rows
    n_hp = d // (2 * HEAD_DIM)
    blk = pl.BlockSpec((1, L, 2 * HEAD_DIM), lambda b, hp: (b, 0, hp))
    return pl.pallas_call(
        functools.partial(_attn_kernel, rows=rows, kh=kh),
        grid=(bsz, n_hp),
        in_specs=[blk,
                  pl.BlockSpec((1, L, 2 * HEAD_DIM), lambda b, hp: (b, 0, n_hp + hp)),
                  pl.BlockSpec((1, L, 2 * HEAD_DIM), lambda b, hp: (b, 0, 2 * n_hp + hp)),
                  pl.BlockSpec((1,) + tiles.shape[1:], lambda b, hp: (hp, 0, 0, 0, 0)),
                  pl.BlockSpec(masks.shape, lambda b, hp: (0, 0, 0))],
        out_specs=blk,
        out_shape=jax.ShapeDtypeStruct((bsz, L, d), F32),
        compiler_params=_params(("arbitrary", "arbitrary")),
        name="attn",
    )(qkv, qkv, qkv, tiles, masks)


def _mix_kernel(ys_ref, wglu_ref, bglu_ref, gs_ref, ya_ref, ga_ref, wout_ref,
                x_ref, gt_ref, gf_ref, sh_ref, sc_ref, wr_ref,
                x1_ref, h2_ref, lg_ref):
    d_ssm = ys_ref.shape[3]
    y = ys_ref[:, 0].reshape(-1, d_ssm)
    z = jax.nn.gelu(y)
    gate = _sigmoid(jnp.dot(z.astype(BF16), wglu_ref[...], preferred_element_type=F32) + bglu_ref[...])
    a = _rms(z * gate, gs_ref[...]).astype(BF16)
    t = _rms(ya_ref[0], ga_ref[...]).astype(BF16)
    mixed = (jnp.dot(a, wout_ref[:d_ssm, :], preferred_element_type=F32)
             + jnp.dot(t, wout_ref[d_ssm:, :], preferred_element_type=F32))
    x1 = x_ref[0] + gt_ref[0] * mixed
    x1_ref[0] = x1
    h2 = _rms(x1, gf_ref[...]) * (1.0 + sc_ref[0]) + sh_ref[0]
    h2_hi = h2.astype(BF16)
    h2_ref[0] = h2_hi
    h2_lo = (h2 - h2_hi.astype(F32)).astype(BF16)
    wr = wr_ref[...]
    wr_hi = wr.astype(BF16)
    wr_lo = (wr - wr_hi.astype(F32)).astype(BF16)
    n_e = wr.shape[1]
    tm = h2.shape[0]
    r = jnp.dot(jnp.concatenate([h2_hi, h2_lo], axis=0), jnp.concatenate([wr_hi, wr_lo], axis=1),
                preferred_element_type=F32)
    lg_ref[0] = r[:tm, :n_e] + r[:tm, n_e:] + r[tm:, :n_e]


def _mix(ys, wglu_bf, b_glu, g_ssm, ya, g_attn, wout_bf, x, gt1, g_ffn, sh2, sc2, w_router):
    bsz, L, D = x.shape
    d_ssm = ys.shape[3]
    E = w_router.shape[1]
    tm = 512
    T = SSM_T
    tile = lambda n: pl.BlockSpec((1, tm, n), lambda b, i: (b, i, 0))
    per_b = lambda n: pl.BlockSpec((1, 1, n), lambda b, i: (b, 0, 0))
    full = lambda r, n: pl.BlockSpec((r, n), lambda b, i: (0, 0), pipeline_mode=pl.Buffered(1))
    return pl.pallas_call(
        _mix_kernel,
        grid=(bsz, L // tm),
        in_specs=[pl.BlockSpec((tm // T, 1, T, d_ssm), lambda b, i: (i, b, 0, 0)),
                  full(d_ssm, d_ssm), full(1, d_ssm), full(1, d_ssm),
                  tile(d_ssm), full(1, d_ssm), full(D, D),
                  tile(D), per_b(D), full(1, D), per_b(D), per_b(D), full(D, E)],
        out_specs=[tile(D), tile(D), tile(E)],
        out_shape=[jax.ShapeDtypeStruct((bsz, L, D), F32),
                   jax.ShapeDtypeStruct((bsz, L, D), BF16),
                   jax.ShapeDtypeStruct((bsz, L, E), F32)],
        compiler_params=_params(("arbitrary", "arbitrary")),
        name="mix",
    )(ys, wglu_bf, b_glu, g_ssm, ya, g_attn, wout_bf, x, gt1, g_ffn, sh2, sc2, w_router)


TOPK_BISECTIONS = 160


def _topk_kernel(lg_ref, aff_ref, pos_ref, *, cap):
    lg = lg_ref[...]
    bsz, E, L = lg.shape
    e = jnp.exp(lg - jnp.max(lg, axis=1, keepdims=True))
    aff3 = e / jnp.sum(e, axis=1, keepdims=True)
    aff_ref[...] = aff3
    aff = aff3.reshape(bsz * E, L)

    def count(mask):
        return jnp.sum(jnp.where(mask, 1.0, 0.0), axis=-1, keepdims=True)

    def midpoint(lo, hi):
        return lo + (hi - lo) * 0.5

    def unresolved(carry):
        lo, hi, it = carry
        mid = midpoint(lo, hi)
        open_rows = jnp.max(jnp.where((mid > lo) & (mid < hi), 1.0, 0.0))
        return jnp.logical_and(it < TOPK_BISECTIONS, open_rows > 0.0)

    def halve(carry):
        lo, hi, it = carry
        mid = midpoint(lo, hi)
        ok = count(aff >= mid) >= cap
        return jnp.where(ok, mid, lo), jnp.where(ok, hi, mid), it + 1

    lo0 = jnp.zeros((bsz * E, 1), F32)
    thr, _, _ = lax.while_loop(unresolved, halve, (lo0, lo0 + 2.0, jnp.int32(0)))
    gt = aff > thr
    eq = aff == thr
    need = cap - count(gt)
    blk = LANES
    tri = jnp.where(lax.broadcasted_iota(I32, (blk, blk), 0) < lax.broadcasted_iota(I32, (blk, blk), 1),
                    1.0, 0.0).astype(BF16)

    def prefix_count(mask):
        ones = jnp.where(mask, 1.0, 0.0)
        run = jnp.zeros((bsz * E, 1), F32)
        outs = []
        for j in range(L // blk):
            piece = ones[:, j * blk:(j + 1) * blk]
            outs.append(jnp.dot(piece.astype(BF16), tri, preferred_element_type=F32) + run)
            run = run + jnp.sum(piece, axis=-1, keepdims=True)
        return jnp.concatenate(outs, axis=-1)

    sel = gt | (eq & (prefix_count(eq) < need))
    pos = prefix_count(sel)
    pos_ref[...] = jnp.where(sel, pos.astype(I32), -1).reshape(bsz, E, L)


def _topk(lg_t, cap):
    bsz, E, L = lg_t.shape
    spec = pl.BlockSpec((bsz, E, L), lambda i: (0, 0, 0))
    return pl.pallas_call(
        functools.partial(_topk_kernel, cap=cap),
        grid=(1,),
        in_specs=[spec],
        out_specs=[spec, spec],
        out_shape=[jax.ShapeDtypeStruct((bsz, E, L), F32), jax.ShapeDtypeStruct((bsz, E, L), I32)],
        compiler_params=_params(("arbitrary",)),
        name="topk",
    )(lg_t)


def _gather_kernel(pos_ref, aff_ref, h_ref, xe_ref, as_ref, *, cap):
    L = h_ref.shape[1]
    n_e = pos_ref.shape[1]
    slot = lax.broadcasted_iota(I32, (cap, L), 0)
    hits = [pos_ref[0, e] == slot for e in range(n_e)]
    onehot = jnp.concatenate([jnp.where(h, 1.0, 0.0).astype(BF16) for h in hits], axis=0)
    xe = jnp.dot(onehot, h_ref[0], preferred_element_type=F32).astype(BF16)
    for e in range(n_e):
        xe_ref[e] = xe[e * cap:(e + 1) * cap]
        as_ref[e] = jnp.sum(jnp.where(hits[e], aff_ref[0, e], 0.0), axis=-1, keepdims=True)


def _gather(pos, aff, h2, cap):
    bsz, E, L = pos.shape
    D = h2.shape[2]
    pos4 = pos.reshape(bsz, E, 1, L)
    aff4 = aff.reshape(bsz, E, 1, L)
    ge = 4
    return pl.pallas_call(
        functools.partial(_gather_kernel, cap=cap),
        grid=(bsz, E // ge),
        in_specs=[pl.BlockSpec((1, ge, 1, L), lambda b, e: (b, e, 0, 0)),
                  pl.BlockSpec((1, ge, 1, L), lambda b, e: (b, e, 0, 0)),
                  pl.BlockSpec((1, L, D), lambda b, e: (b, 0, 0))],
        out_specs=[pl.BlockSpec((ge, cap, D), lambda b, e: (e, b, 0)),
                   pl.BlockSpec((ge, cap, 1), lambda b, e: (e, b, 0))],
        out_shape=[jax.ShapeDtypeStruct((E, bsz * cap, D), BF16),
                   jax.ShapeDtypeStruct((E, bsz * cap, 1), F32)],
        compiler_params=_params(("arbitrary", "arbitrary")),
        name="gather",
    )(pos4, aff4, h2)


def _ffn_kernel(xe_ref, wg_ref, wu_ref, wd_ref, as_ref, ye_ref, acc_ref):
    f = pl.program_id(1)

    @pl.when(f == 0)
    def _():
        acc_ref[...] = jnp.zeros_like(acc_ref)

    x = xe_ref[0]
    g = jnp.dot(x, wg_ref[0].astype(BF16), preferred_element_type=F32)
    up = jnp.dot(x, wu_ref[0].astype(BF16), preferred_element_type=F32)
    hid = (g * _sigmoid(g) * up).astype(BF16)
    acc_ref[...] += jnp.dot(hid, wd_ref[0].astype(BF16), preferred_element_type=F32)

    @pl.when(f == pl.num_programs(1) - 1)
    def _():
        bsz, _, cap, _ = ye_ref.shape
        y = (acc_ref[...] * as_ref[0]).astype(BF16)
        for b in range(bsz):
            ye_ref[b, 0] = y[b * cap:(b + 1) * cap]


def _ffn(xe, w_gate, w_up, w_down, aff_slot, bsz):
    E, R, D = xe.shape
    F = w_gate.shape[2]
    cap = R // bsz
    tf = 256
    return pl.pallas_call(
        _ffn_kernel,
        grid=(E, F // tf),
        in_specs=[pl.BlockSpec((1, R, D), lambda e, f: (e, 0, 0)),
                  pl.BlockSpec((1, D, tf), lambda e, f: (e, 0, f)),
                  pl.BlockSpec((1, D, tf), lambda e, f: (e, 0, f)),
                  pl.BlockSpec((1, tf, D), lambda e, f: (e, f, 0)),
                  pl.BlockSpec((1, R, 1), lambda e, f: (e, 0, 0))],
        out_specs=pl.BlockSpec((bsz, 1, cap, D), lambda e, f: (0, e, 0, 0)),
        out_shape=jax.ShapeDtypeStruct((bsz, E, cap, D), BF16),
        scratch_shapes=[pltpu.VMEM((R, D), F32)],
        compiler_params=_params(("arbitrary", "arbitrary")),
        name="ffn",
    )(xe, w_gate, w_up, w_down, aff_slot)


def _combine_kernel(pos_ref, ye_ref, x1_ref, gt_ref, g_ref, o_ref, *, final_norm):
    pos_t = pos_ref[0]
    n_e, cap, d = ye_ref.shape[1:]
    slot = lax.broadcasted_iota(I32, (pos_t.shape[0], cap), 1)
    onehot = jnp.concatenate([jnp.where(pos_t[:, e:e + 1] == slot, 1.0, 0.0).astype(BF16) for e in range(n_e)],
                             axis=-1)
    moe = jnp.dot(onehot, ye_ref[0].reshape(n_e * cap, d), preferred_element_type=F32)
    o = x1_ref[0] + gt_ref[0] * moe
    o_ref[0] = _rms(o, g_ref[...]) if final_norm else o


def _combine(pos_t, ye, x1, gt2, g_final, final_norm):
    bsz, L, E = pos_t.shape
    D = x1.shape[2]
    cap = ye.shape[2]
    tl = 512
    return pl.pallas_call(
        functools.partial(_combine_kernel, final_norm=final_norm),
        grid=(bsz, L // tl),
        in_specs=[pl.BlockSpec((1, tl, E), lambda b, i: (b, i, 0)),
                  pl.BlockSpec((1, E, cap, D), lambda b, i: (b, 0, 0, 0), pipeline_mode=pl.Buffered(1)),
                  pl.BlockSpec((1, tl, D), lambda b, i: (b, i, 0)),
                  pl.BlockSpec((1, 1, D), lambda b, i: (b, 0, 0)),
                  pl.BlockSpec((1, D), lambda b, i: (0, 0))],
        out_specs=pl.BlockSpec((1, tl, D), lambda b, i: (b, i, 0)),
        out_shape=jax.ShapeDtypeStruct((bsz, L, D), F32),
        compiler_params=_params(("arbitrary", "arbitrary")),
        name="combine",
    )(pos_t, ye, x1, gt2, g_final)


def kernel(x, c, w_ada, b_ada, g_mix, w_in, ssm_a_re, ssm_a_im, ssm_log_dt, ssm_b_re, ssm_b_im,
           ssm_c_re, ssm_c_im, ssm_d, w_glu, b_glu, rpb, g_ssm_out, g_attn_out, w_out, g_ffn,
           w_router, w_gate, w_up, w_down, g_final):
    bsz, L, D = x.shape
    depth = w_ada.shape[0]
    d_ssm = ssm_d.shape[1]
    E = w_router.shape[2]
    cap = CAPACITY_FACTOR * L // E
    rows = L // GRID_W
    c8 = jnp.zeros((8, D), F32).at[:bsz].set(c)
    row = lambda v: v.reshape(1, -1)
    for layer in range(depth):
        mod = _ada(c8, w_ada[layer], row(b_ada[layer]))[:bsz]
        sh1, sc1, gt1, sh2, sc2, gt2 = [m.reshape(bsz, 1, D) for m in jnp.split(mod, 6, axis=-1)]

        u, qkv = _inproj(x, sh1, sc1, row(g_mix[layer]), _cast_bf16(w_in[layer]), d_ssm)
        ops = _ssm_operators(ssm_a_re[layer], ssm_a_im[layer], ssm_log_dt[layer], ssm_b_re[layer],
                             ssm_b_im[layer], ssm_c_re[layer], ssm_c_im[layer])
        y_ssm = _ssm(u.reshape(-1, d_ssm), row(ssm_d[layer]), ops, bsz).reshape(u.shape)
        y_attn = _attn(qkv, *_attn_tables(rpb[layer], rows))
        x1, h2, logits = _mix(y_ssm, _cast_bf16(w_glu[layer]), row(b_glu[layer]),
                              row(g_ssm_out[layer]), y_attn, row(g_attn_out[layer]), _cast_bf16(w_out[layer]),
                              x, gt1, row(g_ffn[layer]), sh2, sc2, w_router[layer])

        aff, pos = _topk(jnp.swapaxes(logits, 1, 2), cap)
        xe, aff_slot = _gather(pos, aff, h2, cap)
        ye = _ffn(xe, w_gate[layer], w_up[layer], w_down[layer], aff_slot, bsz)
        x = _combine(jnp.swapaxes(pos, 1, 2), ye, x1, gt2, row(g_final), layer == depth - 1)
    return x
```

```python
import functools
import math

import jax
import jax.numpy as jnp
from jax import lax
from jax.experimental import pallas as pl
from jax.experimental.pallas import tpu as pltpu

F32 = jnp.float32
BF16 = jnp.bfloat16
I32 = jnp.int32

EPS = 1e-6
GRID_W = 64
SSM_GROUP = 16
HEAD_DIM = 64
WIN_H_MAX = 8
WIN_W = 16
CAPACITY_FACTOR = 2
NEG = -1e30
LOG2E = math.log2(math.e)

SSM_T = 16
ATT_ROWS = 4
ATT_WIN = 12

LANES = 128
V7X_VMEM_BYTES = 64 * 1024 * 1024
VMEM_LIMIT = V7X_VMEM_BYTES // 8 * 7


def _params(sem, vmem=VMEM_LIMIT):
    return pltpu.CompilerParams(dimension_semantics=sem, vmem_limit_bytes=vmem)


def _sigmoid(x):
    return 1.0 / (1.0 + jnp.exp(-x))


def _rms(x, g):
    return x * lax.rsqrt(jnp.mean(x * x, axis=-1, keepdims=True) + EPS) * g


def _ada_kernel(c_ref, w_ref, b_ref, o_ref):
    c = c_ref[...]
    ca = (c * _sigmoid(c)).astype(BF16)
    o_ref[...] = jnp.dot(ca, w_ref[...].astype(BF16), preferred_element_type=F32) + b_ref[...]


def _ada(c8, w, b):
    k, n = w.shape
    tn = 2048
    return pl.pallas_call(
        _ada_kernel,
        grid=(n // tn,),
        in_specs=[pl.BlockSpec((8, k), lambda j: (0, 0)),
                  pl.BlockSpec((k, tn), lambda j: (0, j)),
                  pl.BlockSpec((1, tn), lambda j: (0, j))],
        out_specs=pl.BlockSpec((8, tn), lambda j: (0, j)),
        out_shape=jax.ShapeDtypeStruct((8, n), F32),
        compiler_params=_params(("arbitrary",)),
        name="ada",
    )(c8, w, b)


def _cast_kernel(w_ref, o_ref):
    o_ref[...] = w_ref[...].astype(BF16)


def _cast_bf16(w):
    k, n = w.shape
    tk = 256
    return pl.pallas_call(
        _cast_kernel,
        grid=(k // tk,),
        in_specs=[pl.BlockSpec((tk, n), lambda i: (i, 0))],
        out_specs=pl.BlockSpec((tk, n), lambda i: (i, 0)),
        out_shape=jax.ShapeDtypeStruct((k, n), BF16),
        compiler_params=_params(("arbitrary",)),
        name="cast",
    )(w)


def _inproj_kernel(x_ref, sh_ref, sc_ref, g_ref, w_ref, u_ref, qkv_ref, *, d_ssm):
    h = (_rms(x_ref[0], g_ref[...]) * (1.0 + sc_ref[0]) + sh_ref[0]).astype(BF16)
    n_total = w_ref.shape[1]
    for n in range(n_total // d_ssm):
        r = jnp.dot(h, w_ref[:, n * d_ssm:(n + 1) * d_ssm], preferred_element_type=F32)
        if n == 0:
            for cc in range(u_ref.shape[0]):
                u_ref[cc, 0] = r[cc * SSM_T:(cc + 1) * SSM_T]
        else:
            if n == 1:
                r = r * (HEAD_DIM ** -0.5 * LOG2E)
            qkv_ref[0, :, (n - 1) * d_ssm:n * d_ssm] = r.astype(BF16)


def _inproj(x, sh, sc, g, w_bf, d_ssm):
    bsz, L, D = x.shape
    n = w_bf.shape[1]
    tm = 512
    T = SSM_T
    return pl.pallas_call(
        functools.partial(_inproj_kernel, d_ssm=d_ssm),
        grid=(bsz, L // tm),
        in_specs=[pl.BlockSpec((1, tm, D), lambda b, i: (b, i, 0)),
                  pl.BlockSpec((1, 1, D), lambda b, i: (b, 0, 0)),
                  pl.BlockSpec((1, 1, D), lambda b, i: (b, 0, 0)),
                  pl.BlockSpec((1, D), lambda b, i: (0, 0)),
                  pl.BlockSpec((D, n), lambda b, i: (0, 0), pipeline_mode=pl.Buffered(1))],
        out_specs=[pl.BlockSpec((tm // T, 1, T, d_ssm), lambda b, i: (i, b, 0, 0)),
                   pl.BlockSpec((1, tm, n - d_ssm), lambda b, i: (b, i, 0))],
        out_shape=[jax.ShapeDtypeStruct((L // T, bsz, T, d_ssm), F32),
                   jax.ShapeDtypeStruct((bsz, L, n - d_ssm), BF16)],
        compiler_params=_params(("arbitrary", "arbitrary")),
        name="inproj",
    )(x, sh, sc, g, w_bf)


def _ssm_operators(a_re, a_im, log_dt, b_re, b_im, c_re, c_im):
    T = SSM_T
    G, P = a_re.shape[1], a_re.shape[2]
    H = b_re.shape[3]
    dt = jnp.exp(log_dt)[..., None]
    mag = jnp.exp(a_re * dt)
    lbr = mag * jnp.cos(a_im * dt)
    lbi = mag * jnp.sin(a_im * dt)
    den = a_re * a_re + a_im * a_im
    nr = lbr - 1.0
    fr = ((nr * a_re + lbi * a_im) / den)[:, :, None, :]
    fi = ((lbi * a_re - nr * a_im) / den)[:, :, None, :]
    b_re_t, b_im_t = jnp.swapaxes(b_re, 2, 3), jnp.swapaxes(b_im, 2, 3)
    bbr = fr * b_re_t - fi * b_im_t
    bbi = fr * b_im_t + fi * b_re_t
    prs, pis = [jnp.ones_like(lbr)], [jnp.zeros_like(lbi)]
    for _ in range(T):
        pr, pi = prs[-1], pis[-1]
        prs.append(pr * lbr - pi * lbi)
        pis.append(pr * lbi + pi * lbr)
    pr = jnp.stack(prs, axis=2)
    pi = jnp.stack(pis, axis=2)
    assert H == T
    rev = lambda x: x[:, ::-1]
    sections = [bbr[0], bbi[0], bbr[1], bbi[1], c_re[0], c_im[0], c_re[1], c_im[1],
                rev(pr[0, :, :T]), rev(pi[0, :, :T]), pr[1, :, :T], pi[1, :, :T],
                pr[0, :, 1:], pi[0, :, 1:], rev(pr[1, :, 1:]), rev(pi[1, :, 1:]),
                pr[0, :, :T], pi[0, :, :T]]
    m, wpair, vtpair = _ssm_prep(jnp.concatenate(sections, axis=1))
    a4 = jnp.stack([pr[0, :, T], pi[0, :, T], pr[1, :, T], pi[1, :, T]])
    a_chunk = jnp.transpose(a4.reshape(4, G // 2, 2, P), (1, 0, 2, 3)).reshape(1, G // 2 * 8 * P)
    return m, wpair, vtpair, a_chunk


def _nt_dot(a, b):
    return lax.dot_general(a, b, (((1,), (1,)), ((), ())), preferred_element_type=F32)


def _ssm_prep_kernel(pk_ref, m_ref, w_ref, vt_ref, mats_ref, rows_ref, kmats_ref, krows_ref):
    T = SSM_T
    H = T
    kw = T * H
    half = kmats_ref.shape[2] // 2
    lane = lax.broadcasted_iota(I32, (1, 2 * half), 1)
    cat = lambda xs: jnp.concatenate(xs, axis=-1)

    def group(g, carry):
        (bbr_f, bbi_f, bbr_b, bbi_b, cre_f, cim_f, cre_b, cim_b, qrf, qif, qrb, qib,
         rrf, rif, rrb, rib, krf, kif) = [pk_ref[g, i * T:(i + 1) * T, :] for i in range(18)]
        zero = jnp.zeros_like(bbr_f)
        even = g % 2 == 0

        def slots(v4):
            ev = cat([v4[0], zero, v4[1], zero, v4[2], zero, v4[3], zero])
            od = cat([zero, v4[0], zero, v4[1], zero, v4[2], zero, v4[3]])
            return jnp.where(even, ev, od)

        mats_ref[0] = slots([bbr_f, bbi_f, bbr_b, bbi_b])
        mats_ref[1] = slots([-bbi_f, bbr_f, -bbi_b, bbr_b])
        mats_ref[2] = slots([cre_f, -cim_f, cre_b, -cim_b])
        mats_ref[3] = slots([-cim_f, -cre_f, -cim_b, -cre_b])
        rows_ref[0] = cat([qrf] * 4 + [qrb] * 4)
        rows_ref[1] = cat([qif] * 4 + [qib] * 4)
        rows_ref[2] = cat([rrf] * 4 + [rrb] * 4)
        rows_ref[3] = cat([rif] * 4 + [rib] * 4)
        kmats_ref[0] = cat([bbr_f, -bbi_f, bbr_b, -bbi_b])
        kmats_ref[1] = cat([cre_f, cre_f, cre_b, cre_b])
        kmats_ref[2] = cat([-cim_f, cim_f, -cim_b, cim_b])
        krows_ref[0] = cat([krf, kif, qrb, qib])
        krows_ref[1] = cat([kif, krf, qib, qrb])

        pair, row0 = g // 2, (g % 2) * kw
        aw, bw, av, bv = [mats_ref[i] for i in range(4)]
        for s in range(T):
            r = pl.ds(pl.multiple_of(row0 + s * H, H), H)
            w_ref[pair, r, :] = (aw * rows_ref[0, pl.ds(s, 1), :] + bw * rows_ref[1, pl.ds(s, 1), :]).astype(BF16)
            vt_ref[pair, r, :] = (av * rows_ref[2, pl.ds(s, 1), :] + bv * rows_ref[3, pl.ds(s, 1), :]).astype(BF16)
        lhs, a4, b4 = [kmats_ref[i] for i in range(3)]
        blocks = []
        for i in range(2 * T):
            k = abs(i - (T - 1))
            if i == 2 * T - 1:
                blocks.append(jnp.zeros_like(a4))
                continue
            cp = a4 * krows_ref[0, pl.ds(k, 1), :] + b4 * krows_ref[1, pl.ds(k, 1), :]
            if i < T - 1:
                cp = jnp.where(lane >= half, cp, 0.0)
            elif i > T - 1:
                cp = jnp.where(lane < half, cp, 0.0)
            blocks.append(cp)
        rhs = jnp.concatenate(blocks, axis=0)
        lhs_hi = lhs.astype(BF16)
        lhs_lo = (lhs - lhs_hi.astype(F32)).astype(BF16)
        rhs_hi = rhs.astype(BF16)
        rhs_lo = (rhs - rhs_hi.astype(F32)).astype(BF16)
        z = _nt_dot(lhs_hi, rhs_hi) + _nt_dot(lhs_lo, rhs_hi) + _nt_dot(lhs_hi, rhs_lo)
        for s in range(T):
            off = (T - 1 - s) * H
            m_ref[g, s * H:(s + 1) * H, :] = z[:, off:off + kw].astype(BF16)
        return carry

    lax.fori_loop(0, pk_ref.shape[0], group, 0)


def _ssm_prep(packed):
    G, n_rows, P = packed.shape
    T = H = SSM_T
    gs = LANES // SSM_GROUP
    return pl.pallas_call(
        _ssm_prep_kernel,
        grid=(G // gs,),
        in_specs=[pl.BlockSpec((gs, n_rows, P), lambda j: (j, 0, 0))],
        out_specs=[pl.BlockSpec((gs, T * H, T * H), lambda j: (j, 0, 0)),
                   pl.BlockSpec((gs // 2, 2 * T * H, 8 * P), lambda j: (j, 0, 0)),
                   pl.BlockSpec((gs // 2, 2 * T * H, 8 * P), lambda j: (j, 0, 0))],
        out_shape=[jax.ShapeDtypeStruct((G, T * H, T * H), BF16),
                   jax.ShapeDtypeStruct((G // 2, 2 * T * H, 8 * P), BF16),
                   jax.ShapeDtypeStruct((G // 2, 2 * T * H, 8 * P), BF16)],
        scratch_shapes=[pltpu.VMEM((4, H, 8 * P), F32), pltpu.VMEM((4, T, 8 * P), F32),
                        pltpu.VMEM((3, H, 4 * P), F32), pltpu.VMEM((2, T, 4 * P), F32)],
        compiler_params=_params(("arbitrary",)),
        name="ssm_prep",
    )(packed)


def _slab_permutation():
    idx = jnp.arange(8 * LANES)
    t, g, h = idx // LANES, (idx % LANES) // SSM_GROUP, idx % SSM_GROUP
    dst = g * LANES + t * SSM_GROUP + h
    return (dst[:, None] == jnp.arange(8 * LANES)[None, :]).astype(BF16)


def _chunk_scan(s_ref, a_ref, *, bsz, n_pairs):
    n_it = s_ref.shape[0] // (2 * bsz)
    cols = [[pl.ds((4 * j + q) * LANES, LANES) for q in range(4)] for j in range(n_pairs)]
    decay = [[a_ref[:, c] for c in cols[j]] for j in range(n_pairs)]

    def step(i, carry):
        rf = pl.ds(pl.multiple_of(i * 2 * bsz, 2 * bsz), 2 * bsz)
        rb = pl.ds(pl.multiple_of((n_it - 1 - i) * 2 * bsz, 2 * bsz), 2 * bsz)
        out = []
        for j in range(n_pairs):
            ar_f, ai_f, ar_b, ai_b = decay[j]
            xr, xi, yr, yi = carry[4 * j:4 * j + 4]
            sr, si = s_ref[rf, cols[j][0]], s_ref[rf, cols[j][1]]
            xr1 = ar_f * xr - ai_f * xi + sr[:bsz]
            xi1 = ar_f * xi + ai_f * xr + si[:bsz]
            s_ref[rf, cols[j][0]] = jnp.concatenate([xr, xr1], axis=0)
            s_ref[rf, cols[j][1]] = jnp.concatenate([xi, xi1], axis=0)
            xr2 = ar_f * xr1 - ai_f * xi1 + sr[bsz:]
            xi2 = ar_f * xi1 + ai_f * xr1 + si[bsz:]
            tr, ti = s_ref[rb, cols[j][2]], s_ref[rb, cols[j][3]]
            yr1 = ar_b * yr - ai_b * yi + tr[bsz:]
            yi1 = ar_b * yi + ai_b * yr + ti[bsz:]
            s_ref[rb, cols[j][2]] = jnp.concatenate([yr1, yr], axis=0)
            s_ref[rb, cols[j][3]] = jnp.concatenate([yi1, yi], axis=0)
            yr2 = ar_b * yr1 - ai_b * yi1 + tr[:bsz]
            yi2 = ar_b * yi1 + ai_b * yr1 + ti[:bsz]
            out += [xr2, xi2, yr2, yi2]
        return tuple(out)

    z = jnp.zeros((bsz, LANES), F32)
    lax.fori_loop(0, n_it, step, (z,) * (4 * n_pairs), unroll=2)


def _ssm_kernel(u_ref, d_ref, perm_ref, iperm_ref, w_ref, m_ref, v_ref, a_ref, y_ref,
                ug_ref, s_ref, yg_ref, *, bsz):
    T = SSM_T
    R = u_ref.shape[0] // T
    n_groups = LANES // SSM_GROUP
    kw = T * SSM_GROUP
    for tq in range(T // 8):
        cat = jnp.concatenate([u_ref[pl.ds(tq * 8 + t, R, stride=T), :].astype(BF16) for t in range(8)], axis=-1)
        grouped = jnp.dot(cat, perm_ref[...], preferred_element_type=F32).astype(BF16)
        for g in range(n_groups):
            ug_ref[g, :, tq * LANES:(tq + 1) * LANES] = grouped[:, g * LANES:(g + 1) * LANES]
    for pr in range(n_groups // 2):
        u2 = jnp.concatenate([ug_ref[2 * pr], ug_ref[2 * pr + 1]], axis=-1)
        s_ref[:, pr * 4 * LANES:(pr + 1) * 4 * LANES] = jnp.dot(u2, w_ref[pr], preferred_element_type=F32)
    _chunk_scan(s_ref, a_ref, bsz=bsz, n_pairs=n_groups // 2)
    for pr in range(n_groups // 2):
        inter = _nt_dot(s_ref[:, pr * 4 * LANES:(pr + 1) * 4 * LANES].astype(BF16), v_ref[pr])
        for gl in range(2):
            g = 2 * pr + gl
            yg_ref[g] = (jnp.dot(ug_ref[g], m_ref[g], preferred_element_type=F32)
                         + inter[:, gl * kw:(gl + 1) * kw])
    for tq in range(T // 8):
        ycat = jnp.concatenate([yg_ref[g, :, tq * LANES:(tq + 1) * LANES] for g in range(n_groups)], axis=-1)
        y_hi = ycat.astype(BF16)
        y_lo = (ycat - y_hi.astype(F32)).astype(BF16)
        back = (jnp.dot(y_hi, iperm_ref[...], preferred_element_type=F32)
                + jnp.dot(y_lo, iperm_ref[...], preferred_element_type=F32))
        for t in range(8):
            rows = pl.ds(tq * 8 + t, R, stride=T)
            y_ref[rows, :] = back[:, t * LANES:(t + 1) * LANES] + d_ref[...] * u_ref[rows, :]


def _ssm(u_tok, d_skip, ops, bsz):
    m, wpair, vpair, a_chunk = ops
    N, d_ssm = u_tok.shape
    T = SSM_T
    R = N // T
    n_groups = LANES // SSM_GROUP
    perm = _slab_permutation()
    const = lambda shape: pl.BlockSpec(shape, lambda j: (0,) * len(shape))
    return pl.pallas_call(
        functools.partial(_ssm_kernel, bsz=bsz),
        grid=(d_ssm // LANES,),
        in_specs=[pl.BlockSpec((N, LANES), lambda j: (0, j)),
                  pl.BlockSpec((1, LANES), lambda j: (0, j)),
                  const(perm.shape), const(perm.shape),
                  pl.BlockSpec((n_groups // 2,) + wpair.shape[1:], lambda j: (j, 0, 0)),
                  pl.BlockSpec((n_groups,) + m.shape[1:], lambda j: (j, 0, 0)),
                  pl.BlockSpec((n_groups // 2,) + vpair.shape[1:], lambda j: (j, 0, 0)),
                  pl.BlockSpec((1, n_groups // 2 * 4 * LANES), lambda j: (0, j))],
        out_specs=pl.BlockSpec((N, LANES), lambda j: (0, j)),
        out_shape=jax.ShapeDtypeStruct((N, d_ssm), F32),
        scratch_shapes=[pltpu.VMEM((n_groups, R, T * SSM_GROUP), BF16),
                        pltpu.VMEM((R, n_groups // 2 * 4 * LANES), F32),
                        pltpu.VMEM((n_groups, R, T * SSM_GROUP), F32)],
        compiler_params=_params(("arbitrary",)),
        name="ssm",
    )(u_tok, d_skip, perm, perm.T, wpair, m, vpair, a_chunk)


def _attn_tables(rpb, rows):
    kh = min(WIN_H_MAX, rows)
    H = rpb.shape[0]
    w = jnp.arange(GRID_W)
    col_start = jnp.clip(w - WIN_W // 2, 0, GRID_W - WIN_W)
    col_mask = (w[None, :] >= col_start[:, None]) & (w[None, :] < col_start[:, None] + WIN_W)
    dc = jnp.clip(w[None, :] - w[:, None] + (WIN_W - 1), 0, 2 * WIN_W - 2)
    col_sel = (dc[:, :, None] == jnp.arange(2 * WIN_W - 1)[None, None, :]).astype(F32)
    tile = jnp.einsum('hdc,qkc->hdqk', rpb, col_sel, precision=lax.Precision.HIGHEST)
    tile = jnp.where(col_mask[None, None], tile * LOG2E, NEG)
    lo_pad, hi_pad = ATT_ROWS, ATT_WIN - kh + 1
    tile = jnp.pad(tile, ((0, 0), (lo_pad, hi_pad), (0, 0), (0, 0)))
    n_dr = tile.shape[1] - 1
    tiles = jnp.concatenate([tile[:, :-1], tile[:, 1:]], axis=-1).reshape(H // 2, 2, n_dr, GRID_W, 2 * GRID_W)
    i = jnp.arange(ATT_ROWS)
    last_union = rows - ATT_WIN
    last_rel = (rows - kh) - last_union
    rel = jnp.stack([jnp.zeros_like(i), i, jnp.full_like(i, last_rel)])
    j = jnp.arange(ATT_WIN)
    valid = (j[None, None, :] >= rel[:, :, None]) & (j[None, None, :] < rel[:, :, None] + kh)
    masks = jnp.where(valid, 0.0, NEG).astype(F32)
    masks = jnp.repeat(masks.reshape(3 * ATT_ROWS * ATT_WIN // 2, 2), GRID_W, axis=-1)
    return tiles, masks.reshape(-1, 1, 2 * GRID_W)


def _attn_kernel(q_ref, k_ref, v_ref, t_ref, m_ref, o_ref, *, rows, kh):
    n_blocks = rows // ATT_ROWS
    nq = ATT_ROWS * GRID_W
    nk = ATT_WIN * GRID_W
    n_pairs = ATT_WIN // 2
    pw = 2 * GRID_W
    lane = lax.broadcasted_iota(I32, (1, 2 * HEAD_DIM), 1)
    head_lanes = [lane < HEAD_DIM, lane >= HEAD_DIM]
    rel = [[0] * ATT_ROWS, list(range(ATT_ROWS)), [(rows - kh) - (rows - ATT_WIN)] * ATT_ROWS]

    def geometry(rb):
        pattern = 0 if rb == 0 else (2 if rb == n_blocks - 1 else 1)
        union = min(max(rb * ATT_ROWS - kh // 2, 0), rows - ATT_WIN)
        dr0 = union - rb * ATT_ROWS + (WIN_H_MAX - 1) + ATT_ROWS
        return pattern, dr0, pl.ds(rb * nq, nq), pl.ds(union * GRID_W, nk)

    def scores(rb, hl):
        _, _, q_rows, k_rows = geometry(rb)
        qm = jnp.where(head_lanes[hl], q_ref[0, q_rows, :], 0.0).astype(BF16)
        return _nt_dot(qm, k_ref[0, k_rows, :])

    def softmax(rb, hl, s):
        pattern, dr0, _, _ = geometry(rb)
        probs, denoms = [], []
        for i in range(ATT_ROWS):
            first, last = rel[pattern][i], rel[pattern][i] + kh - 1
            pairs = range(first // 2, last // 2 + 1)
            bias = []
            for jp in pairs:
                tile = t_ref[0, hl, dr0 + 2 * jp - i]
                if 2 * jp < first or 2 * jp + 1 > last:
                    tile = tile + m_ref[(pattern * ATT_ROWS + i) * n_pairs + jp]
                bias.append(tile)
            si = s[i * GRID_W:(i + 1) * GRID_W, pairs[0] * pw:(pairs[-1] + 1) * pw] + jnp.concatenate(bias, axis=-1)
            pi = jnp.exp2(si - jnp.max(si, axis=-1, keepdims=True))
            denoms.append(jnp.sum(pi, axis=-1, keepdims=True))
            pieces = [pi.astype(BF16)]
            if pairs[0] > 0:
                pieces.insert(0, jnp.zeros((GRID_W, pairs[0] * pw), BF16))
            if pairs[-1] + 1 < n_pairs:
                pieces.append(jnp.zeros((GRID_W, (n_pairs - 1 - pairs[-1]) * pw), BF16))
            probs.append(jnp.concatenate(pieces, axis=-1))
        return jnp.concatenate(probs, axis=0), jnp.concatenate(denoms, axis=0)

    def weighted_values(rb, hl, p, denom):
        _, _, _, k_rows = geometry(rb)
        vm = jnp.where(head_lanes[hl], v_ref[0, k_rows, :], 0.0).astype(BF16)
        return jnp.dot(p, vm, preferred_element_type=F32) / denom

    passes = [(rb, hl) for rb in range(n_blocks) for hl in range(2)]
    ahead = 1
    queue = [scores(*passes[k]) for k in range(ahead)]
    pending = None
    partial = {}

    def finish(item):
        (rb, hl), p, denom = item
        o = weighted_values(rb, hl, p, denom)
        if hl == 0:
            partial[rb] = o
        else:
            o_ref[0, geometry(rb)[2], :] = partial.pop(rb) + o

    for n, (rb, hl) in enumerate(passes):
        s = queue.pop(0)
        if n + ahead < len(passes):
            queue.append(scores(*passes[n + ahead]))
        p, denom = softmax(rb, hl, s)
        if pending is not None:
            finish(pending)
        pending = ((rb, hl), p, denom)
    finish(pending)


def _attn(qkv, tiles, masks):
    bsz, L, n3 = qkv.shape
    d = n3 // 3
    rows = L // GRID_W
    kh = min(WIN_H_MAX, rows)
    assert rows % ATT_ROWS == 0 and rows // ATT_ROWS >= 3 and ATT_ROWS == kh // 2
    assert ATT_WIN % 2 == 0 and kh + ATT_ROWS - 1 <= ATT_WIN <= rows
    n_hp = d // (2 * HEAD_DIM)
    blk = pl.BlockSpec((1, L, 2 * HEAD_DIM), lambda b, hp: (b, 0, hp))
    return pl.pallas_call(
        functools.partial(_attn_kernel, rows=rows, kh=kh),
        grid=(bsz, n_hp),
        in_specs=[blk,
                  pl.BlockSpec((1, L, 2 * HEAD_DIM), lambda b, hp: (b, 0, n_hp + hp)),
                  pl.BlockSpec((1, L, 2 * HEAD_DIM), lambda b, hp: (b, 0, 2 * n_hp + hp)),
                  pl.BlockSpec((1,) + tiles.shape[1:], lambda b, hp: (hp, 0, 0, 0, 0)),
                  pl.BlockSpec(masks.shape, lambda b, hp: (0, 0, 0))],
        out_specs=blk,
        out_shape=jax.ShapeDtypeStruct((bsz, L, d), F32),
        compiler_params=_params(("arbitrary", "arbitrary")),
        name="attn",
    )(qkv, qkv, qkv, tiles, masks)


def _mix_kernel(ys_ref, wglu_ref, bglu_ref, gs_ref, ya_ref, ga_ref, wout_ref,
                x_ref, gt_ref, gf_ref, sh_ref, sc_ref, wr_ref,
                x1_ref, h2_ref, lg_ref):
    d_ssm = ys_ref.shape[3]
    y = ys_ref[:, 0].reshape(-1, d_ssm)
    z = jax.nn.gelu(y)
    gate = _sigmoid(jnp.dot(z.astype(BF16), wglu_ref[...], preferred_element_type=F32) + bglu_ref[...])
    a = _rms(z * gate, gs_ref[...]).astype(BF16)
    t = _rms(ya_ref[0], ga_ref[...]).astype(BF16)
    mixed = (jnp.dot(a, wout_ref[:d_ssm, :], preferred_element_type=F32)
             + jnp.dot(t, wout_ref[d_ssm:, :], preferred_element_type=F32))
    x1 = x_ref[0] + gt_ref[0] * mixed
    x1_ref[0] = x1
    h2 = _rms(x1, gf_ref[...]) * (1.0 + sc_ref[0]) + sh_ref[0]
    h2_hi = h2.astype(BF16)
    h2_ref[0] = h2_hi
    h2_lo = (h2 - h2_hi.astype(F32)).astype(BF16)
    wr = wr_ref[...]
    wr_hi = wr.astype(BF16)
    wr_lo = (wr - wr_hi.astype(F32)).astype(BF16)
    n_e = wr.shape[1]
    tm = h2.shape[0]
    r = jnp.dot(jnp.concatenate([h2_hi, h2_lo], axis=0), jnp.concatenate([wr_hi, wr_lo], axis=1),
                preferred_element_type=F32)
    lg_ref[0] = r[:tm, :n_e] + r[:tm, n_e:] + r[tm:, :n_e]


def _mix(ys, wglu_bf, b_glu, g_ssm, ya, g_attn, wout_bf, x, gt1, g_ffn, sh2, sc2, w_router):
    bsz, L, D = x.shape
    d_ssm = ys.shape[3]
    E = w_router.shape[1]
    tm = 512
    T = SSM_T
    tile = lambda n: pl.BlockSpec((1, tm, n), lambda b, i: (b, i, 0))
    per_b = lambda n: pl.BlockSpec((1, 1, n), lambda b, i: (b, 0, 0))
    full = lambda r, n: pl.BlockSpec((r, n), lambda b, i: (0, 0), pipeline_mode=pl.Buffered(1))
    return pl.pallas_call(
        _mix_kernel,
        grid=(bsz, L // tm),
        in_specs=[pl.BlockSpec((tm // T, 1, T, d_ssm), lambda b, i: (i, b, 0, 0)),
                  full(d_ssm, d_ssm), full(1, d_ssm), full(1, d_ssm),
                  tile(d_ssm), full(1, d_ssm), full(D, D),
                  tile(D), per_b(D), full(1, D), per_b(D), per_b(D), full(D, E)],
        out_specs=[tile(D), tile(D), tile(E)],
        out_shape=[jax.ShapeDtypeStruct((bsz, L, D), F32),
                   jax.ShapeDtypeStruct((bsz, L, D), BF16),
                   jax.ShapeDtypeStruct((bsz, L, E), F32)],
        compiler_params=_params(("arbitrary", "arbitrary")),
        name="mix",
    )(ys, wglu_bf, b_glu, g_ssm, ya, g_attn, wout_bf, x, gt1, g_ffn, sh2, sc2, w_router)


TOPK_BISECTIONS = 160


def _topk_kernel(lg_ref, aff_ref, pos_ref, *, cap):
    lg = lg_ref[...]
    bsz, E, L = lg.shape
    e = jnp.exp(lg - jnp.max(lg, axis=1, keepdims=True))
    aff3 = e / jnp.sum(e, axis=1, keepdims=True)
    aff_ref[...] = aff3
    aff = aff3.reshape(bsz * E, L)

    def count(mask):
        return jnp.sum(jnp.where(mask, 1.0, 0.0), axis=-1, keepdims=True)

    def midpoint(lo, hi):
        return lo + (hi - lo) * 0.5

    def unresolved(carry):
        lo, hi, it = carry
        mid = midpoint(lo, hi)
        open_rows = jnp.max(jnp.where((mid > lo) & (mid < hi), 1.0, 0.0))
        return jnp.logical_and(it < TOPK_BISECTIONS, open_rows > 0.0)

    def halve(carry):
        lo, hi, it = carry
        mid = midpoint(lo, hi)
        ok = count(aff >= mid) >= cap
        return jnp.where(ok, mid, lo), jnp.where(ok, hi, mid), it + 1

    lo0 = jnp.zeros((bsz * E, 1), F32)
    thr, _, _ = lax.while_loop(unresolved, halve, (lo0, lo0 + 2.0, jnp.int32(0)))
    gt = aff > thr
    eq = aff == thr
    need = cap - count(gt)
    blk = LANES
    tri = jnp.where(lax.broadcasted_iota(I32, (blk, blk), 0) < lax.broadcasted_iota(I32, (blk, blk), 1),
                    1.0, 0.0).astype(BF16)

    def prefix_count(mask):
        ones = jnp.where(mask, 1.0, 0.0)
        run = jnp.zeros((bsz * E, 1), F32)
        outs = []
        for j in range(L // blk):
            piece = ones[:, j * blk:(j + 1) * blk]
            outs.append(jnp.dot(piece.astype(BF16), tri, preferred_element_type=F32) + run)
            run = run + jnp.sum(piece, axis=-1, keepdims=True)
        return jnp.concatenate(outs, axis=-1)

    sel = gt | (eq & (prefix_count(eq) < need))
    pos = prefix_count(sel)
    pos_ref[...] = jnp.where(sel, pos.astype(I32), -1).reshape(bsz, E, L)


def _topk(lg_t, cap):
    bsz, E, L = lg_t.shape
    spec = pl.BlockSpec((bsz, E, L), lambda i: (0, 0, 0))
    return pl.pallas_call(
        functools.partial(_topk_kernel, cap=cap),
        grid=(1,),
        in_specs=[spec],
        out_specs=[spec, spec],
        out_shape=[jax.ShapeDtypeStruct((bsz, E, L), F32), jax.ShapeDtypeStruct((bsz, E, L), I32)],
        compiler_params=_params(("arbitrary",)),
        name="topk",
    )(lg_t)


def _gather_kernel(pos_ref, aff_ref, h_ref, xe_ref, as_ref, *, cap):
    L = h_ref.shape[1]
    n_e = pos_ref.shape[1]
    slot = lax.broadcasted_iota(I32, (cap, L), 0)
    hits = [pos_ref[0, e] == slot for e in range(n_e)]
    onehot = jnp.concatenate([jnp.where(h, 1.0, 0.0).astype(BF16) for h in hits], axis=0)
    xe = jnp.dot(onehot, h_ref[0], preferred_element_type=F32).astype(BF16)
    for e in range(n_e):
        xe_ref[e] = xe[e * cap:(e + 1) * cap]
        as_ref[e] = jnp.sum(jnp.where(hits[e], aff_ref[0, e], 0.0), axis=-1, keepdims=True)


def _gather(pos, aff, h2, cap):
    bsz, E, L = pos.shape
    D = h2.shape[2]
    pos4 = pos.reshape(bsz, E, 1, L)
    aff4 = aff.reshape(bsz, E, 1, L)
    ge = 4
    return pl.pallas_call(
        functools.partial(_gather_kernel, cap=cap),
        grid=(bsz, E // ge),
        in_specs=[pl.BlockSpec((1, ge, 1, L), lambda b, e: (b, e, 0, 0)),
                  pl.BlockSpec((1, ge, 1, L), lambda b, e: (b, e, 0, 0)),
                  pl.BlockSpec((1, L, D), lambda b, e: (b, 0, 0))],
        out_specs=[pl.BlockSpec((ge, cap, D), lambda b, e: (e, b, 0)),
                   pl.BlockSpec((ge, cap, 1), lambda b, e: (e, b, 0))],
        out_shape=[jax.ShapeDtypeStruct((E, bsz * cap, D), BF16),
                   jax.ShapeDtypeStruct((E, bsz * cap, 1), F32)],
        compiler_params=_params(("arbitrary", "arbitrary")),
        name="gather",
    )(pos4, aff4, h2)


def _ffn_kernel(xe_ref, wg_ref, wu_ref, wd_ref, as_ref, ye_ref, acc_ref):
    f = pl.program_id(1)

    @pl.when(f == 0)
    def _():
        acc_ref[...] = jnp.zeros_like(acc_ref)

    x = xe_ref[0]
    g = jnp.dot(x, wg_ref[0].astype(BF16), preferred_element_type=F32)
    up = jnp.dot(x, wu_ref[0].astype(BF16), preferred_element_type=F32)
    hid = (g * _sigmoid(g) * up).astype(BF16)
    acc_ref[...] += jnp.dot(hid, wd_ref[0].astype(BF16), preferred_element_type=F32)

    @pl.when(f == pl.num_programs(1) - 1)
    def _():
        bsz, _, cap, _ = ye_ref.shape
        y = (acc_ref[...] * as_ref[0]).astype(BF16)
        for b in range(bsz):
            ye_ref[b, 0] = y[b * cap:(b + 1) * cap]


def _ffn(xe, w_gate, w_up, w_down, aff_slot, bsz):
    E, R, D = xe.shape
    F = w_gate.shape[2]
    cap = R // bsz
    tf = 256
    return pl.pallas_call(
        _ffn_kernel,
        grid=(E, F // tf),
        in_specs=[pl.BlockSpec((1, R, D), lambda e, f: (e, 0, 0)),
                  pl.BlockSpec((1, D, tf), lambda e, f: (e, 0, f)),
                  pl.BlockSpec((1, D, tf), lambda e, f: (e, 0, f)),
                  pl.BlockSpec((1, tf, D), lambda e, f: (e, f, 0)),
                  pl.BlockSpec((1, R, 1), lambda e, f: (e, 0, 0))],
        out_specs=pl.BlockSpec((bsz, 1, cap, D), lambda e, f: (0, e, 0, 0)),
        out_shape=jax.ShapeDtypeStruct((bsz, E, cap, D), BF16),
        scratch_shapes=[pltpu.VMEM((R, D), F32)],
        compiler_params=_params(("arbitrary", "arbitrary")),
        name="ffn",
    )(xe, w_gate, w_up, w_down, aff_slot)


def _combine_kernel(pos_ref, ye_ref, x1_ref, gt_ref, g_ref, o_ref, *, final_norm):
    pos_t = pos_ref[0]
    n_e, cap, d = ye_ref.shape[1:]
    slot = lax.broadcasted_iota(I32, (pos_t.shape[0], cap), 1)
    onehot = jnp.concatenate([jnp.where(pos_t[:, e:e + 1] == slot, 1.0, 0.0).astype(BF16) for e in range(n_e)],
                             axis=-1)
    moe = jnp.dot(onehot, ye_ref[0].reshape(n_e * cap, d), preferred_element_type=F32)
    o = x1_ref[0] + gt_ref[0] * moe
    o_ref[0] = _rms(o, g_ref[...]) if final_norm else o


def _combine(pos_t, ye, x1, gt2, g_final, final_norm):
    bsz, L, E = pos_t.shape
    D = x1.shape[2]
    cap = ye.shape[2]
    tl = 512
    return pl.pallas_call(
        functools.partial(_combine_kernel, final_norm=final_norm),
        grid=(bsz, L // tl),
        in_specs=[pl.BlockSpec((1, tl, E), lambda b, i: (b, i, 0)),
                  pl.BlockSpec((1, E, cap, D), lambda b, i: (b, 0, 0, 0), pipeline_mode=pl.Buffered(1)),
                  pl.BlockSpec((1, tl, D), lambda b, i: (b, i, 0)),
                  pl.BlockSpec((1, 1, D), lambda b, i: (b, 0, 0)),
                  pl.BlockSpec((1, D), lambda b, i: (0, 0))],
        out_specs=pl.BlockSpec((1, tl, D), lambda b, i: (b, i, 0)),
        out_shape=jax.ShapeDtypeStruct((bsz, L, D), F32),
        compiler_params=_params(("arbitrary", "arbitrary")),
        name="combine",
    )(pos_t, ye, x1, gt2, g_final)


def kernel(x, c, w_ada, b_ada, g_mix, w_in, ssm_a_re, ssm_a_im, ssm_log_dt, ssm_b_re, ssm_b_im,
           ssm_c_re, ssm_c_im, ssm_d, w_glu, b_glu, rpb, g_ssm_out, g_attn_out, w_out, g_ffn,
           w_router, w_gate, w_up, w_down, g_final):
    bsz, L, D = x.shape
    depth = w_ada.shape[0]
    d_ssm = ssm_d.shape[1]
    E = w_router.shape[2]
    cap = CAPACITY_FACTOR * L // E
    rows = L // GRID_W
    c8 = jnp.zeros((8, D), F32).at[:bsz].set(c)
    row = lambda v: v.reshape(1, -1)
    for layer in range(depth):
        mod = _ada(c8, w_ada[layer], row(b_ada[layer]))[:bsz]
        sh1, sc1, gt1, sh2, sc2, gt2 = [m.reshape(bsz, 1, D) for m in jnp.split(mod, 6, axis=-1)]

        u, qkv = _inproj(x, sh1, sc1, row(g_mix[layer]), _cast_bf16(w_in[layer]), d_ssm)
        ops = _ssm_operators(ssm_a_re[layer], ssm_a_im[layer], ssm_log_dt[layer], ssm_b_re[layer],
                             ssm_b_im[layer], ssm_c_re[layer], ssm_c_im[layer])
        y_ssm = _ssm(u.reshape(-1, d_ssm), row(ssm_d[layer]), ops, bsz).reshape(u.shape)
        y_attn = _attn(qkv, *_attn_tables(rpb[layer], rows))
        x1, h2, logits = _mix(y_ssm, _cast_bf16(w_glu[layer]), row(b_glu[layer]),
                              row(g_ssm_out[layer]), y_attn, row(g_attn_out[layer]), _cast_bf16(w_out[layer]),
                              x, gt1, row(g_ffn[layer]), sh2, sc2, w_router[layer])

        aff, pos = _topk(jnp.swapaxes(logits, 1, 2), cap)
        xe, aff_slot = _gather(pos, aff, h2, cap)
        ye = _ffn(xe, w_gate[layer], w_up[layer], w_down[layer], aff_slot, bsz)
        x = _combine(jnp.swapaxes(pos, 1, 2), ye, x1, gt2, row(g_final), layer == depth - 1)
    return x
```

```python
import functools
import math

import jax
import jax.numpy as jnp
from jax import lax
from jax.experimental import pallas as pl
from jax.experimental.pallas import tpu as pltpu

F32 = jnp.float32
BF16 = jnp.bfloat16
I32 = jnp.int32

EPS = 1e-6
GRID_W = 64
SSM_GROUP = 16
HEAD_DIM = 64
WIN_H_MAX = 8
WIN_W = 16
CAPACITY_FACTOR = 2
NEG = -1e30
LOG2E = math.log2(math.e)

SSM_T = 16
ATT_ROWS = 4
ATT_WIN = 12

LANES = 128
V7X_VMEM_BYTES = 64 * 1024 * 1024
VMEM_LIMIT = V7X_VMEM_BYTES // 8 * 7


def _params(sem, vmem=VMEM_LIMIT):
    return pltpu.CompilerParams(dimension_semantics=sem, vmem_limit_bytes=vmem)


def _sigmoid(x):
    return 1.0 / (1.0 + jnp.exp(-x))


def _rms(x, g):
    return x * lax.rsqrt(jnp.mean(x * x, axis=-1, keepdims=True) + EPS) * g


def _ada_kernel(c_ref, w_ref, b_ref, o_ref):
    c = c_ref[...]
    ca = (c * _sigmoid(c)).astype(BF16)
    o_ref[...] = jnp.dot(ca, w_ref[...].astype(BF16), preferred_element_type=F32) + b_ref[...]


def _ada(c8, w, b):
    k, n = w.shape
    tn = 1024
    return pl.pallas_call(
        _ada_kernel,
        grid=(n // tn,),
        in_specs=[pl.BlockSpec((8, k), lambda j: (0, 0)),
                  pl.BlockSpec((k, tn), lambda j: (0, j)),
                  pl.BlockSpec((1, tn), lambda j: (0, j))],
        out_specs=pl.BlockSpec((8, tn), lambda j: (0, j)),
        out_shape=jax.ShapeDtypeStruct((8, n), F32),
        compiler_params=_params(("arbitrary",)),
        name="ada",
    )(c8, w, b)


def _cast_kernel(w_ref, o_ref):
    o_ref[...] = w_ref[...].astype(BF16)


def _cast_bf16(w):
    k, n = w.shape
    tk = 256
    return pl.pallas_call(
        _cast_kernel,
        grid=(k // tk,),
        in_specs=[pl.BlockSpec((tk, n), lambda i: (i, 0))],
        out_specs=pl.BlockSpec((tk, n), lambda i: (i, 0)),
        out_shape=jax.ShapeDtypeStruct((k, n), BF16),
        compiler_params=_params(("arbitrary",)),
        name="cast",
    )(w)


def _inproj_kernel(x_ref, sh_ref, sc_ref, g_ref, w_ref, u_ref, qkv_ref, *, d_ssm):
    h = (_rms(x_ref[0], g_ref[...]) * (1.0 + sc_ref[0]) + sh_ref[0]).astype(BF16)
    n_total = w_ref.shape[1]
    for n in range(n_total // d_ssm):
        r = jnp.dot(h, w_ref[:, n * d_ssm:(n + 1) * d_ssm], preferred_element_type=F32)
        if n == 0:
            for cc in range(u_ref.shape[0]):
                u_ref[cc, 0] = r[cc * SSM_T:(cc + 1) * SSM_T]
        else:
            if n == 1:
                r = r * (HEAD_DIM ** -0.5 * LOG2E)
            qkv_ref[0, :, (n - 1) * d_ssm:n * d_ssm] = r.astype(BF16)


def _inproj(x, sh, sc, g, w_bf, d_ssm):
    bsz, L, D = x.shape
    n = w_bf.shape[1]
    tm = 512
    T = SSM_T
    return pl.pallas_call(
        functools.partial(_inproj_kernel, d_ssm=d_ssm),
        grid=(bsz, L // tm),
        in_specs=[pl.BlockSpec((1, tm, D), lambda b, i: (b, i, 0)),
                  pl.BlockSpec((1, 1, D), lambda b, i: (b, 0, 0)),
                  pl.BlockSpec((1, 1, D), lambda b, i: (b, 0, 0)),
                  pl.BlockSpec((1, D), lambda b, i: (0, 0)),
                  pl.BlockSpec((D, n), lambda b, i: (0, 0), pipeline_mode=pl.Buffered(1))],
        out_specs=[pl.BlockSpec((tm // T, 1, T, d_ssm), lambda b, i: (i, b, 0, 0)),
                   pl.BlockSpec((1, tm, n - d_ssm), lambda b, i: (b, i, 0))],
        out_shape=[jax.ShapeDtypeStruct((L // T, bsz, T, d_ssm), F32),
                   jax.ShapeDtypeStruct((bsz, L, n - d_ssm), BF16)],
        compiler_params=_params(("arbitrary", "arbitrary")),
        name="inproj",
    )(x, sh, sc, g, w_bf)


def _ssm_operators(a_re, a_im, log_dt, b_re, b_im, c_re, c_im):
    T = SSM_T
    G, P = a_re.shape[1], a_re.shape[2]
    H = b_re.shape[3]
    dt = jnp.exp(log_dt)[..., None]
    mag = jnp.exp(a_re * dt)
    lbr = mag * jnp.cos(a_im * dt)
    lbi = mag * jnp.sin(a_im * dt)
    den = a_re * a_re + a_im * a_im
    nr = lbr - 1.0
    fr = ((nr * a_re + lbi * a_im) / den)[:, :, None, :]
    fi = ((lbi * a_re - nr * a_im) / den)[:, :, None, :]
    b_re_t, b_im_t = jnp.swapaxes(b_re, 2, 3), jnp.swapaxes(b_im, 2, 3)
    bbr = fr * b_re_t - fi * b_im_t
    bbi = fr * b_im_t + fi * b_re_t
    prs, pis = [jnp.ones_like(lbr)], [jnp.zeros_like(lbi)]
    for _ in range(T):
        pr, pi = prs[-1], pis[-1]
        prs.append(pr * lbr - pi * lbi)
        pis.append(pr * lbi + pi * lbr)
    pr = jnp.stack(prs, axis=2)
    pi = jnp.stack(pis, axis=2)
    assert H == T
    rev = lambda x: x[:, ::-1]
    sections = [bbr[0], bbi[0], bbr[1], bbi[1], c_re[0], c_im[0], c_re[1], c_im[1],
                rev(pr[0, :, :T]), rev(pi[0, :, :T]), pr[1, :, :T], pi[1, :, :T],
                pr[0, :, 1:], pi[0, :, 1:], rev(pr[1, :, 1:]), rev(pi[1, :, 1:]),
                pr[0, :, :T], pi[0, :, :T]]
    m, wpair, vtpair = _ssm_prep(jnp.concatenate(sections, axis=1))
    a4 = jnp.stack([pr[0, :, T], pi[0, :, T], pr[1, :, T], pi[1, :, T]])
    a_chunk = jnp.transpose(a4.reshape(4, G // 2, 2, P), (1, 0, 2, 3)).reshape(1, G // 2 * 8 * P)
    return m, wpair, vtpair, a_chunk


def _nt_dot(a, b):
    return lax.dot_general(a, b, (((1,), (1,)), ((), ())), preferred_element_type=F32)


def _ssm_prep_kernel(pk_ref, m_ref, w_ref, vt_ref, mats_ref, rows_ref, kmats_ref, krows_ref):
    T = SSM_T
    H = T
    kw = T * H
    half = kmats_ref.shape[2] // 2
    lane = lax.broadcasted_iota(I32, (1, 2 * half), 1)
    cat = lambda xs: jnp.concatenate(xs, axis=-1)

    def group(g, carry):
        (bbr_f, bbi_f, bbr_b, bbi_b, cre_f, cim_f, cre_b, cim_b, qrf, qif, qrb, qib,
         rrf, rif, rrb, rib, krf, kif) = [pk_ref[g, i * T:(i + 1) * T, :] for i in range(18)]
        zero = jnp.zeros_like(bbr_f)
        even = g % 2 == 0

        def slots(v4):
            ev = cat([v4[0], zero, v4[1], zero, v4[2], zero, v4[3], zero])
            od = cat([zero, v4[0], zero, v4[1], zero, v4[2], zero, v4[3]])
            return jnp.where(even, ev, od)

        mats_ref[0] = slots([bbr_f, bbi_f, bbr_b, bbi_b])
        mats_ref[1] = slots([-bbi_f, bbr_f, -bbi_b, bbr_b])
        mats_ref[2] = slots([cre_f, -cim_f, cre_b, -cim_b])
        mats_ref[3] = slots([-cim_f, -cre_f, -cim_b, -cre_b])
        rows_ref[0] = cat([qrf] * 4 + [qrb] * 4)
        rows_ref[1] = cat([qif] * 4 + [qib] * 4)
        rows_ref[2] = cat([rrf] * 4 + [rrb] * 4)
        rows_ref[3] = cat([rif] * 4 + [rib] * 4)
        kmats_ref[0] = cat([bbr_f, -bbi_f, bbr_b, -bbi_b])
        kmats_ref[1] = cat([cre_f, cre_f, cre_b, cre_b])
        kmats_ref[2] = cat([-cim_f, cim_f, -cim_b, cim_b])
        krows_ref[0] = cat([krf, kif, qrb, qib])
        krows_ref[1] = cat([kif, krf, qib, qrb])

        pair, row0 = g // 2, (g % 2) * kw
        aw, bw, av, bv = [mats_ref[i] for i in range(4)]
        for s in range(T):
            r = pl.ds(pl.multiple_of(row0 + s * H, H), H)
            w_ref[pair, r, :] = (aw * rows_ref[0, pl.ds(s, 1), :] + bw * rows_ref[1, pl.ds(s, 1), :]).astype(BF16)
            vt_ref[pair, r, :] = (av * rows_ref[2, pl.ds(s, 1), :] + bv * rows_ref[3, pl.ds(s, 1), :]).astype(BF16)
        lhs, a4, b4 = [kmats_ref[i] for i in range(3)]
        blocks = []
        for i in range(2 * T):
            k = abs(i - (T - 1))
            if i == 2 * T - 1:
                blocks.append(jnp.zeros_like(a4))
                continue
            cp = a4 * krows_ref[0, pl.ds(k, 1), :] + b4 * krows_ref[1, pl.ds(k, 1), :]
            if i < T - 1:
                cp = jnp.where(lane >= half, cp, 0.0)
            elif i > T - 1:
                cp = jnp.where(lane < half, cp, 0.0)
            blocks.append(cp)
        rhs = jnp.concatenate(blocks, axis=0)
        lhs_hi = lhs.astype(BF16)
        lhs_lo = (lhs - lhs_hi.astype(F32)).astype(BF16)
        rhs_hi = rhs.astype(BF16)
        rhs_lo = (rhs - rhs_hi.astype(F32)).astype(BF16)
        z = _nt_dot(lhs_hi, rhs_hi) + _nt_dot(lhs_lo, rhs_hi) + _nt_dot(lhs_hi, rhs_lo)
        for s in range(T):
            off = (T - 1 - s) * H
            m_ref[g, s * H:(s + 1) * H, :] = z[:, off:off + kw].astype(BF16)
        return carry

    lax.fori_loop(0, pk_ref.shape[0], group, 0)


def _ssm_prep(packed):
    G, n_rows, P = packed.shape
    T = H = SSM_T
    gs = LANES // SSM_GROUP
    return pl.pallas_call(
        _ssm_prep_kernel,
        grid=(G // gs,),
        in_specs=[pl.BlockSpec((gs, n_rows, P), lambda j: (j, 0, 0))],
        out_specs=[pl.BlockSpec((gs, T * H, T * H), lambda j: (j, 0, 0)),
                   pl.BlockSpec((gs // 2, 2 * T * H, 8 * P), lambda j: (j, 0, 0)),
                   pl.BlockSpec((gs // 2, 2 * T * H, 8 * P), lambda j: (j, 0, 0))],
        out_shape=[jax.ShapeDtypeStruct((G, T * H, T * H), BF16),
                   jax.ShapeDtypeStruct((G // 2, 2 * T * H, 8 * P), BF16),
                   jax.ShapeDtypeStruct((G // 2, 2 * T * H, 8 * P), BF16)],
        scratch_shapes=[pltpu.VMEM((4, H, 8 * P), F32), pltpu.VMEM((4, T, 8 * P), F32),
                        pltpu.VMEM((3, H, 4 * P), F32), pltpu.VMEM((2, T, 4 * P), F32)],
        compiler_params=_params(("arbitrary",)),
        name="ssm_prep",
    )(packed)


def _slab_permutation():
    idx = jnp.arange(8 * LANES)
    t, g, h = idx // LANES, (idx % LANES) // SSM_GROUP, idx % SSM_GROUP
    dst = g * LANES + t * SSM_GROUP + h
    return (dst[:, None] == jnp.arange(8 * LANES)[None, :]).astype(BF16)


def _chunk_scan(s_ref, a_ref, *, bsz, n_pairs):
    n_it = s_ref.shape[0] // (2 * bsz)
    cols = [[pl.ds((4 * j + q) * LANES, LANES) for q in range(4)] for j in range(n_pairs)]
    decay = [[a_ref[:, c] for c in cols[j]] for j in range(n_pairs)]

    def step(i, carry):
        rf = pl.ds(pl.multiple_of(i * 2 * bsz, 2 * bsz), 2 * bsz)
        rb = pl.ds(pl.multiple_of((n_it - 1 - i) * 2 * bsz, 2 * bsz), 2 * bsz)
        out = []
        for j in range(n_pairs):
            ar_f, ai_f, ar_b, ai_b = decay[j]
            xr, xi, yr, yi = carry[4 * j:4 * j + 4]
            sr, si = s_ref[rf, cols[j][0]], s_ref[rf, cols[j][1]]
            xr1 = ar_f * xr - ai_f * xi + sr[:bsz]
            xi1 = ar_f * xi + ai_f * xr + si[:bsz]
            s_ref[rf, cols[j][0]] = jnp.concatenate([xr, xr1], axis=0)
            s_ref[rf, cols[j][1]] = jnp.concatenate([xi, xi1], axis=0)
            xr2 = ar_f * xr1 - ai_f * xi1 + sr[bsz:]
            xi2 = ar_f * xi1 + ai_f * xr1 + si[bsz:]
            tr, ti = s_ref[rb, cols[j][2]], s_ref[rb, cols[j][3]]
            yr1 = ar_b * yr - ai_b * yi + tr[bsz:]
            yi1 = ar_b * yi + ai_b * yr + ti[bsz:]
            s_ref[rb, cols[j][2]] = jnp.concatenate([yr1, yr], axis=0)
            s_ref[rb, cols[j][3]] = jnp.concatenate([yi1, yi], axis=0)
            yr2 = ar_b * yr1 - ai_b * yi1 + tr[:bsz]
            yi2 = ar_b * yi1 + ai_b * yr1 + ti[:bsz]
            out += [xr2, xi2, yr2, yi2]
        return tuple(out)

    z = jnp.zeros((bsz, LANES), F32)
    lax.fori_loop(0, n_it, step, (z,) * (4 * n_pairs))


def _ssm_kernel(u_ref, d_ref, perm_ref, iperm_ref, w_ref, m_ref, v_ref, a_ref, y_ref,
                ug_ref, s_ref, yg_ref, *, bsz):
    T = SSM_T
    R = u_ref.shape[0] // T
    n_groups = LANES // SSM_GROUP
    kw = T * SSM_GROUP
    for tq in range(T // 8):
        cat = jnp.concatenate([u_ref[pl.ds(tq * 8 + t, R, stride=T), :].astype(BF16) for t in range(8)], axis=-1)
        grouped = jnp.dot(cat, perm_ref[...], preferred_element_type=F32).astype(BF16)
        for g in range(n_groups):
            ug_ref[g, :, tq * LANES:(tq + 1) * LANES] = grouped[:, g * LANES:(g + 1) * LANES]
    for pr in range(n_groups // 2):
        u2 = jnp.concatenate([ug_ref[2 * pr], ug_ref[2 * pr + 1]], axis=-1)
        s_ref[:, pr * 4 * LANES:(pr + 1) * 4 * LANES] = jnp.dot(u2, w_ref[pr], preferred_element_type=F32)
    _chunk_scan(s_ref, a_ref, bsz=bsz, n_pairs=n_groups // 2)
    for pr in range(n_groups // 2):
        inter = _nt_dot(s_ref[:, pr * 4 * LANES:(pr + 1) * 4 * LANES].astype(BF16), v_ref[pr])
        for gl in range(2):
            g = 2 * pr + gl
            yg_ref[g] = (jnp.dot(ug_ref[g], m_ref[g], preferred_element_type=F32)
                         + inter[:, gl * kw:(gl + 1) * kw])
    for tq in range(T // 8):
        ycat = jnp.concatenate([yg_ref[g, :, tq * LANES:(tq + 1) * LANES] for g in range(n_groups)], axis=-1)
        y_hi = ycat.astype(BF16)
        y_lo = (ycat - y_hi.astype(F32)).astype(BF16)
        back = (jnp.dot(y_hi, iperm_ref[...], preferred_element_type=F32)
                + jnp.dot(y_lo, iperm_ref[...], preferred_element_type=F32))
        for t in range(8):
            rows = pl.ds(tq * 8 + t, R, stride=T)
            y_ref[rows, :] = back[:, t * LANES:(t + 1) * LANES] + d_ref[...] * u_ref[rows, :]


def _ssm(u_tok, d_skip, ops, bsz):
    m, wpair, vpair, a_chunk = ops
    N, d_ssm = u_tok.shape
    T = SSM_T
    R = N // T
    n_groups = LANES // SSM_GROUP
    perm = _slab_permutation()
    const = lambda shape: pl.BlockSpec(shape, lambda j: (0,) * len(shape))
    return pl.pallas_call(
        functools.partial(_ssm_kernel, bsz=bsz),
        grid=(d_ssm // LANES,),
        in_specs=[pl.BlockSpec((N, LANES), lambda j: (0, j)),
                  pl.BlockSpec((1, LANES), lambda j: (0, j)),
                  const(perm.shape), const(perm.shape),
                  pl.BlockSpec((n_groups // 2,) + wpair.shape[1:], lambda j: (j, 0, 0)),
                  pl.BlockSpec((n_groups,) + m.shape[1:], lambda j: (j, 0, 0)),
                  pl.BlockSpec((n_groups // 2,) + vpair.shape[1:], lambda j: (j, 0, 0)),
                  pl.BlockSpec((1, n_groups // 2 * 4 * LANES), lambda j: (0, j))],
        out_specs=pl.BlockSpec((N, LANES), lambda j: (0, j)),
        out_shape=jax.ShapeDtypeStruct((N, d_ssm), F32),
        scratch_shapes=[pltpu.VMEM((n_groups, R, T * SSM_GROUP), BF16),
                        pltpu.VMEM((R, n_groups // 2 * 4 * LANES), F32),
                        pltpu.VMEM((n_groups, R, T * SSM_GROUP), F32)],
        compiler_params=_params(("arbitrary",)),
        name="ssm",
    )(u_tok, d_skip, perm, perm.T, wpair, m, vpair, a_chunk)


def _attn_tables(rpb, rows):
    kh = min(WIN_H_MAX, rows)
    H = rpb.shape[0]
    w = jnp.arange(GRID_W)
    col_start = jnp.clip(w - WIN_W // 2, 0, GRID_W - WIN_W)
    col_mask = (w[None, :] >= col_start[:, None]) & (w[None, :] < col_start[:, None] + WIN_W)
    dc = jnp.clip(w[None, :] - w[:, None] + (WIN_W - 1), 0, 2 * WIN_W - 2)
    col_sel = (dc[:, :, None] == jnp.arange(2 * WIN_W - 1)[None, None, :]).astype(F32)
    tile = jnp.einsum('hdc,qkc->hdqk', rpb, col_sel, precision=lax.Precision.HIGHEST)
    tile = jnp.where(col_mask[None, None], tile * LOG2E, NEG)
    lo_pad, hi_pad = ATT_ROWS, ATT_WIN - kh + 1
    tile = jnp.pad(tile, ((0, 0), (lo_pad, hi_pad), (0, 0), (0, 0)))
    n_dr = tile.shape[1] - 1
    tiles = jnp.concatenate([tile[:, :-1], tile[:, 1:]], axis=-1).reshape(H // 2, 2, n_dr, GRID_W, 2 * GRID_W)
    i = jnp.arange(ATT_ROWS)
    last_union = rows - ATT_WIN
    last_rel = (rows - kh) - last_union
    rel = jnp.stack([jnp.zeros_like(i), i, jnp.full_like(i, last_rel)])
    j = jnp.arange(ATT_WIN)
    valid = (j[None, None, :] >= rel[:, :, None]) & (j[None, None, :] < rel[:, :, None] + kh)
    masks = jnp.where(valid, 0.0, NEG).astype(F32)
    masks = jnp.repeat(masks.reshape(3 * ATT_ROWS * ATT_WIN // 2, 2), GRID_W, axis=-1)
    return tiles, masks.reshape(-1, 1, 2 * GRID_W)


def _attn_kernel(q_ref, k_ref, v_ref, t_ref, m_ref, o_ref, *, rows, kh):
    n_blocks = rows // ATT_ROWS
    nq = ATT_ROWS * GRID_W
    nk = ATT_WIN * GRID_W
    n_pairs = ATT_WIN // 2
    pw = 2 * GRID_W
    lane = lax.broadcasted_iota(I32, (1, 2 * HEAD_DIM), 1)
    head_lanes = [lane < HEAD_DIM, lane >= HEAD_DIM]
    rel = [[0] * ATT_ROWS, list(range(ATT_ROWS)), [(rows - kh) - (rows - ATT_WIN)] * ATT_ROWS]

    def geometry(rb):
        pattern = 0 if rb == 0 else (2 if rb == n_blocks - 1 else 1)
        union = min(max(rb * ATT_ROWS - kh // 2, 0), rows - ATT_WIN)
        dr0 = union - rb * ATT_ROWS + (WIN_H_MAX - 1) + ATT_ROWS
        return pattern, dr0, pl.ds(rb * nq, nq), pl.ds(union * GRID_W, nk)

    def scores(rb, hl):
        _, _, q_rows, k_rows = geometry(rb)
        qm = jnp.where(head_lanes[hl], q_ref[0, q_rows, :], 0.0).astype(BF16)
        return _nt_dot(qm, k_ref[0, k_rows, :])

    def softmax(rb, hl, s):
        pattern, dr0, _, _ = geometry(rb)
        probs, denoms = [], []
        for i in range(ATT_ROWS):
            first, last = rel[pattern][i], rel[pattern][i] + kh - 1
            pairs = range(first // 2, last // 2 + 1)
            bias = []
            for jp in pairs:
                tile = t_ref[0, hl, dr0 + 2 * jp - i]
                if 2 * jp < first or 2 * jp + 1 > last:
                    tile = tile + m_ref[(pattern * ATT_ROWS + i) * n_pairs + jp]
                bias.append(tile)
            si = s[i * GRID_W:(i + 1) * GRID_W, pairs[0] * pw:(pairs[-1] + 1) * pw] + jnp.concatenate(bias, axis=-1)
            pi = jnp.exp2(si - jnp.max(si, axis=-1, keepdims=True))
            denoms.append(jnp.sum(pi, axis=-1, keepdims=True))
            pieces = [pi.astype(BF16)]
            if pairs[0] > 0:
                pieces.insert(0, jnp.zeros((GRID_W, pairs[0] * pw), BF16))
            if pairs[-1] + 1 < n_pairs:
                pieces.append(jnp.zeros((GRID_W, (n_pairs - 1 - pairs[-1]) * pw), BF16))
            probs.append(jnp.concatenate(pieces, axis=-1))
        return jnp.concatenate(probs, axis=0), jnp.concatenate(denoms, axis=0)

    def weighted_values(rb, hl, p, denom):
        _, _, _, k_rows = geometry(rb)
        vm = jnp.where(head_lanes[hl], v_ref[0, k_rows, :], 0.0).astype(BF16)
        return jnp.dot(p, vm, preferred_element_type=F32) / denom

    passes = [(rb, hl) for rb in range(n_blocks) for hl in range(2)]
    ahead = 1
    queue = [scores(*passes[k]) for k in range(ahead)]
    pending = None
    partial = {}

    def finish(item):
        (rb, hl), p, denom = item
        o = weighted_values(rb, hl, p, denom)
        if hl == 0:
            partial[rb] = o
        else:
            o_ref[0, geometry(rb)[2], :] = partial.pop(rb) + o

    for n, (rb, hl) in enumerate(passes):
        s = queue.pop(0)
        if n + ahead < len(passes):
            queue.append(scores(*passes[n + ahead]))
        p, denom = softmax(rb, hl, s)
        if pending is not None:
            finish(pending)
        pending = ((rb, hl), p, denom)
    finish(pending)


def _attn(qkv, tiles, masks):
    bsz, L, n3 = qkv.shape
    d = n3 // 3
    rows = L // GRID_W
    kh = min(WIN_H_MAX, rows)
    assert rows % ATT_ROWS == 0 and rows // ATT_ROWS >= 3 and ATT_ROWS == kh // 2
    assert ATT_WIN % 2 == 0 and kh + ATT_ROWS - 1 <= ATT_WIN <= rows
    n_hp = d // (2 * HEAD_DIM)
    blk = pl.BlockSpec((1, L, 2 * HEAD_DIM), lambda b, hp: (b, 0, hp))
    return pl.pallas_call(
        functools.partial(_attn_kernel, rows=rows, kh=kh),
        grid=(bsz, n_hp),
        in_specs=[blk,
                  pl.BlockSpec((1, L, 2 * HEAD_DIM), lambda b, hp: (b, 0, n_hp + hp)),
                  pl.BlockSpec((1, L, 2 * HEAD_DIM), lambda b, hp: (b, 0, 2 * n_hp + hp)),
                  pl.BlockSpec((1,) + tiles.shape[1:], lambda b, hp: (hp, 0, 0, 0, 0)),
                  pl.BlockSpec(masks.shape, lambda b, hp: (0, 0, 0))],
        out_specs=blk,
        out_shape=jax.ShapeDtypeStruct((bsz, L, d), F32),
        compiler_params=_params(("arbitrary", "arbitrary")),
        name="attn",
    )(qkv, qkv, qkv, tiles, masks)


def _mix_kernel(ys_ref, wglu_ref, bglu_ref, gs_ref, ya_ref, ga_ref, wout_ref,
                x_ref, gt_ref, gf_ref, sh_ref, sc_ref, wr_ref,
                x1_ref, h2_ref, lg_ref):
    d_ssm = ys_ref.shape[3]
    y = ys_ref[:, 0].reshape(-1, d_ssm)
    z = jax.nn.gelu(y)
    gate = _sigmoid(jnp.dot(z.astype(BF16), wglu_ref[...], preferred_element_type=F32) + bglu_ref[...])
    a = _rms(z * gate, gs_ref[...]).astype(BF16)
    t = _rms(ya_ref[0], ga_ref[...]).astype(BF16)
    mixed = (jnp.dot(a, wout_ref[:d_ssm, :], preferred_element_type=F32)
             + jnp.dot(t, wout_ref[d_ssm:, :], preferred_element_type=F32))
    x1 = x_ref[0] + gt_ref[0] * mixed
    x1_ref[0] = x1
    h2 = _rms(x1, gf_ref[...]) * (1.0 + sc_ref[0]) + sh_ref[0]
    h2_hi = h2.astype(BF16)
    h2_ref[0] = h2_hi
    h2_lo = (h2 - h2_hi.astype(F32)).astype(BF16)
    wr = wr_ref[...]
    wr_hi = wr.astype(BF16)
    wr_lo = (wr - wr_hi.astype(F32)).astype(BF16)
    n_e = wr.shape[1]
    tm = h2.shape[0]
    r = jnp.dot(jnp.concatenate([h2_hi, h2_lo], axis=0), jnp.concatenate([wr_hi, wr_lo], axis=1),
                preferred_element_type=F32)
    lg_ref[0] = r[:tm, :n_e] + r[:tm, n_e:] + r[tm:, :n_e]


def _mix(ys, wglu_bf, b_glu, g_ssm, ya, g_attn, wout_bf, x, gt1, g_ffn, sh2, sc2, w_router):
    bsz, L, D = x.shape
    d_ssm = ys.shape[3]
    E = w_router.shape[1]
    tm = 512
    T = SSM_T
    tile = lambda n: pl.BlockSpec((1, tm, n), lambda b, i: (b, i, 0))
    per_b = lambda n: pl.BlockSpec((1, 1, n), lambda b, i: (b, 0, 0))
    full = lambda r, n: pl.BlockSpec((r, n), lambda b, i: (0, 0), pipeline_mode=pl.Buffered(1))
    return pl.pallas_call(
        _mix_kernel,
        grid=(bsz, L // tm),
        in_specs=[pl.BlockSpec((tm // T, 1, T, d_ssm), lambda b, i: (i, b, 0, 0)),
                  full(d_ssm, d_ssm), full(1, d_ssm), full(1, d_ssm),
                  tile(d_ssm), full(1, d_ssm), full(D, D),
                  tile(D), per_b(D), full(1, D), per_b(D), per_b(D), full(D, E)],
        out_specs=[tile(D), tile(D), tile(E)],
        out_shape=[jax.ShapeDtypeStruct((bsz, L, D), F32),
                   jax.ShapeDtypeStruct((bsz, L, D), BF16),
                   jax.ShapeDtypeStruct((bsz, L, E), F32)],
        compiler_params=_params(("arbitrary", "arbitrary")),
        name="mix",
    )(ys, wglu_bf, b_glu, g_ssm, ya, g_attn, wout_bf, x, gt1, g_ffn, sh2, sc2, w_router)


TOPK_BISECTIONS = 160


def _topk_kernel(lg_ref, aff_ref, pos_ref, *, cap):
    lg = lg_ref[...]
    bsz, E, L = lg.shape
    e = jnp.exp(lg - jnp.max(lg, axis=1, keepdims=True))
    aff3 = e / jnp.sum(e, axis=1, keepdims=True)
    aff_ref[...] = aff3
    aff = aff3.reshape(bsz * E, L)

    def count(mask):
        return jnp.sum(jnp.where(mask, 1.0, 0.0), axis=-1, keepdims=True)

    def midpoint(lo, hi):
        return lo + (hi - lo) * 0.5

    def unresolved(carry):
        lo, hi, it = carry
        mid = midpoint(lo, hi)
        open_rows = jnp.max(jnp.where((mid > lo) & (mid < hi), 1.0, 0.0))
        return jnp.logical_and(it < TOPK_BISECTIONS, open_rows > 0.0)

    def halve(carry):
        lo, hi, it = carry
        mid = midpoint(lo, hi)
        ok = count(aff >= mid) >= cap
        return jnp.where(ok, mid, lo), jnp.where(ok, hi, mid), it + 1

    lo0 = jnp.zeros((bsz * E, 1), F32)
    thr, _, _ = lax.while_loop(unresolved, halve, (lo0, lo0 + 2.0, jnp.int32(0)))
    gt = aff > thr
    eq = aff == thr
    need = cap - count(gt)
    blk = LANES
    tri = jnp.where(lax.broadcasted_iota(I32, (blk, blk), 0) < lax.broadcasted_iota(I32, (blk, blk), 1),
                    1.0, 0.0).astype(BF16)

    def prefix_count(mask):
        ones = jnp.where(mask, 1.0, 0.0)
        run = jnp.zeros((bsz * E, 1), F32)
        outs = []
        for j in range(L // blk):
            piece = ones[:, j * blk:(j + 1) * blk]
            outs.append(jnp.dot(piece.astype(BF16), tri, preferred_element_type=F32) + run)
            run = run + jnp.sum(piece, axis=-1, keepdims=True)
        return jnp.concatenate(outs, axis=-1)

    sel = gt | (eq & (prefix_count(eq) < need))
    pos = prefix_count(sel)
    pos_ref[...] = jnp.where(sel, pos.astype(I32), -1).reshape(bsz, E, L)


def _topk(lg_t, cap):
    bsz, E, L = lg_t.shape
    spec = pl.BlockSpec((bsz, E, L), lambda i: (0, 0, 0))
    return pl.pallas_call(
        functools.partial(_topk_kernel, cap=cap),
        grid=(1,),
        in_specs=[spec],
        out_specs=[spec, spec],
        out_shape=[jax.ShapeDtypeStruct((bsz, E, L), F32), jax.ShapeDtypeStruct((bsz, E, L), I32)],
        compiler_params=_params(("arbitrary",)),
        name="topk",
    )(lg_t)


def _gather_kernel(pos_ref, aff_ref, h_ref, xe_ref, as_ref, *, cap):
    L = h_ref.shape[1]
    n_e = pos_ref.shape[1]
    slot = lax.broadcasted_iota(I32, (cap, L), 0)
    hits = [pos_ref[0, e] == slot for e in range(n_e)]
    onehot = jnp.concatenate([jnp.where(h, 1.0, 0.0).astype(BF16) for h in hits], axis=0)
    xe = jnp.dot(onehot, h_ref[0], preferred_element_type=F32).astype(BF16)
    for e in range(n_e):
        xe_ref[e] = xe[e * cap:(e + 1) * cap]
        as_ref[e] = jnp.sum(jnp.where(hits[e], aff_ref[0, e], 0.0), axis=-1, keepdims=True)


def _gather(pos, aff, h2, cap):
    bsz, E, L = pos.shape
    D = h2.shape[2]
    pos4 = pos.reshape(bsz, E, 1, L)
    aff4 = aff.reshape(bsz, E, 1, L)
    ge = 4
    return pl.pallas_call(
        functools.partial(_gather_kernel, cap=cap),
        grid=(bsz, E // ge),
        in_specs=[pl.BlockSpec((1, ge, 1, L), lambda b, e: (b, e, 0, 0)),
                  pl.BlockSpec((1, ge, 1, L), lambda b, e: (b, e, 0, 0)),
                  pl.BlockSpec((1, L, D), lambda b, e: (b, 0, 0))],
        out_specs=[pl.BlockSpec((ge, cap, D), lambda b, e: (e, b, 0)),
                   pl.BlockSpec((ge, cap, 1), lambda b, e: (e, b, 0))],
        out_shape=[jax.ShapeDtypeStruct((E, bsz * cap, D), BF16),
                   jax.ShapeDtypeStruct((E, bsz * cap, 1), F32)],
        compiler_params=_params(("arbitrary", "arbitrary")),
        name="gather",
    )(pos4, aff4, h2)


def _ffn_kernel(xe_ref, wg_ref, wu_ref, wd_ref, as_ref, ye_ref, acc_ref):
    f = pl.program_id(1)

    @pl.when(f == 0)
    def _():
        acc_ref[...] = jnp.zeros_like(acc_ref)

    x = xe_ref[0]
    g = jnp.dot(x, wg_ref[0].astype(BF16), preferred_element_type=F32)
    up = jnp.dot(x, wu_ref[0].astype(BF16), preferred_element_type=F32)
    hid = (g * _sigmoid(g) * up).astype(BF16)
    acc_ref[...] += jnp.dot(hid, wd_ref[0].astype(BF16), preferred_element_type=F32)

    @pl.when(f == pl.num_programs(1) - 1)
    def _():
        bsz, _, cap, _ = ye_ref.shape
        y = (acc_ref[...] * as_ref[0]).astype(BF16)
        for b in range(bsz):
            ye_ref[b, 0] = y[b * cap:(b + 1) * cap]


def _ffn(xe, w_gate, w_up, w_down, aff_slot, bsz):
    E, R, D = xe.shape
    F = w_gate.shape[2]
    cap = R // bsz
    tf = 256
    return pl.pallas_call(
        _ffn_kernel,
        grid=(E, F // tf),
        in_specs=[pl.BlockSpec((1, R, D), lambda e, f: (e, 0, 0)),
                  pl.BlockSpec((1, D, tf), lambda e, f: (e, 0, f)),
                  pl.BlockSpec((1, D, tf), lambda e, f: (e, 0, f)),
                  pl.BlockSpec((1, tf, D), lambda e, f: (e, f, 0)),
                  pl.BlockSpec((1, R, 1), lambda e, f: (e, 0, 0))],
        out_specs=pl.BlockSpec((bsz, 1, cap, D), lambda e, f: (0, e, 0, 0)),
        out_shape=jax.ShapeDtypeStruct((bsz, E, cap, D), BF16),
        scratch_shapes=[pltpu.VMEM((R, D), F32)],
        compiler_params=_params(("arbitrary", "arbitrary")),
        name="ffn",
    )(xe, w_gate, w_up, w_down, aff_slot)


def _combine_kernel(pos_ref, ye_ref, x1_ref, gt_ref, g_ref, o_ref, *, final_norm):
    pos_t = pos_ref[0]
    n_e, cap, d = ye_ref.shape[1:]
    slot = lax.broadcasted_iota(I32, (pos_t.shape[0], cap), 1)
    onehot = jnp.concatenate([jnp.where(pos_t[:, e:e + 1] == slot, 1.0, 0.0).astype(BF16) for e in range(n_e)],
                             axis=-1)
    moe = jnp.dot(onehot, ye_ref[0].reshape(n_e * cap, d), preferred_element_type=F32)
    o = x1_ref[0] + gt_ref[0] * moe
    o_ref[0] = _rms(o, g_ref[...]) if final_norm else o


def _combine(pos_t, ye, x1, gt2, g_final, final_norm):
    bsz, L, E = pos_t.shape
    D = x1.shape[2]
    cap = ye.shape[2]
    tl = 512
    return pl.pallas_call(
        functools.partial(_combine_kernel, final_norm=final_norm),
        grid=(bsz, L // tl),
        in_specs=[pl.BlockSpec((1, tl, E), lambda b, i: (b, i, 0)),
                  pl.BlockSpec((1, E, cap, D), lambda b, i: (b, 0, 0, 0), pipeline_mode=pl.Buffered(1)),
                  pl.BlockSpec((1, tl, D), lambda b, i: (b, i, 0)),
                  pl.BlockSpec((1, 1, D), lambda b, i: (b, 0, 0)),
                  pl.BlockSpec((1, D), lambda b, i: (0, 0))],
        out_specs=pl.BlockSpec((1, tl, D), lambda b, i: (b, i, 0)),
        out_shape=jax.ShapeDtypeStruct((bsz, L, D), F32),
        compiler_params=_params(("arbitrary", "arbitrary")),
        name="combine",
    )(pos_t, ye, x1, gt2, g_final)


def kernel(x, c, w_ada, b_ada, g_mix, w_in, ssm_a_re, ssm_a_im, ssm_log_dt, ssm_b_re, ssm_b_im,
           ssm_c_re, ssm_c_im, ssm_d, w_glu, b_glu, rpb, g_ssm_out, g_attn_out, w_out, g_ffn,
           w_router, w_gate, w_up, w_down, g_final):
    bsz, L, D = x.shape
    depth = w_ada.shape[0]
    d_ssm = ssm_d.shape[1]
    E = w_router.shape[2]
    cap = CAPACITY_FACTOR * L // E
    rows = L // GRID_W
    c8 = jnp.zeros((8, D), F32).at[:bsz].set(c)
    row = lambda v: v.reshape(1, -1)
    for layer in range(depth):
        mod = _ada(c8, w_ada[layer], row(b_ada[layer]))[:bsz]
        sh1, sc1, gt1, sh2, sc2, gt2 = [m.reshape(bsz, 1, D) for m in jnp.split(mod, 6, axis=-1)]

        u, qkv = _inproj(x, sh1, sc1, row(g_mix[layer]), _cast_bf16(w_in[layer]), d_ssm)
        ops = _ssm_operators(ssm_a_re[layer], ssm_a_im[layer], ssm_log_dt[layer], ssm_b_re[layer],
                             ssm_b_im[layer], ssm_c_re[layer], ssm_c_im[layer])
        y_ssm = _ssm(u.reshape(-1, d_ssm), row(ssm_d[layer]), ops, bsz).reshape(u.shape)
        y_attn = _attn(qkv, *_attn_tables(rpb[layer], rows))
        x1, h2, logits = _mix(y_ssm, _cast_bf16(w_glu[layer]), row(b_glu[layer]),
                              row(g_ssm_out[layer]), y_attn, row(g_attn_out[layer]), _cast_bf16(w_out[layer]),
                              x, gt1, row(g_ffn[layer]), sh2, sc2, w_router[layer])

        aff, pos = _topk(jnp.swapaxes(logits, 1, 2), cap)
        xe, aff_slot = _gather(pos, aff, h2, cap)
        ye = _ffn(xe, w_gate[layer], w_up[layer], w_down[layer], aff_slot, bsz)
        x = _combine(jnp.swapaxes(pos, 1, 2), ye, x1, gt2, row(g_final), layer == depth - 1)
    return x
```

```python
import functools
import math

import jax
import jax.numpy as jnp
from jax import lax
from jax.experimental import pallas as pl
from jax.experimental.pallas import tpu as pltpu

F32 = jnp.float32
BF16 = jnp.bfloat16
I32 = jnp.int32

EPS = 1e-6
GRID_W = 64
SSM_GROUP = 16
HEAD_DIM = 64
WIN_H_MAX = 8
WIN_W = 16
CAPACITY_FACTOR = 2
NEG = -1e30
LOG2E = math.log2(math.e)

SSM_T = 16
ATT_ROWS = 4
ATT_WIN = 12

LANES = 128
V7X_VMEM_BYTES = 64 * 1024 * 1024
VMEM_LIMIT = V7X_VMEM_BYTES // 8 * 7


def _params(sem, vmem=VMEM_LIMIT):
    return pltpu.CompilerParams(dimension_semantics=sem, vmem_limit_bytes=vmem)


def _sigmoid(x):
    return 1.0 / (1.0 + jnp.exp(-x))


def _rms(x, g):
    return x * lax.rsqrt(jnp.mean(x * x, axis=-1, keepdims=True) + EPS) * g


def _ada_kernel(c_ref, w_ref, b_ref, o_ref):
    c = c_ref[...]
    ca = (c * _sigmoid(c)).astype(BF16)
    o_ref[...] = jnp.dot(ca, w_ref[...].astype(BF16), preferred_element_type=F32) + b_ref[...]


def _ada(c8, w, b):
    k, n = w.shape
    tn = 1024
    return pl.pallas_call(
        _ada_kernel,
        grid=(n // tn,),
        in_specs=[pl.BlockSpec((8, k), lambda j: (0, 0)),
                  pl.BlockSpec((k, tn), lambda j: (0, j)),
                  pl.BlockSpec((1, tn), lambda j: (0, j))],
        out_specs=pl.BlockSpec((8, tn), lambda j: (0, j)),
        out_shape=jax.ShapeDtypeStruct((8, n), F32),
        compiler_params=_params(("arbitrary",)),
        name="ada",
    )(c8, w, b)


def _cast_kernel(w_ref, o_ref):
    o_ref[...] = w_ref[...].astype(BF16)


def _cast_bf16(w):
    k, n = w.shape
    tk = 256
    return pl.pallas_call(
        _cast_kernel,
        grid=(k // tk,),
        in_specs=[pl.BlockSpec((tk, n), lambda i: (i, 0))],
        out_specs=pl.BlockSpec((tk, n), lambda i: (i, 0)),
        out_shape=jax.ShapeDtypeStruct((k, n), BF16),
        compiler_params=_params(("arbitrary",)),
        name="cast",
    )(w)


def _inproj_kernel(x_ref, sh_ref, sc_ref, g_ref, w_ref, u_ref, qkv_ref, *, d_ssm):
    h = (_rms(x_ref[0], g_ref[...]) * (1.0 + sc_ref[0]) + sh_ref[0]).astype(BF16)
    n_total = w_ref.shape[1]
    for n in range(n_total // d_ssm):
        r = jnp.dot(h, w_ref[:, n * d_ssm:(n + 1) * d_ssm], preferred_element_type=F32)
        if n == 0:
            for cc in range(u_ref.shape[0]):
                u_ref[cc, 0] = r[cc * SSM_T:(cc + 1) * SSM_T]
        else:
            if n == 1:
                r = r * (HEAD_DIM ** -0.5 * LOG2E)
            qkv_ref[0, :, (n - 1) * d_ssm:n * d_ssm] = r.astype(BF16)


def _inproj(x, sh, sc, g, w_bf, d_ssm):
    bsz, L, D = x.shape
    n = w_bf.shape[1]
    tm = 512
    T = SSM_T
    return pl.pallas_call(
        functools.partial(_inproj_kernel, d_ssm=d_ssm),
        grid=(bsz, L // tm),
        in_specs=[pl.BlockSpec((1, tm, D), lambda b, i: (b, i, 0)),
                  pl.BlockSpec((1, 1, D), lambda b, i: (b, 0, 0)),
                  pl.BlockSpec((1, 1, D), lambda b, i: (b, 0, 0)),
                  pl.BlockSpec((1, D), lambda b, i: (0, 0)),
                  pl.BlockSpec((D, n), lambda b, i: (0, 0), pipeline_mode=pl.Buffered(1))],
        out_specs=[pl.BlockSpec((tm // T, 1, T, d_ssm), lambda b, i: (i, b, 0, 0)),
                   pl.BlockSpec((1, tm, n - d_ssm), lambda b, i: (b, i, 0))],
        out_shape=[jax.ShapeDtypeStruct((L // T, bsz, T, d_ssm), F32),
                   jax.ShapeDtypeStruct((bsz, L, n - d_ssm), BF16)],
        compiler_params=_params(("arbitrary", "arbitrary")),
        name="inproj",
    )(x, sh, sc, g, w_bf)


def _ssm_operators(a_re, a_im, log_dt, b_re, b_im, c_re, c_im):
    T = SSM_T
    G, P = a_re.shape[1], a_re.shape[2]
    H = b_re.shape[3]
    dt = jnp.exp(log_dt)[..., None]
    mag = jnp.exp(a_re * dt)
    lbr = mag * jnp.cos(a_im * dt)
    lbi = mag * jnp.sin(a_im * dt)
    den = a_re * a_re + a_im * a_im
    nr = lbr - 1.0
    fr = ((nr * a_re + lbi * a_im) / den)[:, :, None, :]
    fi = ((lbi * a_re - nr * a_im) / den)[:, :, None, :]
    b_re_t, b_im_t = jnp.swapaxes(b_re, 2, 3), jnp.swapaxes(b_im, 2, 3)
    bbr = fr * b_re_t - fi * b_im_t
    bbi = fr * b_im_t + fi * b_re_t
    prs, pis = [jnp.ones_like(lbr)], [jnp.zeros_like(lbi)]
    for _ in range(T):
        pr, pi = prs[-1], pis[-1]
        prs.append(pr * lbr - pi * lbi)
        pis.append(pr * lbi + pi * lbr)
    pr = jnp.stack(prs, axis=2)
    pi = jnp.stack(pis, axis=2)
    assert H == T
    rev = lambda x: x[:, ::-1]
    sections = [bbr[0], bbi[0], bbr[1], bbi[1], c_re[0], c_im[0], c_re[1], c_im[1],
                rev(pr[0, :, :T]), rev(pi[0, :, :T]), pr[1, :, :T], pi[1, :, :T],
                pr[0, :, 1:], pi[0, :, 1:], rev(pr[1, :, 1:]), rev(pi[1, :, 1:]),
                pr[0, :, :T], pi[0, :, :T]]
    m, wpair, vtpair = _ssm_prep(jnp.concatenate(sections, axis=1))
    a4 = jnp.stack([pr[0, :, T], pi[0, :, T], pr[1, :, T], pi[1, :, T]])
    a_chunk = jnp.transpose(a4.reshape(4, G // 2, 2, P), (1, 0, 2, 3)).reshape(1, G // 2 * 8 * P)
    return m, wpair, vtpair, a_chunk


def _nt_dot(a, b):
    return lax.dot_general(a, b, (((1,), (1,)), ((), ())), preferred_element_type=F32)


def _ssm_prep_kernel(pk_ref, m_ref, w_ref, vt_ref, mats_ref, rows_ref, kmats_ref, krows_ref):
    T = SSM_T
    H = T
    kw = T * H
    half = kmats_ref.shape[2] // 2
    lane = lax.broadcasted_iota(I32, (1, 2 * half), 1)
    cat = lambda xs: jnp.concatenate(xs, axis=-1)

    def group(g, carry):
        (bbr_f, bbi_f, bbr_b, bbi_b, cre_f, cim_f, cre_b, cim_b, qrf, qif, qrb, qib,
         rrf, rif, rrb, rib, krf, kif) = [pk_ref[g, i * T:(i + 1) * T, :] for i in range(18)]
        zero = jnp.zeros_like(bbr_f)
        even = g % 2 == 0

        def slots(v4):
            ev = cat([v4[0], zero, v4[1], zero, v4[2], zero, v4[3], zero])
            od = cat([zero, v4[0], zero, v4[1], zero, v4[2], zero, v4[3]])
            return jnp.where(even, ev, od)

        mats_ref[0] = slots([bbr_f, bbi_f, bbr_b, bbi_b])
        mats_ref[1] = slots([-bbi_f, bbr_f, -bbi_b, bbr_b])
        mats_ref[2] = slots([cre_f, -cim_f, cre_b, -cim_b])
        mats_ref[3] = slots([-cim_f, -cre_f, -cim_b, -cre_b])
        rows_ref[0] = cat([qrf] * 4 + [qrb] * 4)
        rows_ref[1] = cat([qif] * 4 + [qib] * 4)
        rows_ref[2] = cat([rrf] * 4 + [rrb] * 4)
        rows_ref[3] = cat([rif] * 4 + [rib] * 4)
        kmats_ref[0] = cat([bbr_f, -bbi_f, bbr_b, -bbi_b])
        kmats_ref[1] = cat([cre_f, cre_f, cre_b, cre_b])
        kmats_ref[2] = cat([-cim_f, cim_f, -cim_b, cim_b])
        krows_ref[0] = cat([krf, kif, qrb, qib])
        krows_ref[1] = cat([kif, krf, qib, qrb])

        pair, row0 = g // 2, (g % 2) * kw
        aw, bw, av, bv = [mats_ref[i] for i in range(4)]
        for s in range(T):
            r = pl.ds(pl.multiple_of(row0 + s * H, H), H)
            w_ref[pair, r, :] = (aw * rows_ref[0, pl.ds(s, 1), :] + bw * rows_ref[1, pl.ds(s, 1), :]).astype(BF16)
            vt_ref[pair, r, :] = (av * rows_ref[2, pl.ds(s, 1), :] + bv * rows_ref[3, pl.ds(s, 1), :]).astype(BF16)
        lhs, a4, b4 = [kmats_ref[i] for i in range(3)]
        blocks = []
        for i in range(2 * T):
            k = abs(i - (T - 1))
            if i == 2 * T - 1:
                blocks.append(jnp.zeros_like(a4))
                continue
            cp = a4 * krows_ref[0, pl.ds(k, 1), :] + b4 * krows_ref[1, pl.ds(k, 1), :]
            if i < T - 1:
                cp = jnp.where(lane >= half, cp, 0.0)
            elif i > T - 1:
                cp = jnp.where(lane < half, cp, 0.0)
            blocks.append(cp)
        rhs = jnp.concatenate(blocks, axis=0)
        lhs_hi = lhs.astype(BF16)
        lhs_lo = (lhs - lhs_hi.astype(F32)).astype(BF16)
        rhs_hi = rhs.astype(BF16)
        rhs_lo = (rhs - rhs_hi.astype(F32)).astype(BF16)
        z = _nt_dot(lhs_hi, rhs_hi) + _nt_dot(lhs_lo, rhs_hi) + _nt_dot(lhs_hi, rhs_lo)
        for s in range(T):
            off = (T - 1 - s) * H
            m_ref[g, s * H:(s + 1) * H, :] = z[:, off:off + kw].astype(BF16)
        return carry

    lax.fori_loop(0, pk_ref.shape[0], group, 0)


def _ssm_prep(packed):
    G, n_rows, P = packed.shape
    T = H = SSM_T
    gs = LANES // SSM_GROUP
    return pl.pallas_call(
        _ssm_prep_kernel,
        grid=(G // gs,),
        in_specs=[pl.BlockSpec((gs, n_rows, P), lambda j: (j, 0, 0))],
        out_specs=[pl.BlockSpec((gs, T * H, T * H), lambda j: (j, 0, 0)),
                   pl.BlockSpec((gs // 2, 2 * T * H, 8 * P), lambda j: (j, 0, 0)),
                   pl.BlockSpec((gs // 2, 2 * T * H, 8 * P), lambda j: (j, 0, 0))],
        out_shape=[jax.ShapeDtypeStruct((G, T * H, T * H), BF16),
                   jax.ShapeDtypeStruct((G // 2, 2 * T * H, 8 * P), BF16),
                   jax.ShapeDtypeStruct((G // 2, 2 * T * H, 8 * P), BF16)],
        scratch_shapes=[pltpu.VMEM((4, H, 8 * P), F32), pltpu.VMEM((4, T, 8 * P), F32),
                        pltpu.VMEM((3, H, 4 * P), F32), pltpu.VMEM((2, T, 4 * P), F32)],
        compiler_params=_params(("arbitrary",)),
        name="ssm_prep",
    )(packed)


def _slab_permutation():
    idx = jnp.arange(8 * LANES)
    t, g, h = idx // LANES, (idx % LANES) // SSM_GROUP, idx % SSM_GROUP
    dst = g * LANES + t * SSM_GROUP + h
    return (dst[:, None] == jnp.arange(8 * LANES)[None, :]).astype(BF16)


def _chunk_scan(s_ref, a_ref, *, bsz, n_pairs):
    n_it = s_ref.shape[0] // (2 * bsz)
    cols = [[pl.ds((4 * j + q) * LANES, LANES) for q in range(4)] for j in range(n_pairs)]
    decay = [[a_ref[:, c] for c in cols[j]] for j in range(n_pairs)]

    def step(i, carry):
        rf = pl.ds(pl.multiple_of(i * 2 * bsz, 2 * bsz), 2 * bsz)
        rb = pl.ds(pl.multiple_of((n_it - 1 - i) * 2 * bsz, 2 * bsz), 2 * bsz)
        out = []
        for j in range(n_pairs):
            ar_f, ai_f, ar_b, ai_b = decay[j]
            xr, xi, yr, yi = carry[4 * j:4 * j + 4]
            sr, si = s_ref[rf, cols[j][0]], s_ref[rf, cols[j][1]]
            xr1 = ar_f * xr - ai_f * xi + sr[:bsz]
            xi1 = ar_f * xi + ai_f * xr + si[:bsz]
            s_ref[rf, cols[j][0]] = jnp.concatenate([xr, xr1], axis=0)
            s_ref[rf, cols[j][1]] = jnp.concatenate([xi, xi1], axis=0)
            xr2 = ar_f * xr1 - ai_f * xi1 + sr[bsz:]
            xi2 = ar_f * xi1 + ai_f * xr1 + si[bsz:]
            tr, ti = s_ref[rb, cols[j][2]], s_ref[rb, cols[j][3]]
            yr1 = ar_b * yr - ai_b * yi + tr[bsz:]
            yi1 = ar_b * yi + ai_b * yr + ti[bsz:]
            s_ref[rb, cols[j][2]] = jnp.concatenate([yr1, yr], axis=0)
            s_ref[rb, cols[j][3]] = jnp.concatenate([yi1, yi], axis=0)
            yr2 = ar_b * yr1 - ai_b * yi1 + tr[:bsz]
            yi2 = ar_b * yi1 + ai_b * yr1 + ti[:bsz]
            out += [xr2, xi2, yr2, yi2]
        return tuple(out)

    z = jnp.zeros((bsz, LANES), F32)
    lax.fori_loop(0, n_it, step, (z,) * (4 * n_pairs))


def _ssm_kernel(u_ref, d_ref, perm_ref, iperm_ref, w_ref, m_ref, v_ref, a_ref, y_ref,
                ug_ref, s_ref, yg_ref, *, bsz):
    T = SSM_T
    R = u_ref.shape[0] // T
    n_groups = LANES // SSM_GROUP
    kw = T * SSM_GROUP
    for tq in range(T // 8):
        cat = jnp.concatenate([u_ref[pl.ds(tq * 8 + t, R, stride=T), :].astype(BF16) for t in range(8)], axis=-1)
        grouped = jnp.dot(cat, perm_ref[...], preferred_element_type=F32).astype(BF16)
        for g in range(n_groups):
            ug_ref[g, :, tq * LANES:(tq + 1) * LANES] = grouped[:, g * LANES:(g + 1) * LANES]
    for pr in range(n_groups // 2):
        u2 = jnp.concatenate([ug_ref[2 * pr], ug_ref[2 * pr + 1]], axis=-1)
        s_ref[:, pr * 4 * LANES:(pr + 1) * 4 * LANES] = jnp.dot(u2, w_ref[pr], preferred_element_type=F32)
    _chunk_scan(s_ref, a_ref, bsz=bsz, n_pairs=n_groups // 2)
    for pr in range(n_groups // 2):
        inter = _nt_dot(s_ref[:, pr * 4 * LANES:(pr + 1) * 4 * LANES].astype(BF16), v_ref[pr])
        for gl in range(2):
            g = 2 * pr + gl
            yg_ref[g] = (jnp.dot(ug_ref[g], m_ref[g], preferred_element_type=F32)
                         + inter[:, gl * kw:(gl + 1) * kw])
    for tq in range(T // 8):
        ycat = jnp.concatenate([yg_ref[g, :, tq * LANES:(tq + 1) * LANES] for g in range(n_groups)], axis=-1)
        back = jnp.dot(ycat.astype(BF16), iperm_ref[...], preferred_element_type=F32)
        for t in range(8):
            rows = pl.ds(tq * 8 + t, R, stride=T)
            y_ref[rows, :] = back[:, t * LANES:(t + 1) * LANES] + d_ref[...] * u_ref[rows, :]


def _ssm(u_tok, d_skip, ops, bsz):
    m, wpair, vpair, a_chunk = ops
    N, d_ssm = u_tok.shape
    T = SSM_T
    R = N // T
    n_groups = LANES // SSM_GROUP
    perm = _slab_permutation()
    const = lambda shape: pl.BlockSpec(shape, lambda j: (0,) * len(shape))
    return pl.pallas_call(
        functools.partial(_ssm_kernel, bsz=bsz),
        grid=(d_ssm // LANES,),
        in_specs=[pl.BlockSpec((N, LANES), lambda j: (0, j)),
                  pl.BlockSpec((1, LANES), lambda j: (0, j)),
                  const(perm.shape), const(perm.shape),
                  pl.BlockSpec((n_groups // 2,) + wpair.shape[1:], lambda j: (j, 0, 0)),
                  pl.BlockSpec((n_groups,) + m.shape[1:], lambda j: (j, 0, 0)),
                  pl.BlockSpec((n_groups // 2,) + vpair.shape[1:], lambda j: (j, 0, 0)),
                  pl.BlockSpec((1, n_groups // 2 * 4 * LANES), lambda j: (0, j))],
        out_specs=pl.BlockSpec((N, LANES), lambda j: (0, j)),
        out_shape=jax.ShapeDtypeStruct((N, d_ssm), F32),
        scratch_shapes=[pltpu.VMEM((n_groups, R, T * SSM_GROUP), BF16),
                        pltpu.VMEM((R, n_groups // 2 * 4 * LANES), F32),
                        pltpu.VMEM((n_groups, R, T * SSM_GROUP), F32)],
        compiler_params=_params(("arbitrary",)),
        name="ssm",
    )(u_tok, d_skip, perm, perm.T, wpair, m, vpair, a_chunk)


def _attn_tables(rpb, rows):
    kh = min(WIN_H_MAX, rows)
    H = rpb.shape[0]
    w = jnp.arange(GRID_W)
    col_start = jnp.clip(w - WIN_W // 2, 0, GRID_W - WIN_W)
    col_mask = (w[None, :] >= col_start[:, None]) & (w[None, :] < col_start[:, None] + WIN_W)
    dc = jnp.clip(w[None, :] - w[:, None] + (WIN_W - 1), 0, 2 * WIN_W - 2)
    col_sel = (dc[:, :, None] == jnp.arange(2 * WIN_W - 1)[None, None, :]).astype(F32)
    tile = jnp.einsum('hdc,qkc->hdqk', rpb, col_sel, precision=lax.Precision.HIGHEST)
    tile = jnp.where(col_mask[None, None], tile * LOG2E, NEG)
    lo_pad, hi_pad = ATT_ROWS, ATT_WIN - kh + 1
    tile = jnp.pad(tile, ((0, 0), (lo_pad, hi_pad), (0, 0), (0, 0)))
    n_dr = tile.shape[1] - 1
    tiles = jnp.concatenate([tile[:, :-1], tile[:, 1:]], axis=-1).reshape(H // 2, 2, n_dr, GRID_W, 2 * GRID_W)
    i = jnp.arange(ATT_ROWS)
    last_union = rows - ATT_WIN
    last_rel = (rows - kh) - last_union
    rel = jnp.stack([jnp.zeros_like(i), i, jnp.full_like(i, last_rel)])
    j = jnp.arange(ATT_WIN)
    valid = (j[None, None, :] >= rel[:, :, None]) & (j[None, None, :] < rel[:, :, None] + kh)
    masks = jnp.where(valid, 0.0, NEG).astype(F32)
    masks = jnp.repeat(masks.reshape(3 * ATT_ROWS * ATT_WIN // 2, 2), GRID_W, axis=-1)
    return tiles, masks.reshape(-1, 1, 2 * GRID_W)


def _attn_kernel(q_ref, k_ref, v_ref, t_ref, m_ref, o_ref, *, rows, kh):
    n_blocks = rows // ATT_ROWS
    nq = ATT_ROWS * GRID_W
    nk = ATT_WIN * GRID_W
    n_pairs = ATT_WIN // 2
    pw = 2 * GRID_W
    lane = lax.broadcasted_iota(I32, (1, 2 * HEAD_DIM), 1)
    head_lanes = [lane < HEAD_DIM, lane >= HEAD_DIM]
    rel = [[0] * ATT_ROWS, list(range(ATT_ROWS)), [(rows - kh) - (rows - ATT_WIN)] * ATT_ROWS]

    def geometry(rb):
        pattern = 0 if rb == 0 else (2 if rb == n_blocks - 1 else 1)
        union = min(max(rb * ATT_ROWS - kh // 2, 0), rows - ATT_WIN)
        dr0 = union - rb * ATT_ROWS + (WIN_H_MAX - 1) + ATT_ROWS
        return pattern, dr0, pl.ds(rb * nq, nq), pl.ds(union * GRID_W, nk)

    def scores(rb, hl):
        _, _, q_rows, k_rows = geometry(rb)
        qm = jnp.where(head_lanes[hl], q_ref[0, q_rows, :], 0.0).astype(BF16)
        return _nt_dot(qm, k_ref[0, k_rows, :])

    def softmax(rb, hl, s):
        pattern, dr0, _, _ = geometry(rb)
        probs, denoms = [], []
        for i in range(ATT_ROWS):
            first, last = rel[pattern][i], rel[pattern][i] + kh - 1
            pairs = range(first // 2, last // 2 + 1)
            bias = []
            for jp in pairs:
                tile = t_ref[0, hl, dr0 + 2 * jp - i]
                if 2 * jp < first or 2 * jp + 1 > last:
                    tile = tile + m_ref[(pattern * ATT_ROWS + i) * n_pairs + jp]
                bias.append(tile)
            si = s[i * GRID_W:(i + 1) * GRID_W, pairs[0] * pw:(pairs[-1] + 1) * pw] + jnp.concatenate(bias, axis=-1)
            pi = jnp.exp2(si - jnp.max(si, axis=-1, keepdims=True))
            denoms.append(jnp.sum(pi, axis=-1, keepdims=True))
            pieces = [pi.astype(BF16)]
            if pairs[0] > 0:
                pieces.insert(0, jnp.zeros((GRID_W, pairs[0] * pw), BF16))
            if pairs[-1] + 1 < n_pairs:
                pieces.append(jnp.zeros((GRID_W, (n_pairs - 1 - pairs[-1]) * pw), BF16))
            probs.append(jnp.concatenate(pieces, axis=-1))
        return jnp.concatenate(probs, axis=0), jnp.concatenate(denoms, axis=0)

    def weighted_values(rb, hl, p, denom):
        _, _, _, k_rows = geometry(rb)
        vm = jnp.where(head_lanes[hl], v_ref[0, k_rows, :], 0.0).astype(BF16)
        return jnp.dot(p, vm, preferred_element_type=F32) / denom

    passes = [(rb, hl) for rb in range(n_blocks) for hl in range(2)]
    ahead = 1
    queue = [scores(*passes[k]) for k in range(ahead)]
    pending = None
    partial = {}

    def finish(item):
        (rb, hl), p, denom = item
        o = weighted_values(rb, hl, p, denom)
        if hl == 0:
            partial[rb] = o
        else:
            o_ref[0, geometry(rb)[2], :] = partial.pop(rb) + o

    for n, (rb, hl) in enumerate(passes):
        s = queue.pop(0)
        if n + ahead < len(passes):
            queue.append(scores(*passes[n + ahead]))
        p, denom = softmax(rb, hl, s)
        if pending is not None:
            finish(pending)
        pending = ((rb, hl), p, denom)
    finish(pending)


def _attn(qkv, tiles, masks):
    bsz, L, n3 = qkv.shape
    d = n3 // 3
    rows = L // GRID_W
    kh = min(WIN_H_MAX, rows)
    assert rows % ATT_ROWS == 0 and rows // ATT_ROWS >= 3 and ATT_ROWS == kh // 2
    assert ATT_WIN % 2 == 0 and kh + ATT_ROWS - 1 <= ATT_WIN <= rows
    n_hp = d // (2 * HEAD_DIM)
    blk = pl.BlockSpec((1, L, 2 * HEAD_DIM), lambda b, hp: (b, 0, hp))
    return pl.pallas_call(
        functools.partial(_attn_kernel, rows=rows, kh=kh),
        grid=(bsz, n_hp),
        in_specs=[blk,
                  pl.BlockSpec((1, L, 2 * HEAD_DIM), lambda b, hp: (b, 0, n_hp + hp)),
                  pl.BlockSpec((1, L, 2 * HEAD_DIM), lambda b, hp: (b, 0, 2 * n_hp + hp)),
                  pl.BlockSpec((1,) + tiles.shape[1:], lambda b, hp: (hp, 0, 0, 0, 0)),
                  pl.BlockSpec(masks.shape, lambda b, hp: (0, 0, 0))],
        out_specs=blk,
        out_shape=jax.ShapeDtypeStruct((bsz, L, d), F32),
        compiler_params=_params(("arbitrary", "arbitrary")),
        name="attn",
    )(qkv, qkv, qkv, tiles, masks)


def _mix_kernel(ys_ref, wglu_ref, bglu_ref, gs_ref, ya_ref, ga_ref, wout_ref,
                x_ref, gt_ref, gf_ref, sh_ref, sc_ref, wr_ref,
                x1_ref, h2_ref, lg_ref):
    d_ssm = ys_ref.shape[3]
    y = ys_ref[:, 0].reshape(-1, d_ssm)
    z = jax.nn.gelu(y)
    gate = _sigmoid(jnp.dot(z.astype(BF16), wglu_ref[...], preferred_element_type=F32) + bglu_ref[...])
    a = _rms(z * gate, gs_ref[...]).astype(BF16)
    t = _rms(ya_ref[0], ga_ref[...]).astype(BF16)
    mixed = (jnp.dot(a, wout_ref[:d_ssm, :], preferred_element_type=F32)
             + jnp.dot(t, wout_ref[d_ssm:, :], preferred_element_type=F32))
    x1 = x_ref[0] + gt_ref[0] * mixed
    x1_ref[0] = x1
    h2 = _rms(x1, gf_ref[...]) * (1.0 + sc_ref[0]) + sh_ref[0]
    h2_hi = h2.astype(BF16)
    h2_ref[0] = h2_hi
    h2_lo = (h2 - h2_hi.astype(F32)).astype(BF16)
    wr = wr_ref[...]
    wr_hi = wr.astype(BF16)
    wr_lo = (wr - wr_hi.astype(F32)).astype(BF16)
    n_e = wr.shape[1]
    tm = h2.shape[0]
    r = jnp.dot(jnp.concatenate([h2_hi, h2_lo], axis=0), jnp.concatenate([wr_hi, wr_lo], axis=1),
                preferred_element_type=F32)
    lg_ref[0] = r[:tm, :n_e] + r[:tm, n_e:] + r[tm:, :n_e]


def _mix(ys, wglu_bf, b_glu, g_ssm, ya, g_attn, wout_bf, x, gt1, g_ffn, sh2, sc2, w_router):
    bsz, L, D = x.shape
    d_ssm = ys.shape[3]
    E = w_router.shape[1]
    tm = 512
    T = SSM_T
    tile = lambda n: pl.BlockSpec((1, tm, n), lambda b, i: (b, i, 0))
    per_b = lambda n: pl.BlockSpec((1, 1, n), lambda b, i: (b, 0, 0))
    full = lambda r, n: pl.BlockSpec((r, n), lambda b, i: (0, 0), pipeline_mode=pl.Buffered(1))
    return pl.pallas_call(
        _mix_kernel,
        grid=(bsz, L // tm),
        in_specs=[pl.BlockSpec((tm // T, 1, T, d_ssm), lambda b, i: (i, b, 0, 0)),
                  full(d_ssm, d_ssm), full(1, d_ssm), full(1, d_ssm),
                  tile(d_ssm), full(1, d_ssm), full(D, D),
                  tile(D), per_b(D), full(1, D), per_b(D), per_b(D), full(D, E)],
        out_specs=[tile(D), tile(D), tile(E)],
        out_shape=[jax.ShapeDtypeStruct((bsz, L, D), F32),
                   jax.ShapeDtypeStruct((bsz, L, D), BF16),
                   jax.ShapeDtypeStruct((bsz, L, E), F32)],
        compiler_params=_params(("arbitrary", "arbitrary")),
        name="mix",
    )(ys, wglu_bf, b_glu, g_ssm, ya, g_attn, wout_bf, x, gt1, g_ffn, sh2, sc2, w_router)


TOPK_BISECTIONS = 160


def _topk_kernel(lg_ref, aff_ref, pos_ref, *, cap):
    lg = lg_ref[...]
    bsz, E, L = lg.shape
    e = jnp.exp(lg - jnp.max(lg, axis=1, keepdims=True))
    aff3 = e / jnp.sum(e, axis=1, keepdims=True)
    aff_ref[...] = aff3
    aff = aff3.reshape(bsz * E, L)

    def count(mask):
        return jnp.sum(jnp.where(mask, 1.0, 0.0), axis=-1, keepdims=True)

    def midpoint(lo, hi):
        return lo + (hi - lo) * 0.5

    def unresolved(carry):
        lo, hi, it = carry
        mid = midpoint(lo, hi)
        open_rows = jnp.max(jnp.where((mid > lo) & (mid < hi), 1.0, 0.0))
        return jnp.logical_and(it < TOPK_BISECTIONS, open_rows > 0.0)

    def halve(carry):
        lo, hi, it = carry
        mid = midpoint(lo, hi)
        ok = count(aff >= mid) >= cap
        return jnp.where(ok, mid, lo), jnp.where(ok, hi, mid), it + 1

    lo0 = jnp.zeros((bsz * E, 1), F32)
    thr, _, _ = lax.while_loop(unresolved, halve, (lo0, lo0 + 2.0, jnp.int32(0)))
    gt = aff > thr
    eq = aff == thr
    need = cap - count(gt)
    blk = LANES
    tri = jnp.where(lax.broadcasted_iota(I32, (blk, blk), 0) < lax.broadcasted_iota(I32, (blk, blk), 1),
                    1.0, 0.0).astype(BF16)

    def prefix_count(mask):
        ones = jnp.where(mask, 1.0, 0.0)
        run = jnp.zeros((bsz * E, 1), F32)
        outs = []
        for j in range(L // blk):
            piece = ones[:, j * blk:(j + 1) * blk]
            outs.append(jnp.dot(piece.astype(BF16), tri, preferred_element_type=F32) + run)
            run = run + jnp.sum(piece, axis=-1, keepdims=True)
        return jnp.concatenate(outs, axis=-1)

    sel = gt | (eq & (prefix_count(eq) < need))
    pos = prefix_count(sel)
    pos_ref[...] = jnp.where(sel, pos.astype(I32), -1).reshape(bsz, E, L)


def _topk(lg_t, cap):
    bsz, E, L = lg_t.shape
    spec = pl.BlockSpec((bsz, E, L), lambda i: (0, 0, 0))
    return pl.pallas_call(
        functools.partial(_topk_kernel, cap=cap),
        grid=(1,),
        in_specs=[spec],
        out_specs=[spec, spec],
        out_shape=[jax.ShapeDtypeStruct((bsz, E, L), F32), jax.ShapeDtypeStruct((bsz, E, L), I32)],
        compiler_params=_params(("arbitrary",)),
        name="topk",
    )(lg_t)


def _gather_kernel(pos_ref, aff_ref, h_ref, xe_ref, as_ref, *, cap):
    L = h_ref.shape[1]
    n_e = pos_ref.shape[1]
    slot = lax.broadcasted_iota(I32, (cap, L), 0)
    hits = [pos_ref[0, e] == slot for e in range(n_e)]
    onehot = jnp.concatenate([jnp.where(h, 1.0, 0.0).astype(BF16) for h in hits], axis=0)
    xe = jnp.dot(onehot, h_ref[0], preferred_element_type=F32).astype(BF16)
    for e in range(n_e):
        xe_ref[e] = xe[e * cap:(e + 1) * cap]
        as_ref[e] = jnp.sum(jnp.where(hits[e], aff_ref[0, e], 0.0), axis=-1, keepdims=True)


def _gather(pos, aff, h2, cap):
    bsz, E, L = pos.shape
    D = h2.shape[2]
    pos4 = pos.reshape(bsz, E, 1, L)
    aff4 = aff.reshape(bsz, E, 1, L)
    ge = 4
    return pl.pallas_call(
        functools.partial(_gather_kernel, cap=cap),
        grid=(bsz, E // ge),
        in_specs=[pl.BlockSpec((1, ge, 1, L), lambda b, e: (b, e, 0, 0)),
                  pl.BlockSpec((1, ge, 1, L), lambda b, e: (b, e, 0, 0)),
                  pl.BlockSpec((1, L, D), lambda b, e: (b, 0, 0))],
        out_specs=[pl.BlockSpec((ge, cap, D), lambda b, e: (e, b, 0)),
                   pl.BlockSpec((ge, cap, 1), lambda b, e: (e, b, 0))],
        out_shape=[jax.ShapeDtypeStruct((E, bsz * cap, D), BF16),
                   jax.ShapeDtypeStruct((E, bsz * cap, 1), F32)],
        compiler_params=_params(("arbitrary", "arbitrary")),
        name="gather",
    )(pos4, aff4, h2)


def _ffn_kernel(xe_ref, wg_ref, wu_ref, wd_ref, as_ref, ye_ref, acc_ref):
    f = pl.program_id(1)

    @pl.when(f == 0)
    def _():
        acc_ref[...] = jnp.zeros_like(acc_ref)

    x = xe_ref[0]
    g = jnp.dot(x, wg_ref[0].astype(BF16), preferred_element_type=F32)
    up = jnp.dot(x, wu_ref[0].astype(BF16), preferred_element_type=F32)
    hid = (g * _sigmoid(g) * up).astype(BF16)
    acc_ref[...] += jnp.dot(hid, wd_ref[0].astype(BF16), preferred_element_type=F32)

    @pl.when(f == pl.num_programs(1) - 1)
    def _():
        bsz, _, cap, _ = ye_ref.shape
        y = (acc_ref[...] * as_ref[0]).astype(BF16)
        for b in range(bsz):
            ye_ref[b, 0] = y[b * cap:(b + 1) * cap]


def _ffn(xe, w_gate, w_up, w_down, aff_slot, bsz):
    E, R, D = xe.shape
    F = w_gate.shape[2]
    cap = R // bsz
    tf = 256
    return pl.pallas_call(
        _ffn_kernel,
        grid=(E, F // tf),
        in_specs=[pl.BlockSpec((1, R, D), lambda e, f: (e, 0, 0)),
                  pl.BlockSpec((1, D, tf), lambda e, f: (e, 0, f)),
                  pl.BlockSpec((1, D, tf), lambda e, f: (e, 0, f)),
                  pl.BlockSpec((1, tf, D), lambda e, f: (e, f, 0)),
                  pl.BlockSpec((1, R, 1), lambda e, f: (e, 0, 0))],
        out_specs=pl.BlockSpec((bsz, 1, cap, D), lambda e, f: (0, e, 0, 0)),
        out_shape=jax.ShapeDtypeStruct((bsz, E, cap, D), BF16),
        scratch_shapes=[pltpu.VMEM((R, D), F32)],
        compiler_params=_params(("arbitrary", "arbitrary")),
        name="ffn",
    )(xe, w_gate, w_up, w_down, aff_slot)


def _combine_kernel(pos_ref, ye_ref, x1_ref, gt_ref, g_ref, o_ref, *, final_norm):
    pos_t = pos_ref[0]
    n_e, cap, d = ye_ref.shape[1:]
    slot = lax.broadcasted_iota(I32, (pos_t.shape[0], cap), 1)
    onehot = jnp.concatenate([jnp.where(pos_t[:, e:e + 1] == slot, 1.0, 0.0).astype(BF16) for e in range(n_e)],
                             axis=-1)
    moe = jnp.dot(onehot, ye_ref[0].reshape(n_e * cap, d), preferred_element_type=F32)
    o = x1_ref[0] + gt_ref[0] * moe
    o_ref[0] = _rms(o, g_ref[...]) if final_norm else o


def _combine(pos_t, ye, x1, gt2, g_final, final_norm):
    bsz, L, E = pos_t.shape
    D = x1.shape[2]
    cap = ye.shape[2]
    tl = 512
    return pl.pallas_call(
        functools.partial(_combine_kernel, final_norm=final_norm),
        grid=(bsz, L // tl),
        in_specs=[pl.BlockSpec((1, tl, E), lambda b, i: (b, i, 0)),
                  pl.BlockSpec((1, E, cap, D), lambda b, i: (b, 0, 0, 0), pipeline_mode=pl.Buffered(1)),
                  pl.BlockSpec((1, tl, D), lambda b, i: (b, i, 0)),
                  pl.BlockSpec((1, 1, D), lambda b, i: (b, 0, 0)),
                  pl.BlockSpec((1, D), lambda b, i: (0, 0))],
        out_specs=pl.BlockSpec((1, tl, D), lambda b, i: (b, i, 0)),
        out_shape=jax.ShapeDtypeStruct((bsz, L, D), F32),
        compiler_params=_params(("arbitrary", "arbitrary")),
        name="combine",
    )(pos_t, ye, x1, gt2, g_final)


def kernel(x, c, w_ada, b_ada, g_mix, w_in, ssm_a_re, ssm_a_im, ssm_log_dt, ssm_b_re, ssm_b_im,
           ssm_c_re, ssm_c_im, ssm_d, w_glu, b_glu, rpb, g_ssm_out, g_attn_out, w_out, g_ffn,
           w_router, w_gate, w_up, w_down, g_final):
    bsz, L, D = x.shape
    depth = w_ada.shape[0]
    d_ssm = ssm_d.shape[1]
    E = w_router.shape[2]
    cap = CAPACITY_FACTOR * L // E
    rows = L // GRID_W
    c8 = jnp.zeros((8, D), F32).at[:bsz].set(c)
    row = lambda v: v.reshape(1, -1)
    for layer in range(depth):
        mod = _ada(c8, w_ada[layer], row(b_ada[layer]))[:bsz]
        sh1, sc1, gt1, sh2, sc2, gt2 = [m.reshape(bsz, 1, D) for m in jnp.split(mod, 6, axis=-1)]

        u, qkv = _inproj(x, sh1, sc1, row(g_mix[layer]), _cast_bf16(w_in[layer]), d_ssm)
        ops = _ssm_operators(ssm_a_re[layer], ssm_a_im[layer], ssm_log_dt[layer], ssm_b_re[layer],
                             ssm_b_im[layer], ssm_c_re[layer], ssm_c_im[layer])
        y_ssm = _ssm(u.reshape(-1, d_ssm), row(ssm_d[layer]), ops, bsz).reshape(u.shape)
        y_attn = _attn(qkv, *_attn_tables(rpb[layer], rows))
        x1, h2, logits = _mix(y_ssm, _cast_bf16(w_glu[layer]), row(b_glu[layer]),
                              row(g_ssm_out[layer]), y_attn, row(g_attn_out[layer]), _cast_bf16(w_out[layer]),
                              x, gt1, row(g_ffn[layer]), sh2, sc2, w_router[layer])

        aff, pos = _topk(jnp.swapaxes(logits, 1, 2), cap)
        xe, aff_slot = _gather(pos, aff, h2, cap)
        ye = _ffn(xe, w_gate[layer], w_up[layer], w_down[layer], aff_slot, bsz)
        x = _combine(jnp.swapaxes(pos, 1, 2), ye, x1, gt2, row(g_final), layer == depth - 1)
    return x
```

```python
import functools
import math

import jax
import jax.numpy as jnp
from jax import lax
from jax.experimental import pallas as pl
from jax.experimental.pallas import tpu as pltpu

F32 = jnp.float32
BF16 = jnp.bfloat16
I32 = jnp.int32

EPS = 1e-6
GRID_W = 64
SSM_GROUP = 16
HEAD_DIM = 64
WIN_H_MAX = 8
WIN_W = 16
CAPACITY_FACTOR = 2
NEG = -1e30
LOG2E = math.log2(math.e)

SSM_T = 16
ATT_ROWS = 4
ATT_WIN = 12

LANES = 128
V7X_VMEM_BYTES = 64 * 1024 * 1024
VMEM_LIMIT = V7X_VMEM_BYTES // 8 * 7


def _params(sem, vmem=VMEM_LIMIT):
    return pltpu.CompilerParams(dimension_semantics=sem, vmem_limit_bytes=vmem)


def _sigmoid(x):
    return 1.0 / (1.0 + jnp.exp(-x))


def _rms(x, g):
    return x * lax.rsqrt(jnp.mean(x * x, axis=-1, keepdims=True) + EPS) * g


def _ada_kernel(c_ref, w_ref, b_ref, o_ref):
    c = c_ref[...]
    ca = (c * _sigmoid(c)).astype(BF16)
    o_ref[...] = jnp.dot(ca, w_ref[...].astype(BF16), preferred_element_type=F32) + b_ref[...]


def _ada(c8, w, b):
    k, n = w.shape
    tn = 1024
    return pl.pallas_call(
        _ada_kernel,
        grid=(n // tn,),
        in_specs=[pl.BlockSpec((8, k), lambda j: (0, 0)),
                  pl.BlockSpec((k, tn), lambda j: (0, j)),
                  pl.BlockSpec((1, tn), lambda j: (0, j))],
        out_specs=pl.BlockSpec((8, tn), lambda j: (0, j)),
        out_shape=jax.ShapeDtypeStruct((8, n), F32),
        compiler_params=_params(("arbitrary",)),
        name="ada",
    )(c8, w, b)


def _cast_kernel(w_ref, o_ref):
    o_ref[...] = w_ref[...].astype(BF16)


def _cast_bf16(w):
    k, n = w.shape
    tk = 256
    return pl.pallas_call(
        _cast_kernel,
        grid=(k // tk,),
        in_specs=[pl.BlockSpec((tk, n), lambda i: (i, 0))],
        out_specs=pl.BlockSpec((tk, n), lambda i: (i, 0)),
        out_shape=jax.ShapeDtypeStruct((k, n), BF16),
        compiler_params=_params(("arbitrary",)),
        name="cast",
    )(w)


def _inproj_kernel(x_ref, sh_ref, sc_ref, g_ref, w_ref, u_ref, qkv_ref, *, d_ssm):
    h = (_rms(x_ref[0], g_ref[...]) * (1.0 + sc_ref[0]) + sh_ref[0]).astype(BF16)
    n_total = w_ref.shape[1]
    for n in range(n_total // d_ssm):
        r = jnp.dot(h, w_ref[:, n * d_ssm:(n + 1) * d_ssm], preferred_element_type=F32)
        if n == 0:
            for cc in range(u_ref.shape[0]):
                u_ref[cc, 0] = r[cc * SSM_T:(cc + 1) * SSM_T]
        else:
            if n == 1:
                r = r * (HEAD_DIM ** -0.5 * LOG2E)
            qkv_ref[0, :, (n - 1) * d_ssm:n * d_ssm] = r.astype(BF16)


def _inproj(x, sh, sc, g, w_bf, d_ssm):
    bsz, L, D = x.shape
    n = w_bf.shape[1]
    tm = 512
    T = SSM_T
    return pl.pallas_call(
        functools.partial(_inproj_kernel, d_ssm=d_ssm),
        grid=(bsz, L // tm),
        in_specs=[pl.BlockSpec((1, tm, D), lambda b, i: (b, i, 0)),
                  pl.BlockSpec((1, 1, D), lambda b, i: (b, 0, 0)),
                  pl.BlockSpec((1, 1, D), lambda b, i: (b, 0, 0)),
                  pl.BlockSpec((1, D), lambda b, i: (0, 0)),
                  pl.BlockSpec((D, n), lambda b, i: (0, 0), pipeline_mode=pl.Buffered(1))],
        out_specs=[pl.BlockSpec((tm // T, 1, T, d_ssm), lambda b, i: (i, b, 0, 0)),
                   pl.BlockSpec((1, tm, n - d_ssm), lambda b, i: (b, i, 0))],
        out_shape=[jax.ShapeDtypeStruct((L // T, bsz, T, d_ssm), F32),
                   jax.ShapeDtypeStruct((bsz, L, n - d_ssm), BF16)],
        compiler_params=_params(("arbitrary", "arbitrary")),
        name="inproj",
    )(x, sh, sc, g, w_bf)


def _ssm_operators(a_re, a_im, log_dt, b_re, b_im, c_re, c_im):
    T = SSM_T
    G, P = a_re.shape[1], a_re.shape[2]
    H = b_re.shape[3]
    dt = jnp.exp(log_dt)[..., None]
    mag = jnp.exp(a_re * dt)
    lbr = mag * jnp.cos(a_im * dt)
    lbi = mag * jnp.sin(a_im * dt)
    den = a_re * a_re + a_im * a_im
    nr = lbr - 1.0
    fr = ((nr * a_re + lbi * a_im) / den)[:, :, None, :]
    fi = ((lbi * a_re - nr * a_im) / den)[:, :, None, :]
    b_re_t, b_im_t = jnp.swapaxes(b_re, 2, 3), jnp.swapaxes(b_im, 2, 3)
    bbr = fr * b_re_t - fi * b_im_t
    bbi = fr * b_im_t + fi * b_re_t
    prs, pis = [jnp.ones_like(lbr)], [jnp.zeros_like(lbi)]
    for _ in range(T):
        pr, pi = prs[-1], pis[-1]
        prs.append(pr * lbr - pi * lbi)
        pis.append(pr * lbi + pi * lbr)
    pr = jnp.stack(prs, axis=2)
    pi = jnp.stack(pis, axis=2)
    assert H == T
    rev = lambda x: x[:, ::-1]
    sections = [bbr[0], bbi[0], bbr[1], bbi[1], c_re[0], c_im[0], c_re[1], c_im[1],
                rev(pr[0, :, :T]), rev(pi[0, :, :T]), pr[1, :, :T], pi[1, :, :T],
                pr[0, :, 1:], pi[0, :, 1:], rev(pr[1, :, 1:]), rev(pi[1, :, 1:]),
                pr[0, :, :T], pi[0, :, :T]]
    m, wpair, vtpair = _ssm_prep(jnp.concatenate(sections, axis=1))
    a4 = jnp.stack([pr[0, :, T], pi[0, :, T], pr[1, :, T], pi[1, :, T]])
    a_chunk = jnp.transpose(a4.reshape(4, G // 2, 2, P), (1, 0, 2, 3)).reshape(1, G // 2 * 8 * P)
    return m, wpair, vtpair, a_chunk


def _nt_dot(a, b):
    return lax.dot_general(a, b, (((1,), (1,)), ((), ())), preferred_element_type=F32)


def _ssm_prep_kernel(pk_ref, m_ref, w_ref, vt_ref, mats_ref, rows_ref, kmats_ref, krows_ref):
    T = SSM_T
    H = T
    kw = T * H
    half = kmats_ref.shape[2] // 2
    lane = lax.broadcasted_iota(I32, (1, 2 * half), 1)
    cat = lambda xs: jnp.concatenate(xs, axis=-1)

    def group(g, carry):
        (bbr_f, bbi_f, bbr_b, bbi_b, cre_f, cim_f, cre_b, cim_b, qrf, qif, qrb, qib,
         rrf, rif, rrb, rib, krf, kif) = [pk_ref[g, i * T:(i + 1) * T, :] for i in range(18)]
        zero = jnp.zeros_like(bbr_f)
        even = g % 2 == 0

        def slots(v4):
            ev = cat([v4[0], zero, v4[1], zero, v4[2], zero, v4[3], zero])
            od = cat([zero, v4[0], zero, v4[1], zero, v4[2], zero, v4[3]])
            return jnp.where(even, ev, od)

        mats_ref[0] = slots([bbr_f, bbi_f, bbr_b, bbi_b])
        mats_ref[1] = slots([-bbi_f, bbr_f, -bbi_b, bbr_b])
        mats_ref[2] = slots([cre_f, -cim_f, cre_b, -cim_b])
        mats_ref[3] = slots([-cim_f, -cre_f, -cim_b, -cre_b])
        rows_ref[0] = cat([qrf] * 4 + [qrb] * 4)
        rows_ref[1] = cat([qif] * 4 + [qib] * 4)
        rows_ref[2] = cat([rrf] * 4 + [rrb] * 4)
        rows_ref[3] = cat([rif] * 4 + [rib] * 4)
        kmats_ref[0] = cat([bbr_f, -bbi_f, bbr_b, -bbi_b])
        kmats_ref[1] = cat([cre_f, cre_f, cre_b, cre_b])
        kmats_ref[2] = cat([-cim_f, cim_f, -cim_b, cim_b])
        krows_ref[0] = cat([krf, kif, qrb, qib])
        krows_ref[1] = cat([kif, krf, qib, qrb])

        pair, row0 = g // 2, (g % 2) * kw
        aw, bw, av, bv = [mats_ref[i] for i in range(4)]
        for s in range(T):
            r = pl.ds(pl.multiple_of(row0 + s * H, H), H)
            w_ref[pair, r, :] = (aw * rows_ref[0, pl.ds(s, 1), :] + bw * rows_ref[1, pl.ds(s, 1), :]).astype(BF16)
            vt_ref[pair, r, :] = (av * rows_ref[2, pl.ds(s, 1), :] + bv * rows_ref[3, pl.ds(s, 1), :]).astype(BF16)
        lhs, a4, b4 = [kmats_ref[i] for i in range(3)]
        blocks = []
        for i in range(2 * T):
            k = abs(i - (T - 1))
            if i == 2 * T - 1:
                blocks.append(jnp.zeros_like(a4))
                continue
            cp = a4 * krows_ref[0, pl.ds(k, 1), :] + b4 * krows_ref[1, pl.ds(k, 1), :]
            if i < T - 1:
                cp = jnp.where(lane >= half, cp, 0.0)
            elif i > T - 1:
                cp = jnp.where(lane < half, cp, 0.0)
            blocks.append(cp)
        rhs = jnp.concatenate(blocks, axis=0)
        lhs_hi = lhs.astype(BF16)
        lhs_lo = (lhs - lhs_hi.astype(F32)).astype(BF16)
        rhs_hi = rhs.astype(BF16)
        rhs_lo = (rhs - rhs_hi.astype(F32)).astype(BF16)
        z = _nt_dot(lhs_hi, rhs_hi) + _nt_dot(lhs_lo, rhs_hi) + _nt_dot(lhs_hi, rhs_lo)
        for s in range(T):
            off = (T - 1 - s) * H
            m_ref[g, s * H:(s + 1) * H, :] = z[:, off:off + kw].astype(BF16)
        return carry

    lax.fori_loop(0, pk_ref.shape[0], group, 0)


def _ssm_prep(packed):
    G, n_rows, P = packed.shape
    T = H = SSM_T
    gs = LANES // SSM_GROUP
    return pl.pallas_call(
        _ssm_prep_kernel,
        grid=(G // gs,),
        in_specs=[pl.BlockSpec((gs, n_rows, P), lambda j: (j, 0, 0))],
        out_specs=[pl.BlockSpec((gs, T * H, T * H), lambda j: (j, 0, 0)),
                   pl.BlockSpec((gs // 2, 2 * T * H, 8 * P), lambda j: (j, 0, 0)),
                   pl.BlockSpec((gs // 2, 2 * T * H, 8 * P), lambda j: (j, 0, 0))],
        out_shape=[jax.ShapeDtypeStruct((G, T * H, T * H), BF16),
                   jax.ShapeDtypeStruct((G // 2, 2 * T * H, 8 * P), BF16),
                   jax.ShapeDtypeStruct((G // 2, 2 * T * H, 8 * P), BF16)],
        scratch_shapes=[pltpu.VMEM((4, H, 8 * P), F32), pltpu.VMEM((4, T, 8 * P), F32),
                        pltpu.VMEM((3, H, 4 * P), F32), pltpu.VMEM((2, T, 4 * P), F32)],
        compiler_params=_params(("arbitrary",)),
        name="ssm_prep",
    )(packed)


def _slab_permutation():
    idx = jnp.arange(8 * LANES)
    t, g, h = idx // LANES, (idx % LANES) // SSM_GROUP, idx % SSM_GROUP
    dst = g * LANES + t * SSM_GROUP + h
    return (dst[:, None] == jnp.arange(8 * LANES)[None, :]).astype(BF16)


def _chunk_scan(s_ref, a_ref, *, bsz, n_pairs):
    n_it = s_ref.shape[0] // (2 * bsz)
    cols = [[pl.ds((4 * j + q) * LANES, LANES) for q in range(4)] for j in range(n_pairs)]
    decay = [[a_ref[:, c] for c in cols[j]] for j in range(n_pairs)]

    def step(i, carry):
        rf = pl.ds(pl.multiple_of(i * 2 * bsz, 2 * bsz), 2 * bsz)
        rb = pl.ds(pl.multiple_of((n_it - 1 - i) * 2 * bsz, 2 * bsz), 2 * bsz)
        out = []
        for j in range(n_pairs):
            ar_f, ai_f, ar_b, ai_b = decay[j]
            xr, xi, yr, yi = carry[4 * j:4 * j + 4]
            sr, si = s_ref[rf, cols[j][0]], s_ref[rf, cols[j][1]]
            xr1 = ar_f * xr - ai_f * xi + sr[:bsz]
            xi1 = ar_f * xi + ai_f * xr + si[:bsz]
            s_ref[rf, cols[j][0]] = jnp.concatenate([xr, xr1], axis=0)
            s_ref[rf, cols[j][1]] = jnp.concatenate([xi, xi1], axis=0)
            xr2 = ar_f * xr1 - ai_f * xi1 + sr[bsz:]
            xi2 = ar_f * xi1 + ai_f * xr1 + si[bsz:]
            tr, ti = s_ref[rb, cols[j][2]], s_ref[rb, cols[j][3]]
            yr1 = ar_b * yr - ai_b * yi + tr[bsz:]
            yi1 = ar_b * yi + ai_b * yr + ti[bsz:]
            s_ref[rb, cols[j][2]] = jnp.concatenate([yr1, yr], axis=0)
            s_ref[rb, cols[j][3]] = jnp.concatenate([yi1, yi], axis=0)
            yr2 = ar_b * yr1 - ai_b * yi1 + tr[:bsz]
            yi2 = ar_b * yi1 + ai_b * yr1 + ti[:bsz]
            out += [xr2, xi2, yr2, yi2]
        return tuple(out)

    z = jnp.zeros((bsz, LANES), F32)
    lax.fori_loop(0, n_it, step, (z,) * (4 * n_pairs))


def _ssm_kernel(u_ref, d_ref, perm_ref, iperm_ref, w_ref, m_ref, v_ref, a_ref, y_ref,
                ug_ref, s_ref, yg_ref, *, bsz):
    T = SSM_T
    R = u_ref.shape[0] // T
    n_groups = LANES // SSM_GROUP
    kw = T * SSM_GROUP
    for tq in range(T // 8):
        cat = jnp.concatenate([u_ref[pl.ds(tq * 8 + t, R, stride=T), :].astype(BF16) for t in range(8)], axis=-1)
        grouped = jnp.dot(cat, perm_ref[...], preferred_element_type=F32).astype(BF16)
        for g in range(n_groups):
            ug_ref[g, :, tq * LANES:(tq + 1) * LANES] = grouped[:, g * LANES:(g + 1) * LANES]
    for pr in range(n_groups // 2):
        u2 = jnp.concatenate([ug_ref[2 * pr], ug_ref[2 * pr + 1]], axis=-1)
        s_ref[:, pr * 4 * LANES:(pr + 1) * 4 * LANES] = jnp.dot(u2, w_ref[pr], preferred_element_type=F32)
    _chunk_scan(s_ref, a_ref, bsz=bsz, n_pairs=n_groups // 2)
    for pr in range(n_groups // 2):
        inter = _nt_dot(s_ref[:, pr * 4 * LANES:(pr + 1) * 4 * LANES].astype(BF16), v_ref[pr])
        for gl in range(2):
            g = 2 * pr + gl
            yg_ref[g] = (jnp.dot(ug_ref[g], m_ref[g], preferred_element_type=F32)
                         + inter[:, gl * kw:(gl + 1) * kw])
    for tq in range(T // 8):
        ycat = jnp.concatenate([yg_ref[g, :, tq * LANES:(tq + 1) * LANES] for g in range(n_groups)], axis=-1)
        back = jnp.dot(ycat.astype(BF16), iperm_ref[...], preferred_element_type=F32)
        for t in range(8):
            rows = pl.ds(tq * 8 + t, R, stride=T)
            y_ref[rows, :] = back[:, t * LANES:(t + 1) * LANES] + d_ref[...] * u_ref[rows, :]


def _ssm(u_tok, d_skip, ops, bsz):
    m, wpair, vpair, a_chunk = ops
    N, d_ssm = u_tok.shape
    T = SSM_T
    R = N // T
    n_groups = LANES // SSM_GROUP
    perm = _slab_permutation()
    const = lambda shape: pl.BlockSpec(shape, lambda j: (0,) * len(shape))
    return pl.pallas_call(
        functools.partial(_ssm_kernel, bsz=bsz),
        grid=(d_ssm // LANES,),
        in_specs=[pl.BlockSpec((N, LANES), lambda j: (0, j)),
                  pl.BlockSpec((1, LANES), lambda j: (0, j)),
                  const(perm.shape), const(perm.shape),
                  pl.BlockSpec((n_groups // 2,) + wpair.shape[1:], lambda j: (j, 0, 0)),
                  pl.BlockSpec((n_groups,) + m.shape[1:], lambda j: (j, 0, 0)),
                  pl.BlockSpec((n_groups // 2,) + vpair.shape[1:], lambda j: (j, 0, 0)),
                  pl.BlockSpec((1, n_groups // 2 * 4 * LANES), lambda j: (0, j))],
        out_specs=pl.BlockSpec((N, LANES), lambda j: (0, j)),
        out_shape=jax.ShapeDtypeStruct((N, d_ssm), F32),
        scratch_shapes=[pltpu.VMEM((n_groups, R, T * SSM_GROUP), BF16),
                        pltpu.VMEM((R, n_groups // 2 * 4 * LANES), F32),
                        pltpu.VMEM((n_groups, R, T * SSM_GROUP), F32)],
        compiler_params=_params(("arbitrary",)),
        name="ssm",
    )(u_tok, d_skip, perm, perm.T, wpair, m, vpair, a_chunk)


def _attn_tables(rpb, rows):
    kh = min(WIN_H_MAX, rows)
    H = rpb.shape[0]
    w = jnp.arange(GRID_W)
    col_start = jnp.clip(w - WIN_W // 2, 0, GRID_W - WIN_W)
    col_mask = (w[None, :] >= col_start[:, None]) & (w[None, :] < col_start[:, None] + WIN_W)
    dc = jnp.clip(w[None, :] - w[:, None] + (WIN_W - 1), 0, 2 * WIN_W - 2)
    col_sel = (dc[:, :, None] == jnp.arange(2 * WIN_W - 1)[None, None, :]).astype(F32)
    tile = jnp.einsum('hdc,qkc->hdqk', rpb, col_sel, precision=lax.Precision.HIGHEST)
    tile = jnp.where(col_mask[None, None], tile * LOG2E, NEG)
    lo_pad, hi_pad = ATT_ROWS, ATT_WIN - kh + 1
    tile = jnp.pad(tile, ((0, 0), (lo_pad, hi_pad), (0, 0), (0, 0)))
    n_dr = tile.shape[1] - 1
    tiles = jnp.concatenate([tile[:, :-1], tile[:, 1:]], axis=-1).reshape(H // 2, 2, n_dr, GRID_W, 2 * GRID_W)
    i = jnp.arange(ATT_ROWS)
    last_union = rows - ATT_WIN
    last_rel = (rows - kh) - last_union
    rel = jnp.stack([jnp.zeros_like(i), i, jnp.full_like(i, last_rel)])
    j = jnp.arange(ATT_WIN)
    valid = (j[None, None, :] >= rel[:, :, None]) & (j[None, None, :] < rel[:, :, None] + kh)
    masks = jnp.where(valid, 0.0, NEG).astype(F32)
    masks = jnp.repeat(masks.reshape(3 * ATT_ROWS * ATT_WIN // 2, 2), GRID_W, axis=-1)
    return tiles, masks.reshape(-1, 1, 2 * GRID_W)


def _attn_kernel(q_ref, k_ref, v_ref, t_ref, m_ref, o_ref, *, rows, kh):
    n_blocks = rows // ATT_ROWS
    nq = ATT_ROWS * GRID_W
    nk = ATT_WIN * GRID_W
    n_pairs = ATT_WIN // 2
    pw = 2 * GRID_W
    lane = lax.broadcasted_iota(I32, (1, 2 * HEAD_DIM), 1)
    head_lanes = [lane < HEAD_DIM, lane >= HEAD_DIM]
    rel = [[0] * ATT_ROWS, list(range(ATT_ROWS)), [(rows - kh) - (rows - ATT_WIN)] * ATT_ROWS]

    def geometry(rb):
        pattern = 0 if rb == 0 else (2 if rb == n_blocks - 1 else 1)
        union = min(max(rb * ATT_ROWS - kh // 2, 0), rows - ATT_WIN)
        dr0 = union - rb * ATT_ROWS + (WIN_H_MAX - 1) + ATT_ROWS
        return pattern, dr0, pl.ds(rb * nq, nq), pl.ds(union * GRID_W, nk)

    def scores(rb, hl):
        _, _, q_rows, k_rows = geometry(rb)
        qm = jnp.where(head_lanes[hl], q_ref[0, q_rows, :], 0.0).astype(BF16)
        return _nt_dot(qm, k_ref[0, k_rows, :])

    def softmax(rb, hl, s):
        pattern, dr0, _, _ = geometry(rb)
        probs, denoms = [], []
        for i in range(ATT_ROWS):
            first, last = rel[pattern][i], rel[pattern][i] + kh - 1
            pairs = range(first // 2, last // 2 + 1)
            bias = []
            for jp in pairs:
                tile = t_ref[0, hl, dr0 + 2 * jp - i]
                if 2 * jp < first or 2 * jp + 1 > last:
                    tile = tile + m_ref[(pattern * ATT_ROWS + i) * n_pairs + jp]
                bias.append(tile)
            si = s[i * GRID_W:(i + 1) * GRID_W, pairs[0] * pw:(pairs[-1] + 1) * pw] + jnp.concatenate(bias, axis=-1)
            pi = jnp.exp2(si - jnp.max(si, axis=-1, keepdims=True))
            denoms.append(jnp.sum(pi, axis=-1, keepdims=True))
            pieces = [pi.astype(BF16)]
            if pairs[0] > 0:
                pieces.insert(0, jnp.zeros((GRID_W, pairs[0] * pw), BF16))
            if pairs[-1] + 1 < n_pairs:
                pieces.append(jnp.zeros((GRID_W, (n_pairs - 1 - pairs[-1]) * pw), BF16))
            probs.append(jnp.concatenate(pieces, axis=-1))
        return jnp.concatenate(probs, axis=0), jnp.concatenate(denoms, axis=0)

    def weighted_values(rb, hl, p, denom):
        _, _, _, k_rows = geometry(rb)
        vm = jnp.where(head_lanes[hl], v_ref[0, k_rows, :], 0.0).astype(BF16)
        return jnp.dot(p, vm, preferred_element_type=F32) / denom

    passes = [(rb, hl) for rb in range(n_blocks) for hl in range(2)]
    ahead = 1
    queue = [scores(*passes[k]) for k in range(ahead)]
    pending = None
    partial = {}

    def finish(item):
        (rb, hl), p, denom = item
        o = weighted_values(rb, hl, p, denom)
        if hl == 0:
            partial[rb] = o
        else:
            o_ref[0, geometry(rb)[2], :] = partial.pop(rb) + o

    for n, (rb, hl) in enumerate(passes):
        s = queue.pop(0)
        if n + ahead < len(passes):
            queue.append(scores(*passes[n + ahead]))
        p, denom = softmax(rb, hl, s)
        if pending is not None:
            finish(pending)
        pending = ((rb, hl), p, denom)
    finish(pending)


def _attn(qkv, tiles, masks):
    bsz, L, n3 = qkv.shape
    d = n3 // 3
    rows = L // GRID_W
    kh = min(WIN_H_MAX, rows)
    assert rows % ATT_ROWS == 0 and rows // ATT_ROWS >= 3 and ATT_ROWS == kh // 2
    assert ATT_WIN % 2 == 0 and kh + ATT_ROWS - 1 <= ATT_WIN <= rows
    n_hp = d // (2 * HEAD_DIM)
    blk = pl.BlockSpec((1, L, 2 * HEAD_DIM), lambda b, hp: (b, 0, hp))
    return pl.pallas_call(
        functools.partial(_attn_kernel, rows=rows, kh=kh),
        grid=(bsz, n_hp),
        in_specs=[blk,
                  pl.BlockSpec((1, L, 2 * HEAD_DIM), lambda b, hp: (b, 0, n_hp + hp)),
                  pl.BlockSpec((1, L, 2 * HEAD_DIM), lambda b, hp: (b, 0, 2 * n_hp + hp)),
                  pl.BlockSpec((1,) + tiles.shape[1:], lambda b, hp: (hp, 0, 0, 0, 0)),
                  pl.BlockSpec(masks.shape, lambda b, hp: (0, 0, 0))],
        out_specs=blk,
        out_shape=jax.ShapeDtypeStruct((bsz, L, d), F32),
        compiler_params=_params(("arbitrary", "arbitrary")),
        name="attn",
    )(qkv, qkv, qkv, tiles, masks)


def _mix_kernel(ys_ref, wglu_ref, bglu_ref, gs_ref, ya_ref, ga_ref, wout_ref,
                x_ref, gt_ref, gf_ref, sh_ref, sc_ref, wr_ref,
                x1_ref, h2_ref, lg_ref):
    d_ssm = ys_ref.shape[3]
    y = ys_ref[:, 0].reshape(-1, d_ssm)
    z = jax.nn.gelu(y)
    gate = _sigmoid(jnp.dot(z.astype(BF16), wglu_ref[...], preferred_element_type=F32) + bglu_ref[...])
    a = _rms(z * gate, gs_ref[...]).astype(BF16)
    t = _rms(ya_ref[0], ga_ref[...]).astype(BF16)
    mixed = (jnp.dot(a, wout_ref[:d_ssm, :], preferred_element_type=F32)
             + jnp.dot(t, wout_ref[d_ssm:, :], preferred_element_type=F32))
    x1 = x_ref[0] + gt_ref[0] * mixed
    x1_ref[0] = x1
    h2 = _rms(x1, gf_ref[...]) * (1.0 + sc_ref[0]) + sh_ref[0]
    h2_hi = h2.astype(BF16)
    h2_ref[0] = h2_hi
    h2_lo = (h2 - h2_hi.astype(F32)).astype(BF16)
    wr = wr_ref[...]
    wr_hi = wr.astype(BF16)
    wr_lo = (wr - wr_hi.astype(F32)).astype(BF16)
    n_e = wr.shape[1]
    tm = h2.shape[0]
    r = jnp.dot(jnp.concatenate([h2_hi, h2_lo], axis=0), jnp.concatenate([wr_hi, wr_lo], axis=1),
                preferred_element_type=F32)
    lg_ref[0] = r[:tm, :n_e] + r[:tm, n_e:] + r[tm:, :n_e]


def _mix(ys, wglu_bf, b_glu, g_ssm, ya, g_attn, wout_bf, x, gt1, g_ffn, sh2, sc2, w_router):
    bsz, L, D = x.shape
    d_ssm = ys.shape[3]
    E = w_router.shape[1]
    tm = 512
    T = SSM_T
    tile = lambda n: pl.BlockSpec((1, tm, n), lambda b, i: (b, i, 0))
    per_b = lambda n: pl.BlockSpec((1, 1, n), lambda b, i: (b, 0, 0))
    full = lambda r, n: pl.BlockSpec((r, n), lambda b, i: (0, 0), pipeline_mode=pl.Buffered(1))
    return pl.pallas_call(
        _mix_kernel,
        grid=(bsz, L // tm),
        in_specs=[pl.BlockSpec((tm // T, 1, T, d_ssm), lambda b, i: (i, b, 0, 0)),
                  full(d_ssm, d_ssm), full(1, d_ssm), full(1, d_ssm),
                  tile(d_ssm), full(1, d_ssm), full(D, D),
                  tile(D), per_b(D), full(1, D), per_b(D), per_b(D), full(D, E)],
        out_specs=[tile(D), tile(D), tile(E)],
        out_shape=[jax.ShapeDtypeStruct((bsz, L, D), F32),
                   jax.ShapeDtypeStruct((bsz, L, D), BF16),
                   jax.ShapeDtypeStruct((bsz, L, E), F32)],
        compiler_params=_params(("arbitrary", "arbitrary")),
        name="mix",
    )(ys, wglu_bf, b_glu, g_ssm, ya, g_attn, wout_bf, x, gt1, g_ffn, sh2, sc2, w_router)


TOPK_BISECTIONS = 160


def _topk_kernel(lg_ref, aff_ref, pos_ref, *, cap):
    lg = lg_ref[...]
    bsz, E, L = lg.shape
    e = jnp.exp(lg - jnp.max(lg, axis=1, keepdims=True))
    aff3 = e / jnp.sum(e, axis=1, keepdims=True)
    aff_ref[...] = aff3
    aff = aff3.reshape(bsz * E, L)

    def count(mask):
        return jnp.sum(jnp.where(mask, 1.0, 0.0), axis=-1, keepdims=True)

    def midpoint(lo, hi):
        return lo + (hi - lo) * 0.5

    def unresolved(carry):
        lo, hi, it = carry
        mid = midpoint(lo, hi)
        open_rows = jnp.max(jnp.where((mid > lo) & (mid < hi), 1.0, 0.0))
        return jnp.logical_and(it < TOPK_BISECTIONS, open_rows > 0.0)

    def halve(carry):
        lo, hi, it = carry
        mid = midpoint(lo, hi)
        ok = count(aff >= mid) >= cap
        return jnp.where(ok, mid, lo), jnp.where(ok, hi, mid), it + 1

    lo0 = jnp.zeros((bsz * E, 1), F32)
    thr, _, _ = lax.while_loop(unresolved, halve, (lo0, lo0 + 2.0, jnp.int32(0)))
    gt = aff > thr
    eq = aff == thr
    need = cap - count(gt)
    blk = LANES
    tri = jnp.where(lax.broadcasted_iota(I32, (blk, blk), 0) < lax.broadcasted_iota(I32, (blk, blk), 1),
                    1.0, 0.0).astype(BF16)

    def prefix_count(mask):
        ones = jnp.where(mask, 1.0, 0.0)
        run = jnp.zeros((bsz * E, 1), F32)
        outs = []
        for j in range(L // blk):
            piece = ones[:, j * blk:(j + 1) * blk]
            outs.append(jnp.dot(piece.astype(BF16), tri, preferred_element_type=F32) + run)
            run = run + jnp.sum(piece, axis=-1, keepdims=True)
        return jnp.concatenate(outs, axis=-1)

    sel = gt | (eq & (prefix_count(eq) < need))
    pos = prefix_count(sel)
    pos_ref[...] = jnp.where(sel, pos.astype(I32), -1).reshape(bsz, E, L)


def _topk(lg_t, cap):
    bsz, E, L = lg_t.shape
    spec = pl.BlockSpec((bsz, E, L), lambda i: (0, 0, 0))
    return pl.pallas_call(
        functools.partial(_topk_kernel, cap=cap),
        grid=(1,),
        in_specs=[spec],
        out_specs=[spec, spec],
        out_shape=[jax.ShapeDtypeStruct((bsz, E, L), F32), jax.ShapeDtypeStruct((bsz, E, L), I32)],
        compiler_params=_params(("arbitrary",)),
        name="topk",
    )(lg_t)


def _gather_kernel(pos_ref, aff_ref, h_ref, xe_ref, as_ref, *, cap):
    L = h_ref.shape[1]
    n_e = pos_ref.shape[1]
    slot = lax.broadcasted_iota(I32, (cap, L), 0)
    hits = [pos_ref[0, e] == slot for e in range(n_e)]
    onehot = jnp.concatenate([jnp.where(h, 1.0, 0.0).astype(BF16) for h in hits], axis=0)
    xe = jnp.dot(onehot, h_ref[0], preferred_element_type=F32).astype(BF16)
    for e in range(n_e):
        xe_ref[e] = xe[e * cap:(e + 1) * cap]
        as_ref[e] = jnp.sum(jnp.where(hits[e], aff_ref[0, e], 0.0), axis=-1, keepdims=True)


def _gather(pos, aff, h2, cap):
    bsz, E, L = pos.shape
    D = h2.shape[2]
    pos4 = pos.reshape(bsz, E, 1, L)
    aff4 = aff.reshape(bsz, E, 1, L)
    ge = 4
    return pl.pallas_call(
        functools.partial(_gather_kernel, cap=cap),
        grid=(bsz, E // ge),
        in_specs=[pl.BlockSpec((1, ge, 1, L), lambda b, e: (b, e, 0, 0)),
                  pl.BlockSpec((1, ge, 1, L), lambda b, e: (b, e, 0, 0)),
                  pl.BlockSpec((1, L, D), lambda b, e: (b, 0, 0))],
        out_specs=[pl.BlockSpec((ge, cap, D), lambda b, e: (e, b, 0)),
                   pl.BlockSpec((ge, cap, 1), lambda b, e: (e, b, 0))],
        out_shape=[jax.ShapeDtypeStruct((E, bsz * cap, D), BF16),
                   jax.ShapeDtypeStruct((E, bsz * cap, 1), F32)],
        compiler_params=_params(("arbitrary", "arbitrary")),
        name="gather",
    )(pos4, aff4, h2)


def _ffn_kernel(xe_ref, wg_ref, wu_ref, wd_ref, as_ref, ye_ref, hid_ref, *, n_hid, tf):
    s = pl.program_id(1)

    @pl.when(s < n_hid)
    def _():
        x = xe_ref[0]
        g = jnp.dot(x, wg_ref[0].astype(BF16), preferred_element_type=F32)
        up = jnp.dot(x, wu_ref[0].astype(BF16), preferred_element_type=F32)
        hid = (g * _sigmoid(g) * up).astype(BF16)
        for f in range(n_hid):
            @pl.when(s == f)
            def _():
                hid_ref[:, f * tf:(f + 1) * tf] = hid

    @pl.when(s >= n_hid)
    def _():
        bsz, _, cap, _ = ye_ref.shape
        y = jnp.dot(hid_ref[...], wd_ref[0].astype(BF16), preferred_element_type=F32)
        y = (y * as_ref[0]).astype(BF16)
        for b in range(bsz):
            ye_ref[b, 0] = y[b * cap:(b + 1) * cap]


def _ffn(xe, w_gate, w_up, w_down, aff_slot, bsz):
    E, R, D = xe.shape
    F = w_gate.shape[2]
    cap = R // bsz
    tf = 256
    tn = 512
    n_hid, n_out = F // tf, D // tn
    hid_tile = lambda e, s: (e, 0, jnp.minimum(s, n_hid - 1))
    out_tile = lambda s: jnp.maximum(s - n_hid, 0)
    return pl.pallas_call(
        functools.partial(_ffn_kernel, n_hid=n_hid, tf=tf),
        grid=(E, n_hid + n_out),
        in_specs=[pl.BlockSpec((1, R, D), lambda e, s: (e, 0, 0)),
                  pl.BlockSpec((1, D, tf), hid_tile),
                  pl.BlockSpec((1, D, tf), hid_tile),
                  pl.BlockSpec((1, F, tn), lambda e, s: (e, 0, out_tile(s))),
                  pl.BlockSpec((1, R, 1), lambda e, s: (e, 0, 0))],
        out_specs=pl.BlockSpec((bsz, 1, cap, tn), lambda e, s: (0, e, 0, out_tile(s))),
        out_shape=jax.ShapeDtypeStruct((bsz, E, cap, D), BF16),
        scratch_shapes=[pltpu.VMEM((R, F), BF16)],
        compiler_params=_params(("arbitrary", "arbitrary")),
        name="ffn",
    )(xe, w_gate, w_up, w_down, aff_slot)


def _combine_kernel(pos_ref, ye_ref, x1_ref, gt_ref, g_ref, o_ref, *, final_norm):
    pos_t = pos_ref[0]
    n_e, cap, d = ye_ref.shape[1:]
    slot = lax.broadcasted_iota(I32, (pos_t.shape[0], cap), 1)
    onehot = jnp.concatenate([jnp.where(pos_t[:, e:e + 1] == slot, 1.0, 0.0).astype(BF16) for e in range(n_e)],
                             axis=-1)
    moe = jnp.dot(onehot, ye_ref[0].reshape(n_e * cap, d), preferred_element_type=F32)
    o = x1_ref[0] + gt_ref[0] * moe
    o_ref[0] = _rms(o, g_ref[...]) if final_norm else o


def _combine(pos_t, ye, x1, gt2, g_final, final_norm):
    bsz, L, E = pos_t.shape
    D = x1.shape[2]
    cap = ye.shape[2]
    tl = 512
    return pl.pallas_call(
        functools.partial(_combine_kernel, final_norm=final_norm),
        grid=(bsz, L // tl),
        in_specs=[pl.BlockSpec((1, tl, E), lambda b, i: (b, i, 0)),
                  pl.BlockSpec((1, E, cap, D), lambda b, i: (b, 0, 0, 0), pipeline_mode=pl.Buffered(1)),
                  pl.BlockSpec((1, tl, D), lambda b, i: (b, i, 0)),
                  pl.BlockSpec((1, 1, D), lambda b, i: (b, 0, 0)),
                  pl.BlockSpec((1, D), lambda b, i: (0, 0))],
        out_specs=pl.BlockSpec((1, tl, D), lambda b, i: (b, i, 0)),
        out_shape=jax.ShapeDtypeStruct((bsz, L, D), F32),
        compiler_params=_params(("arbitrary", "arbitrary")),
        name="combine",
    )(pos_t, ye, x1, gt2, g_final)


def kernel(x, c, w_ada, b_ada, g_mix, w_in, ssm_a_re, ssm_a_im, ssm_log_dt, ssm_b_re, ssm_b_im,
           ssm_c_re, ssm_c_im, ssm_d, w_glu, b_glu, rpb, g_ssm_out, g_attn_out, w_out, g_ffn,
           w_router, w_gate, w_up, w_down, g_final):
    bsz, L, D = x.shape
    depth = w_ada.shape[0]
    d_ssm = ssm_d.shape[1]
    E = w_router.shape[2]
    cap = CAPACITY_FACTOR * L // E
    rows = L // GRID_W
    c8 = jnp.zeros((8, D), F32).at[:bsz].set(c)
    row = lambda v: v.reshape(1, -1)
    for layer in range(depth):
        mod = _ada(c8, w_ada[layer], row(b_ada[layer]))[:bsz]
        sh1, sc1, gt1, sh2, sc2, gt2 = [m.reshape(bsz, 1, D) for m in jnp.split(mod, 6, axis=-1)]

        u, qkv = _inproj(x, sh1, sc1, row(g_mix[layer]), _cast_bf16(w_in[layer]), d_ssm)
        ops = _ssm_operators(ssm_a_re[layer], ssm_a_im[layer], ssm_log_dt[layer], ssm_b_re[layer],
                             ssm_b_im[layer], ssm_c_re[layer], ssm_c_im[layer])
        y_ssm = _ssm(u.reshape(-1, d_ssm), row(ssm_d[layer]), ops, bsz).reshape(u.shape)
        y_attn = _attn(qkv, *_attn_tables(rpb[layer], rows))
        x1, h2, logits = _mix(y_ssm, _cast_bf16(w_glu[layer]), row(b_glu[layer]),
                              row(g_ssm_out[layer]), y_attn, row(g_attn_out[layer]), _cast_bf16(w_out[layer]),
                              x, gt1, row(g_ffn[layer]), sh2, sc2, w_router[layer])

        aff, pos = _topk(jnp.swapaxes(logits, 1, 2), cap)
        xe, aff_slot = _gather(pos, aff, h2, cap)
        ye = _ffn(xe, w_gate[layer], w_up[layer], w_down[layer], aff_slot, bsz)
        x = _combine(jnp.swapaxes(pos, 1, 2), ye, x1, gt2, row(g_final), layer == depth - 1)
    return x
```

```python
import functools
import math

import jax
import jax.numpy as jnp
from jax import lax
from jax.experimental import pallas as pl
from jax.experimental.pallas import tpu as pltpu

F32 = jnp.float32
BF16 = jnp.bfloat16
I32 = jnp.int32

EPS = 1e-6
GRID_W = 64
SSM_GROUP = 16
HEAD_DIM = 64
WIN_H_MAX = 8
WIN_W = 16
CAPACITY_FACTOR = 2
NEG = -1e30
LOG2E = math.log2(math.e)

SSM_T = 16
ATT_ROWS = 4
ATT_WIN = 12

LANES = 128
V7X_VMEM_BYTES = 64 * 1024 * 1024
VMEM_LIMIT = V7X_VMEM_BYTES // 8 * 7


def _params(sem, vmem=VMEM_LIMIT):
    return pltpu.CompilerParams(dimension_semantics=sem, vmem_limit_bytes=vmem)


def _sigmoid(x):
    return 1.0 / (1.0 + jnp.exp(-x))


def _rms(x, g):
    return x * lax.rsqrt(jnp.mean(x * x, axis=-1, keepdims=True) + EPS) * g


def _ada_kernel(c_ref, w_ref, b_ref, o_ref):
    c = c_ref[...]
    ca = (c * _sigmoid(c)).astype(BF16)
    o_ref[...] = jnp.dot(ca, w_ref[...].astype(BF16), preferred_element_type=F32) + b_ref[...]


def _ada(c8, w, b):
    k, n = w.shape
    tn = 1024
    return pl.pallas_call(
        _ada_kernel,
        grid=(n // tn,),
        in_specs=[pl.BlockSpec((8, k), lambda j: (0, 0)),
                  pl.BlockSpec((k, tn), lambda j: (0, j)),
                  pl.BlockSpec((1, tn), lambda j: (0, j))],
        out_specs=pl.BlockSpec((8, tn), lambda j: (0, j)),
        out_shape=jax.ShapeDtypeStruct((8, n), F32),
        compiler_params=_params(("arbitrary",)),
        name="ada",
    )(c8, w, b)


def _cast_all_kernel(*refs):
    n = len(refs) // 2
    for w_ref, o_ref in zip(refs[:n], refs[n:]):
        o_ref[...] = w_ref[...].astype(BF16)


def _cast_all_bf16(ws, steps=8):
    spec = lambda w: pl.BlockSpec((w.shape[0] // steps, w.shape[1]), lambda i: (i, 0))
    return pl.pallas_call(
        _cast_all_kernel,
        grid=(steps,),
        in_specs=[spec(w) for w in ws],
        out_specs=[spec(w) for w in ws],
        out_shape=[jax.ShapeDtypeStruct(w.shape, BF16) for w in ws],
        compiler_params=_params(("arbitrary",)),
        name="cast",
    )(*ws)


def _inproj_kernel(x_ref, sh_ref, sc_ref, g_ref, w_ref, u_ref, qkv_ref, *, d_ssm):
    h = (_rms(x_ref[0], g_ref[...]) * (1.0 + sc_ref[0]) + sh_ref[0]).astype(BF16)
    n_total = w_ref.shape[1]
    for n in range(n_total // d_ssm):
        r = jnp.dot(h, w_ref[:, n * d_ssm:(n + 1) * d_ssm], preferred_element_type=F32)
        if n == 0:
            for cc in range(u_ref.shape[0]):
                u_ref[cc, 0] = r[cc * SSM_T:(cc + 1) * SSM_T]
        else:
            if n == 1:
                r = r * (HEAD_DIM ** -0.5 * LOG2E)
            qkv_ref[0, :, (n - 1) * d_ssm:n * d_ssm] = r.astype(BF16)


def _inproj(x, sh, sc, g, w_bf, d_ssm):
    bsz, L, D = x.shape
    n = w_bf.shape[1]
    tm = 512
    T = SSM_T
    return pl.pallas_call(
        functools.partial(_inproj_kernel, d_ssm=d_ssm),
        grid=(bsz, L // tm),
        in_specs=[pl.BlockSpec((1, tm, D), lambda b, i: (b, i, 0)),
                  pl.BlockSpec((1, 1, D), lambda b, i: (b, 0, 0)),
                  pl.BlockSpec((1, 1, D), lambda b, i: (b, 0, 0)),
                  pl.BlockSpec((1, D), lambda b, i: (0, 0)),
                  pl.BlockSpec((D, n), lambda b, i: (0, 0), pipeline_mode=pl.Buffered(1))],
        out_specs=[pl.BlockSpec((tm // T, 1, T, d_ssm), lambda b, i: (i, b, 0, 0)),
                   pl.BlockSpec((1, tm, n - d_ssm), lambda b, i: (b, i, 0))],
        out_shape=[jax.ShapeDtypeStruct((L // T, bsz, T, d_ssm), F32),
                   jax.ShapeDtypeStruct((bsz, L, n - d_ssm), BF16)],
        compiler_params=_params(("arbitrary", "arbitrary")),
        name="inproj",
    )(x, sh, sc, g, w_bf)


def _ssm_operators(a_re, a_im, log_dt, b_re, b_im, c_re, c_im):
    T = SSM_T
    G, P = a_re.shape[1], a_re.shape[2]
    H = b_re.shape[3]
    dt = jnp.exp(log_dt)[..., None]
    mag = jnp.exp(a_re * dt)
    lbr = mag * jnp.cos(a_im * dt)
    lbi = mag * jnp.sin(a_im * dt)
    den = a_re * a_re + a_im * a_im
    nr = lbr - 1.0
    fr = ((nr * a_re + lbi * a_im) / den)[:, :, None, :]
    fi = ((lbi * a_re - nr * a_im) / den)[:, :, None, :]
    b_re_t, b_im_t = jnp.swapaxes(b_re, 2, 3), jnp.swapaxes(b_im, 2, 3)
    bbr = fr * b_re_t - fi * b_im_t
    bbi = fr * b_im_t + fi * b_re_t
    prs, pis = [jnp.ones_like(lbr)], [jnp.zeros_like(lbi)]
    for _ in range(T):
        pr, pi = prs[-1], pis[-1]
        prs.append(pr * lbr - pi * lbi)
        pis.append(pr * lbi + pi * lbr)
    pr = jnp.stack(prs, axis=2)
    pi = jnp.stack(pis, axis=2)
    assert H == T
    rev = lambda x: x[:, ::-1]
    sections = [bbr[0], bbi[0], bbr[1], bbi[1], c_re[0], c_im[0], c_re[1], c_im[1],
                rev(pr[0, :, :T]), rev(pi[0, :, :T]), pr[1, :, :T], pi[1, :, :T],
                pr[0, :, 1:], pi[0, :, 1:], rev(pr[1, :, 1:]), rev(pi[1, :, 1:]),
                pr[0, :, :T], pi[0, :, :T]]
    m, wpair, vtpair = _ssm_prep(jnp.concatenate(sections, axis=1))
    a4 = jnp.stack([pr[0, :, T], pi[0, :, T], pr[1, :, T], pi[1, :, T]])
    a_chunk = jnp.transpose(a4.reshape(4, G // 2, 2, P), (1, 0, 2, 3)).reshape(1, G // 2 * 8 * P)
    return m, wpair, vtpair, a_chunk


def _nt_dot(a, b):
    return lax.dot_general(a, b, (((1,), (1,)), ((), ())), preferred_element_type=F32)


def _ssm_prep_kernel(pk_ref, m_ref, w_ref, vt_ref, mats_ref, rows_ref, kmats_ref, krows_ref):
    T = SSM_T
    H = T
    kw = T * H
    half = kmats_ref.shape[2] // 2
    lane = lax.broadcasted_iota(I32, (1, 2 * half), 1)
    cat = lambda xs: jnp.concatenate(xs, axis=-1)

    def group(g, carry):
        (bbr_f, bbi_f, bbr_b, bbi_b, cre_f, cim_f, cre_b, cim_b, qrf, qif, qrb, qib,
         rrf, rif, rrb, rib, krf, kif) = [pk_ref[g, i * T:(i + 1) * T, :] for i in range(18)]
        zero = jnp.zeros_like(bbr_f)
        even = g % 2 == 0

        def slots(v4):
            ev = cat([v4[0], zero, v4[1], zero, v4[2], zero, v4[3], zero])
            od = cat([zero, v4[0], zero, v4[1], zero, v4[2], zero, v4[3]])
            return jnp.where(even, ev, od)

        mats_ref[0] = slots([bbr_f, bbi_f, bbr_b, bbi_b])
        mats_ref[1] = slots([-bbi_f, bbr_f, -bbi_b, bbr_b])
        mats_ref[2] = slots([cre_f, -cim_f, cre_b, -cim_b])
        mats_ref[3] = slots([-cim_f, -cre_f, -cim_b, -cre_b])
        rows_ref[0] = cat([qrf] * 4 + [qrb] * 4)
        rows_ref[1] = cat([qif] * 4 + [qib] * 4)
        rows_ref[2] = cat([rrf] * 4 + [rrb] * 4)
        rows_ref[3] = cat([rif] * 4 + [rib] * 4)
        kmats_ref[0] = cat([bbr_f, -bbi_f, bbr_b, -bbi_b])
        kmats_ref[1] = cat([cre_f, cre_f, cre_b, cre_b])
        kmats_ref[2] = cat([-cim_f, cim_f, -cim_b, cim_b])
        krows_ref[0] = cat([krf, kif, qrb, qib])
        krows_ref[1] = cat([kif, krf, qib, qrb])

        pair, row0 = g // 2, (g % 2) * kw
        aw, bw, av, bv = [mats_ref[i] for i in range(4)]
        for s in range(T):
            r = pl.ds(pl.multiple_of(row0 + s * H, H), H)
            w_ref[pair, r, :] = (aw * rows_ref[0, pl.ds(s, 1), :] + bw * rows_ref[1, pl.ds(s, 1), :]).astype(BF16)
            vt_ref[pair, r, :] = (av * rows_ref[2, pl.ds(s, 1), :] + bv * rows_ref[3, pl.ds(s, 1), :]).astype(BF16)
        lhs, a4, b4 = [kmats_ref[i] for i in range(3)]
        blocks = []
        for i in range(2 * T):
            k = abs(i - (T - 1))
            if i == 2 * T - 1:
                blocks.append(jnp.zeros_like(a4))
                continue
            cp = a4 * krows_ref[0, pl.ds(k, 1), :] + b4 * krows_ref[1, pl.ds(k, 1), :]
            if i < T - 1:
                cp = jnp.where(lane >= half, cp, 0.0)
            elif i > T - 1:
                cp = jnp.where(lane < half, cp, 0.0)
            blocks.append(cp)
        rhs = jnp.concatenate(blocks, axis=0)
        lhs_hi = lhs.astype(BF16)
        lhs_lo = (lhs - lhs_hi.astype(F32)).astype(BF16)
        rhs_hi = rhs.astype(BF16)
        rhs_lo = (rhs - rhs_hi.astype(F32)).astype(BF16)
        z = _nt_dot(lhs_hi, rhs_hi) + _nt_dot(lhs_lo, rhs_hi) + _nt_dot(lhs_hi, rhs_lo)
        for s in range(T):
            off = (T - 1 - s) * H
            m_ref[g, s * H:(s + 1) * H, :] = z[:, off:off + kw].astype(BF16)
        return carry

    lax.fori_loop(0, pk_ref.shape[0], group, 0)


def _ssm_prep(packed):
    G, n_rows, P = packed.shape
    T = H = SSM_T
    gs = LANES // SSM_GROUP
    return pl.pallas_call(
        _ssm_prep_kernel,
        grid=(G // gs,),
        in_specs=[pl.BlockSpec((gs, n_rows, P), lambda j: (j, 0, 0))],
        out_specs=[pl.BlockSpec((gs, T * H, T * H), lambda j: (j, 0, 0)),
                   pl.BlockSpec((gs // 2, 2 * T * H, 8 * P), lambda j: (j, 0, 0)),
                   pl.BlockSpec((gs // 2, 2 * T * H, 8 * P), lambda j: (j, 0, 0))],
        out_shape=[jax.ShapeDtypeStruct((G, T * H, T * H), BF16),
                   jax.ShapeDtypeStruct((G // 2, 2 * T * H, 8 * P), BF16),
                   jax.ShapeDtypeStruct((G // 2, 2 * T * H, 8 * P), BF16)],
        scratch_shapes=[pltpu.VMEM((4, H, 8 * P), F32), pltpu.VMEM((4, T, 8 * P), F32),
                        pltpu.VMEM((3, H, 4 * P), F32), pltpu.VMEM((2, T, 4 * P), F32)],
        compiler_params=_params(("arbitrary",)),
        name="ssm_prep",
    )(packed)


def _slab_permutation():
    idx = jnp.arange(8 * LANES)
    t, g, h = idx // LANES, (idx % LANES) // SSM_GROUP, idx % SSM_GROUP
    dst = g * LANES + t * SSM_GROUP + h
    return (dst[:, None] == jnp.arange(8 * LANES)[None, :]).astype(BF16)


def _chunk_scan(s_ref, a_ref, *, bsz, n_pairs):
    n_it = s_ref.shape[0] // (2 * bsz)
    cols = [[pl.ds((4 * j + q) * LANES, LANES) for q in range(4)] for j in range(n_pairs)]
    decay = [[a_ref[:, c] for c in cols[j]] for j in range(n_pairs)]

    def step(i, carry):
        rf = pl.ds(pl.multiple_of(i * 2 * bsz, 2 * bsz), 2 * bsz)
        rb = pl.ds(pl.multiple_of((n_it - 1 - i) * 2 * bsz, 2 * bsz), 2 * bsz)
        out = []
        for j in range(n_pairs):
            ar_f, ai_f, ar_b, ai_b = decay[j]
            xr, xi, yr, yi = carry[4 * j:4 * j + 4]
            sr, si = s_ref[rf, cols[j][0]], s_ref[rf, cols[j][1]]
            xr1 = ar_f * xr - ai_f * xi + sr[:bsz]
            xi1 = ar_f * xi + ai_f * xr + si[:bsz]
            s_ref[rf, cols[j][0]] = jnp.concatenate([xr, xr1], axis=0)
            s_ref[rf, cols[j][1]] = jnp.concatenate([xi, xi1], axis=0)
            xr2 = ar_f * xr1 - ai_f * xi1 + sr[bsz:]
            xi2 = ar_f * xi1 + ai_f * xr1 + si[bsz:]
            tr, ti = s_ref[rb, cols[j][2]], s_ref[rb, cols[j][3]]
            yr1 = ar_b * yr - ai_b * yi + tr[bsz:]
            yi1 = ar_b * yi + ai_b * yr + ti[bsz:]
            s_ref[rb, cols[j][2]] = jnp.concatenate([yr1, yr], axis=0)
            s_ref[rb, cols[j][3]] = jnp.concatenate([yi1, yi], axis=0)
            yr2 = ar_b * yr1 - ai_b * yi1 + tr[:bsz]
            yi2 = ar_b * yi1 + ai_b * yr1 + ti[:bsz]
            out += [xr2, xi2, yr2, yi2]
        return tuple(out)

    z = jnp.zeros((bsz, LANES), F32)
    lax.fori_loop(0, n_it, step, (z,) * (4 * n_pairs))


def _ssm_kernel(u_ref, d_ref, perm_ref, iperm_ref, w_ref, m_ref, v_ref, a_ref, y_ref,
                ug_ref, s_ref, yg_ref, *, bsz):
    T = SSM_T
    R = u_ref.shape[0] // T
    n_groups = LANES // SSM_GROUP
    kw = T * SSM_GROUP
    for tq in range(T // 8):
        cat = jnp.concatenate([u_ref[pl.ds(tq * 8 + t, R, stride=T), :].astype(BF16) for t in range(8)], axis=-1)
        grouped = jnp.dot(cat, perm_ref[...], preferred_element_type=F32).astype(BF16)
        for g in range(n_groups):
            ug_ref[g, :, tq * LANES:(tq + 1) * LANES] = grouped[:, g * LANES:(g + 1) * LANES]
    for pr in range(n_groups // 2):
        u2 = jnp.concatenate([ug_ref[2 * pr], ug_ref[2 * pr + 1]], axis=-1)
        s_ref[:, pr * 4 * LANES:(pr + 1) * 4 * LANES] = jnp.dot(u2, w_ref[pr], preferred_element_type=F32)
    _chunk_scan(s_ref, a_ref, bsz=bsz, n_pairs=n_groups // 2)
    for pr in range(n_groups // 2):
        inter = _nt_dot(s_ref[:, pr * 4 * LANES:(pr + 1) * 4 * LANES].astype(BF16), v_ref[pr])
        for gl in range(2):
            g = 2 * pr + gl
            yg_ref[g] = (jnp.dot(ug_ref[g], m_ref[g], preferred_element_type=F32)
                         + inter[:, gl * kw:(gl + 1) * kw])
    for tq in range(T // 8):
        ycat = jnp.concatenate([yg_ref[g, :, tq * LANES:(tq + 1) * LANES] for g in range(n_groups)], axis=-1)
        back = jnp.dot(ycat.astype(BF16), iperm_ref[...], preferred_element_type=F32)
        for t in range(8):
            rows = pl.ds(tq * 8 + t, R, stride=T)
            y_ref[rows, :] = back[:, t * LANES:(t + 1) * LANES] + d_ref[...] * u_ref[rows, :]


def _ssm(u_tok, d_skip, ops, bsz):
    m, wpair, vpair, a_chunk = ops
    N, d_ssm = u_tok.shape
    T = SSM_T
    R = N // T
    n_groups = LANES // SSM_GROUP
    perm = _slab_permutation()
    const = lambda shape: pl.BlockSpec(shape, lambda j: (0,) * len(shape))
    return pl.pallas_call(
        functools.partial(_ssm_kernel, bsz=bsz),
        grid=(d_ssm // LANES,),
        in_specs=[pl.BlockSpec((N, LANES), lambda j: (0, j)),
                  pl.BlockSpec((1, LANES), lambda j: (0, j)),
                  const(perm.shape), const(perm.shape),
                  pl.BlockSpec((n_groups // 2,) + wpair.shape[1:], lambda j: (j, 0, 0)),
                  pl.BlockSpec((n_groups,) + m.shape[1:], lambda j: (j, 0, 0)),
                  pl.BlockSpec((n_groups // 2,) + vpair.shape[1:], lambda j: (j, 0, 0)),
                  pl.BlockSpec((1, n_groups // 2 * 4 * LANES), lambda j: (0, j))],
        out_specs=pl.BlockSpec((N, LANES), lambda j: (0, j)),
        out_shape=jax.ShapeDtypeStruct((N, d_ssm), F32),
        scratch_shapes=[pltpu.VMEM((n_groups, R, T * SSM_GROUP), BF16),
                        pltpu.VMEM((R, n_groups // 2 * 4 * LANES), F32),
                        pltpu.VMEM((n_groups, R, T * SSM_GROUP), F32)],
        compiler_params=_params(("arbitrary",)),
        name="ssm",
    )(u_tok, d_skip, perm, perm.T, wpair, m, vpair, a_chunk)


def _attn_tables(rpb, rows):
    kh = min(WIN_H_MAX, rows)
    H = rpb.shape[0]
    w = jnp.arange(GRID_W)
    col_start = jnp.clip(w - WIN_W // 2, 0, GRID_W - WIN_W)
    col_mask = (w[None, :] >= col_start[:, None]) & (w[None, :] < col_start[:, None] + WIN_W)
    dc = jnp.clip(w[None, :] - w[:, None] + (WIN_W - 1), 0, 2 * WIN_W - 2)
    col_sel = (dc[:, :, None] == jnp.arange(2 * WIN_W - 1)[None, None, :]).astype(F32)
    tile = jnp.einsum('hdc,qkc->hdqk', rpb, col_sel, precision=lax.Precision.HIGHEST)
    tile = jnp.where(col_mask[None, None], tile * LOG2E, NEG)
    lo_pad, hi_pad = ATT_ROWS, ATT_WIN - kh + 1
    tile = jnp.pad(tile, ((0, 0), (lo_pad, hi_pad), (0, 0), (0, 0)))
    n_dr = tile.shape[1] - 1
    tiles = jnp.concatenate([tile[:, :-1], tile[:, 1:]], axis=-1).reshape(H // 2, 2, n_dr, GRID_W, 2 * GRID_W)
    i = jnp.arange(ATT_ROWS)
    last_union = rows - ATT_WIN
    last_rel = (rows - kh) - last_union
    rel = jnp.stack([jnp.zeros_like(i), i, jnp.full_like(i, last_rel)])
    j = jnp.arange(ATT_WIN)
    valid = (j[None, None, :] >= rel[:, :, None]) & (j[None, None, :] < rel[:, :, None] + kh)
    masks = jnp.where(valid, 0.0, NEG).astype(F32)
    masks = jnp.repeat(masks.reshape(3 * ATT_ROWS * ATT_WIN // 2, 2), GRID_W, axis=-1)
    return tiles, masks.reshape(-1, 1, 2 * GRID_W)


def _attn_kernel(q_ref, k_ref, v_ref, t_ref, m_ref, o_ref, *, rows, kh):
    n_blocks = rows // ATT_ROWS
    nq = ATT_ROWS * GRID_W
    nk = ATT_WIN * GRID_W
    n_pairs = ATT_WIN // 2
    pw = 2 * GRID_W
    lane = lax.broadcasted_iota(I32, (1, 2 * HEAD_DIM), 1)
    head_lanes = [lane < HEAD_DIM, lane >= HEAD_DIM]
    rel = [[0] * ATT_ROWS, list(range(ATT_ROWS)), [(rows - kh) - (rows - ATT_WIN)] * ATT_ROWS]

    def geometry(rb):
        pattern = 0 if rb == 0 else (2 if rb == n_blocks - 1 else 1)
        union = min(max(rb * ATT_ROWS - kh // 2, 0), rows - ATT_WIN)
        dr0 = union - rb * ATT_ROWS + (WIN_H_MAX - 1) + ATT_ROWS
        return pattern, dr0, pl.ds(rb * nq, nq), pl.ds(union * GRID_W, nk)

    def scores(rb, hl):
        _, _, q_rows, k_rows = geometry(rb)
        qm = jnp.where(head_lanes[hl], q_ref[0, q_rows, :], 0.0).astype(BF16)
        return _nt_dot(qm, k_ref[0, k_rows, :])

    def softmax(rb, hl, s):
        pattern, dr0, _, _ = geometry(rb)
        probs, denoms = [], []
        for i in range(ATT_ROWS):
            first, last = rel[pattern][i], rel[pattern][i] + kh - 1
            pairs = range(first // 2, last // 2 + 1)
            bias = []
            for jp in pairs:
                tile = t_ref[0, hl, dr0 + 2 * jp - i]
                if 2 * jp < first or 2 * jp + 1 > last:
                    tile = tile + m_ref[(pattern * ATT_ROWS + i) * n_pairs + jp]
                bias.append(tile)
            si = s[i * GRID_W:(i + 1) * GRID_W, pairs[0] * pw:(pairs[-1] + 1) * pw] + jnp.concatenate(bias, axis=-1)
            pi = jnp.exp2(si - jnp.max(si, axis=-1, keepdims=True))
            denoms.append(jnp.sum(pi, axis=-1, keepdims=True))
            pieces = [pi.astype(BF16)]
            if pairs[0] > 0:
                pieces.insert(0, jnp.zeros((GRID_W, pairs[0] * pw), BF16))
            if pairs[-1] + 1 < n_pairs:
                pieces.append(jnp.zeros((GRID_W, (n_pairs - 1 - pairs[-1]) * pw), BF16))
            probs.append(jnp.concatenate(pieces, axis=-1))
        return jnp.concatenate(probs, axis=0), jnp.concatenate(denoms, axis=0)

    def weighted_values(rb, hl, p, denom):
        _, _, _, k_rows = geometry(rb)
        vm = jnp.where(head_lanes[hl], v_ref[0, k_rows, :], 0.0).astype(BF16)
        return jnp.dot(p, vm, preferred_element_type=F32) / denom

    passes = [(rb, hl) for rb in range(n_blocks) for hl in range(2)]
    ahead = 1
    queue = [scores(*passes[k]) for k in range(ahead)]
    pending = None
    partial = {}

    def finish(item):
        (rb, hl), p, denom = item
        o = weighted_values(rb, hl, p, denom)
        if hl == 0:
            partial[rb] = o
        else:
            o_ref[0, geometry(rb)[2], :] = partial.pop(rb) + o

    for n, (rb, hl) in enumerate(passes):
        s = queue.pop(0)
        if n + ahead < len(passes):
            queue.append(scores(*passes[n + ahead]))
        p, denom = softmax(rb, hl, s)
        if pending is not None:
            finish(pending)
        pending = ((rb, hl), p, denom)
    finish(pending)


def _attn(qkv, tiles, masks):
    bsz, L, n3 = qkv.shape
    d = n3 // 3
    rows = L // GRID_W
    kh = min(WIN_H_MAX, rows)
    assert rows % ATT_ROWS == 0 and rows // ATT_ROWS >= 3 and ATT_ROWS == kh // 2
    assert ATT_WIN % 2 == 0 and kh + ATT_ROWS - 1 <= ATT_WIN <= rows
    n_hp = d // (2 * HEAD_DIM)
    blk = pl.BlockSpec((1, L, 2 * HEAD_DIM), lambda b, hp: (b, 0, hp))
    return pl.pallas_call(
        functools.partial(_attn_kernel, rows=rows, kh=kh),
        grid=(bsz, n_hp),
        in_specs=[blk,
                  pl.BlockSpec((1, L, 2 * HEAD_DIM), lambda b, hp: (b, 0, n_hp + hp)),
                  pl.BlockSpec((1, L, 2 * HEAD_DIM), lambda b, hp: (b, 0, 2 * n_hp + hp)),
                  pl.BlockSpec((1,) + tiles.shape[1:], lambda b, hp: (hp, 0, 0, 0, 0)),
                  pl.BlockSpec(masks.shape, lambda b, hp: (0, 0, 0))],
        out_specs=blk,
        out_shape=jax.ShapeDtypeStruct((bsz, L, d), F32),
        compiler_params=_params(("arbitrary", "arbitrary")),
        name="attn",
    )(qkv, qkv, qkv, tiles, masks)


def _mix_kernel(ys_ref, wglu_ref, bglu_ref, gs_ref, ya_ref, ga_ref, wout_ref,
                x_ref, gt_ref, gf_ref, sh_ref, sc_ref, wr_ref,
                x1_ref, h2_ref, lg_ref):
    d_ssm = ys_ref.shape[3]
    y = ys_ref[:, 0].reshape(-1, d_ssm)
    z = jax.nn.gelu(y)
    gate = _sigmoid(jnp.dot(z.astype(BF16), wglu_ref[...], preferred_element_type=F32) + bglu_ref[...])
    a = _rms(z * gate, gs_ref[...]).astype(BF16)
    t = _rms(ya_ref[0], ga_ref[...]).astype(BF16)
    mixed = (jnp.dot(a, wout_ref[:d_ssm, :], preferred_element_type=F32)
             + jnp.dot(t, wout_ref[d_ssm:, :], preferred_element_type=F32))
    x1 = x_ref[0] + gt_ref[0] * mixed
    x1_ref[0] = x1
    h2 = _rms(x1, gf_ref[...]) * (1.0 + sc_ref[0]) + sh_ref[0]
    h2_hi = h2.astype(BF16)
    h2_ref[0] = h2_hi
    h2_lo = (h2 - h2_hi.astype(F32)).astype(BF16)
    wr = wr_ref[...]
    wr_hi = wr.astype(BF16)
    wr_lo = (wr - wr_hi.astype(F32)).astype(BF16)
    n_e = wr.shape[1]
    tm = h2.shape[0]
    r = jnp.dot(jnp.concatenate([h2_hi, h2_lo], axis=0), jnp.concatenate([wr_hi, wr_lo], axis=1),
                preferred_element_type=F32)
    lg_ref[0] = r[:tm, :n_e] + r[:tm, n_e:] + r[tm:, :n_e]


def _mix(ys, wglu_bf, b_glu, g_ssm, ya, g_attn, wout_bf, x, gt1, g_ffn, sh2, sc2, w_router):
    bsz, L, D = x.shape
    d_ssm = ys.shape[3]
    E = w_router.shape[1]
    tm = 512
    T = SSM_T
    tile = lambda n: pl.BlockSpec((1, tm, n), lambda b, i: (b, i, 0))
    per_b = lambda n: pl.BlockSpec((1, 1, n), lambda b, i: (b, 0, 0))
    full = lambda r, n: pl.BlockSpec((r, n), lambda b, i: (0, 0), pipeline_mode=pl.Buffered(1))
    return pl.pallas_call(
        _mix_kernel,
        grid=(bsz, L // tm),
        in_specs=[pl.BlockSpec((tm // T, 1, T, d_ssm), lambda b, i: (i, b, 0, 0)),
                  full(d_ssm, d_ssm), full(1, d_ssm), full(1, d_ssm),
                  tile(d_ssm), full(1, d_ssm), full(D, D),
                  tile(D), per_b(D), full(1, D), per_b(D), per_b(D), full(D, E)],
        out_specs=[tile(D), tile(D), tile(E)],
        out_shape=[jax.ShapeDtypeStruct((bsz, L, D), F32),
                   jax.ShapeDtypeStruct((bsz, L, D), BF16),
                   jax.ShapeDtypeStruct((bsz, L, E), F32)],
        compiler_params=_params(("arbitrary", "arbitrary")),
        name="mix",
    )(ys, wglu_bf, b_glu, g_ssm, ya, g_attn, wout_bf, x, gt1, g_ffn, sh2, sc2, w_router)


TOPK_BISECTIONS = 160


def _topk_kernel(lg_ref, aff_ref, pos_ref, *, cap):
    lg = lg_ref[...]
    bsz, E, L = lg.shape
    e = jnp.exp(lg - jnp.max(lg, axis=1, keepdims=True))
    aff3 = e / jnp.sum(e, axis=1, keepdims=True)
    aff_ref[...] = aff3
    aff = aff3.reshape(bsz * E, L)

    def count(mask):
        return jnp.sum(jnp.where(mask, 1.0, 0.0), axis=-1, keepdims=True)

    def midpoint(lo, hi):
        return lo + (hi - lo) * 0.5

    def unresolved(carry):
        lo, hi, it = carry
        mid = midpoint(lo, hi)
        open_rows = jnp.max(jnp.where((mid > lo) & (mid < hi), 1.0, 0.0))
        return jnp.logical_and(it < TOPK_BISECTIONS, open_rows > 0.0)

    def halve(carry):
        lo, hi, it = carry
        mid = midpoint(lo, hi)
        ok = count(aff >= mid) >= cap
        return jnp.where(ok, mid, lo), jnp.where(ok, hi, mid), it + 1

    lo0 = jnp.zeros((bsz * E, 1), F32)
    thr, _, _ = lax.while_loop(unresolved, halve, (lo0, lo0 + 2.0, jnp.int32(0)))
    gt = aff > thr
    eq = aff == thr
    need = cap - count(gt)
    blk = LANES
    tri = jnp.where(lax.broadcasted_iota(I32, (blk, blk), 0) < lax.broadcasted_iota(I32, (blk, blk), 1),
                    1.0, 0.0).astype(BF16)

    def prefix_count(mask):
        ones = jnp.where(mask, 1.0, 0.0)
        run = jnp.zeros((bsz * E, 1), F32)
        outs = []
        for j in range(L // blk):
            piece = ones[:, j * blk:(j + 1) * blk]
            outs.append(jnp.dot(piece.astype(BF16), tri, preferred_element_type=F32) + run)
            run = run + jnp.sum(piece, axis=-1, keepdims=True)
        return jnp.concatenate(outs, axis=-1)

    sel = gt | (eq & (prefix_count(eq) < need))
    pos = prefix_count(sel)
    pos_ref[...] = jnp.where(sel, pos.astype(I32), -1).reshape(bsz, E, L)


def _topk(lg_t, cap):
    bsz, E, L = lg_t.shape
    spec = pl.BlockSpec((bsz, E, L), lambda i: (0, 0, 0))
    return pl.pallas_call(
        functools.partial(_topk_kernel, cap=cap),
        grid=(1,),
        in_specs=[spec],
        out_specs=[spec, spec],
        out_shape=[jax.ShapeDtypeStruct((bsz, E, L), F32), jax.ShapeDtypeStruct((bsz, E, L), I32)],
        compiler_params=_params(("arbitrary",)),
        name="topk",
    )(lg_t)


def _gather_kernel(pos_ref, aff_ref, h_ref, xe_ref, as_ref, *, cap):
    L = h_ref.shape[1]
    n_e = pos_ref.shape[1]
    slot = lax.broadcasted_iota(I32, (cap, L), 0)
    hits = [pos_ref[0, e] == slot for e in range(n_e)]
    onehot = jnp.concatenate([jnp.where(h, 1.0, 0.0).astype(BF16) for h in hits], axis=0)
    xe = jnp.dot(onehot, h_ref[0], preferred_element_type=F32).astype(BF16)
    for e in range(n_e):
        xe_ref[e] = xe[e * cap:(e + 1) * cap]
        as_ref[e] = jnp.sum(jnp.where(hits[e], aff_ref[0, e], 0.0), axis=-1, keepdims=True)


def _gather(pos, aff, h2, cap):
    bsz, E, L = pos.shape
    D = h2.shape[2]
    pos4 = pos.reshape(bsz, E, 1, L)
    aff4 = aff.reshape(bsz, E, 1, L)
    ge = 4
    return pl.pallas_call(
        functools.partial(_gather_kernel, cap=cap),
        grid=(bsz, E // ge),
        in_specs=[pl.BlockSpec((1, ge, 1, L), lambda b, e: (b, e, 0, 0)),
                  pl.BlockSpec((1, ge, 1, L), lambda b, e: (b, e, 0, 0)),
                  pl.BlockSpec((1, L, D), lambda b, e: (b, 0, 0))],
        out_specs=[pl.BlockSpec((ge, cap, D), lambda b, e: (e, b, 0)),
                   pl.BlockSpec((ge, cap, 1), lambda b, e: (e, b, 0))],
        out_shape=[jax.ShapeDtypeStruct((E, bsz * cap, D), BF16),
                   jax.ShapeDtypeStruct((E, bsz * cap, 1), F32)],
        compiler_params=_params(("arbitrary", "arbitrary")),
        name="gather",
    )(pos4, aff4, h2)


def _ffn_kernel(xe_ref, wg_ref, wu_ref, wd_ref, as_ref, ye_ref, acc_ref):
    f = pl.program_id(1)

    @pl.when(f == 0)
    def _():
        acc_ref[...] = jnp.zeros_like(acc_ref)

    x = xe_ref[0]
    g = jnp.dot(x, wg_ref[0].astype(BF16), preferred_element_type=F32)
    up = jnp.dot(x, wu_ref[0].astype(BF16), preferred_element_type=F32)
    hid = (g * _sigmoid(g) * up).astype(BF16)
    acc_ref[...] += jnp.dot(hid, wd_ref[0].astype(BF16), preferred_element_type=F32)

    @pl.when(f == pl.num_programs(1) - 1)
    def _():
        bsz, _, cap, _ = ye_ref.shape
        y = (acc_ref[...] * as_ref[0]).astype(BF16)
        for b in range(bsz):
            ye_ref[b, 0] = y[b * cap:(b + 1) * cap]


def _ffn(xe, w_gate, w_up, w_down, aff_slot, bsz):
    E, R, D = xe.shape
    F = w_gate.shape[2]
    cap = R // bsz
    tf = 256
    return pl.pallas_call(
        _ffn_kernel,
        grid=(E, F // tf),
        in_specs=[pl.BlockSpec((1, R, D), lambda e, f: (e, 0, 0)),
                  pl.BlockSpec((1, D, tf), lambda e, f: (e, 0, f)),
                  pl.BlockSpec((1, D, tf), lambda e, f: (e, 0, f)),
                  pl.BlockSpec((1, tf, D), lambda e, f: (e, f, 0)),
                  pl.BlockSpec((1, R, 1), lambda e, f: (e, 0, 0))],
        out_specs=pl.BlockSpec((bsz, 1, cap, D), lambda e, f: (0, e, 0, 0)),
        out_shape=jax.ShapeDtypeStruct((bsz, E, cap, D), BF16),
        scratch_shapes=[pltpu.VMEM((R, D), F32)],
        compiler_params=_params(("arbitrary", "arbitrary")),
        name="ffn",
    )(xe, w_gate, w_up, w_down, aff_slot)


def _combine_kernel(pos_ref, ye_ref, x1_ref, gt_ref, g_ref, o_ref, *, final_norm):
    pos_t = pos_ref[0]
    n_e, cap, d = ye_ref.shape[1:]
    slot = lax.broadcasted_iota(I32, (pos_t.shape[0], cap), 1)
    onehot = jnp.concatenate([jnp.where(pos_t[:, e:e + 1] == slot, 1.0, 0.0).astype(BF16) for e in range(n_e)],
                             axis=-1)
    moe = jnp.dot(onehot, ye_ref[0].reshape(n_e * cap, d), preferred_element_type=F32)
    o = x1_ref[0] + gt_ref[0] * moe
    o_ref[0] = _rms(o, g_ref[...]) if final_norm else o


def _combine(pos_t, ye, x1, gt2, g_final, final_norm):
    bsz, L, E = pos_t.shape
    D = x1.shape[2]
    cap = ye.shape[2]
    tl = 512
    return pl.pallas_call(
        functools.partial(_combine_kernel, final_norm=final_norm),
        grid=(bsz, L // tl),
        in_specs=[pl.BlockSpec((1, tl, E), lambda b, i: (b, i, 0)),
                  pl.BlockSpec((1, E, cap, D), lambda b, i: (b, 0, 0, 0), pipeline_mode=pl.Buffered(1)),
                  pl.BlockSpec((1, tl, D), lambda b, i: (b, i, 0)),
                  pl.BlockSpec((1, 1, D), lambda b, i: (b, 0, 0)),
                  pl.BlockSpec((1, D), lambda b, i: (0, 0))],
        out_specs=pl.BlockSpec((1, tl, D), lambda b, i: (b, i, 0)),
        out_shape=jax.ShapeDtypeStruct((bsz, L, D), F32),
        compiler_params=_params(("arbitrary", "arbitrary")),
        name="combine",
    )(pos_t, ye, x1, gt2, g_final)


def kernel(x, c, w_ada, b_ada, g_mix, w_in, ssm_a_re, ssm_a_im, ssm_log_dt, ssm_b_re, ssm_b_im,
           ssm_c_re, ssm_c_im, ssm_d, w_glu, b_glu, rpb, g_ssm_out, g_attn_out, w_out, g_ffn,
           w_router, w_gate, w_up, w_down, g_final):
    bsz, L, D = x.shape
    depth = w_ada.shape[0]
    d_ssm = ssm_d.shape[1]
    E = w_router.shape[2]
    cap = CAPACITY_FACTOR * L // E
    rows = L // GRID_W
    c8 = jnp.zeros((8, D), F32).at[:bsz].set(c)
    row = lambda v: v.reshape(1, -1)
    for layer in range(depth):
        mod = _ada(c8, w_ada[layer], row(b_ada[layer]))[:bsz]
        sh1, sc1, gt1, sh2, sc2, gt2 = [m.reshape(bsz, 1, D) for m in jnp.split(mod, 6, axis=-1)]

        w_in_bf, w_glu_bf, w_out_bf = _cast_all_bf16([w_in[layer], w_glu[layer], w_out[layer]])
        u, qkv = _inproj(x, sh1, sc1, row(g_mix[layer]), w_in_bf, d_ssm)
        ops = _ssm_operators(ssm_a_re[layer], ssm_a_im[layer], ssm_log_dt[layer], ssm_b_re[layer],
                             ssm_b_im[layer], ssm_c_re[layer], ssm_c_im[layer])
        y_ssm = _ssm(u.reshape(-1, d_ssm), row(ssm_d[layer]), ops, bsz).reshape(u.shape)
        y_attn = _attn(qkv, *_attn_tables(rpb[layer], rows))
        x1, h2, logits = _mix(y_ssm, w_glu_bf, row(b_glu[layer]),
                              row(g_ssm_out[layer]), y_attn, row(g_attn_out[layer]), w_out_bf,
                              x, gt1, row(g_ffn[layer]), sh2, sc2, w_router[layer])

        aff, pos = _topk(jnp.swapaxes(logits, 1, 2), cap)
        xe, aff_slot = _gather(pos, aff, h2, cap)
        ye = _ffn(xe, w_gate[layer], w_up[layer], w_down[layer], aff_slot, bsz)
        x = _combine(jnp.swapaxes(pos, 1, 2), ye, x1, gt2, row(g_final), layer == depth - 1)
    return x
```
